```python
import math
import jax
import jax.numpy as jnp
from jax import lax
import numpy as np

D_MODEL = 2048
BATCH = 4
SEQ = 4096
DEPTH = 4

GRID_W = 64
CTX_LEN = 256
N_MIXERS = 3
EXPAND = 2
D_INNER = EXPAND * D_MODEL
CONV_W = 3
DIFF_HEADS = D_INNER // 128
DIFF_HEAD_DIM = 64
DIFF_V_DIM = 2 * DIFF_HEAD_DIM
WIN_HEAD_DIM = 128
WIN_HEADS = D_INNER // WIN_HEAD_DIM
WIN_KV_HEADS = 8
WIN_GROUP = WIN_HEADS // WIN_KV_HEADS
WINDOW = 128
BLOCK = 128
ROPE_BASE = 10000.0
EPS = 1e-6
NEG_INF = -1e30

kernel_name = "hybrid_interleaved_dit_block"


def rms_norm(t, g, eps=EPS):
    tf = t.astype(jnp.float32)
    y = tf * lax.rsqrt(jnp.mean(tf * tf, axis=-1, keepdims=True) + eps)
    return (y * g.astype(jnp.float32)).astype(t.dtype)


def modulation(cvec, w_mod, b_mod):
    m = jax.nn.silu(cvec) @ w_mod + b_mod
    return jnp.split(m, 3, axis=-1)


def axial_rope_tables(rows, head_dim, dtype):
    row = jnp.repeat(jnp.arange(rows), GRID_W).astype(jnp.float32)
    col = jnp.tile(jnp.arange(GRID_W), rows).astype(jnp.float32)
    n_freq = head_dim // 4
    inv_freq = ROPE_BASE ** (-(jnp.arange(n_freq, dtype=jnp.float32) / n_freq))
    ang = jnp.concatenate([row[:, None] * inv_freq, col[:, None] * inv_freq], axis=-1)
    return jnp.cos(ang).astype(dtype), jnp.sin(ang).astype(dtype)


def apply_axial_rope(t, cos, sin):
    d = t.shape[-1]
    q4 = d // 4
    tr = t.reshape(t.shape[:-1] + (2, 2, q4))
    t1, t2 = tr[..., 0, :], tr[..., 1, :]
    bshape = (cos.shape[0],) + (1,) * (t.ndim - 3) + (2, q4)
    cs, sn = cos.reshape(bshape), sin.reshape(bshape)
    return jnp.stack([t1 * cs - t2 * sn, t1 * sn + t2 * cs], axis=-2).reshape(t.shape)


def short_conv_branch(h, w_in, conv_w, conv_b, w_out):
    b_gate, c_gate, xt, z = jnp.split(h @ w_in, 4, axis=-1)
    u = c_gate * xt
    n = u.shape[1]
    half = CONV_W // 2
    up = jnp.pad(u, ((0, 0), (half, half), (0, 0)))
    y = sum(up[:, k:k + n] * conv_w[k] for k in range(CONV_W)) + conv_b
    return (b_gate * y * jax.nn.silu(z)) @ w_out


def short_conv_mixer(h, hc, params, need_ctx_out):
    w_in, conv_w, conv_b, w_out = params
    y = short_conv_branch(h, w_in, conv_w, conv_b, w_out)
    yc = short_conv_branch(hc, w_in, conv_w, conv_b, w_out) if need_ctx_out else None
    return y, yc


def _diff_heads(t, g, rope):
    B, L = t.shape[:2]
    t = rms_norm(t.reshape(B, L, DIFF_HEADS, 2, DIFF_HEAD_DIM), g).swapaxes(2, 3)
    if rope is not None:
        t = apply_axial_rope(t, *rope)
    return t


def diff_project(h, w_in, q_norm, k_norm, rope, kv_only):
    B, L, _ = h.shape
    if kv_only:
        k, v = jnp.split(h @ w_in[:, D_INNER:3 * D_INNER], 2, axis=-1)
        q = z = None
    else:
        q, k, v, z = jnp.split(h @ w_in, 4, axis=-1)
        q = _diff_heads(q, q_norm, rope)
    k = _diff_heads(k, k_norm, rope)
    v = v.reshape(B, L, DIFF_HEADS, DIFF_V_DIM)
    return q, k, v, z


def diff_attend(q, k, v, lam):
    s = jnp.einsum("bqmhd,bkmhd->bmhqk", q, k).astype(jnp.float32) * (DIFF_HEAD_DIM ** -0.5)
    p = jax.nn.softmax(s, axis=-1)
    w = p[:, 0] - lam * p[:, 1]
    return jnp.einsum("bhqk,bkhe->bqhe", w.astype(v.dtype), v)


def diff_finish(o, z, sub_norm, lam_init, w_out):
    B, L = o.shape[:2]
    o = rms_norm(o, sub_norm) * (1.0 - lam_init)
    return (o.reshape(B, L, D_INNER) * jax.nn.silu(z)) @ w_out


def diff_attention_mixer(h, hc, params, layer_idx, need_ctx_out, rope):
    w_in, q_norm, k_norm, lq1, lk1, lq2, lk2, sub_norm, w_out = params
    f32 = jnp.float32
    lam_init = 0.8 - 0.6 * math.exp(-0.3 * layer_idx)
    lam = (jnp.exp(jnp.sum(lq1.astype(f32) * lk1.astype(f32)))
           - jnp.exp(jnp.sum(lq2.astype(f32) * lk2.astype(f32))) + lam_init)
    B, L, _ = h.shape
    q, k, v, z = diff_project(h, w_in, q_norm, k_norm, rope, False)
    qc, kc, vc, zc = diff_project(hc, w_in, q_norm, k_norm, None, not need_ctx_out)
    k_all = jnp.concatenate([k, kc], axis=1)
    v_all = jnp.concatenate([v, vc], axis=1)

    def block(i):
        qb = lax.dynamic_slice_in_dim(q, i * BLOCK, BLOCK, axis=1)
        return diff_attend(qb, k_all, v_all, lam)

    o = lax.map(block, jnp.arange(L // BLOCK))
    o = jnp.moveaxis(o, 0, 1).reshape(B, L, DIFF_HEADS, DIFF_V_DIM)
    y = diff_finish(o, z, sub_norm, lam_init, w_out)
    yc = None
    if need_ctx_out:
        oc = diff_attend(qc, kc, vc, lam)
        yc = diff_finish(oc, zc, sub_norm, lam_init, w_out)
    return y, yc


def win_project(h, w_in, q_norm, k_norm, rope, kv_only):
    B, L, _ = h.shape
    kv_w = WIN_KV_HEADS * WIN_HEAD_DIM
    if kv_only:
        k, v = jnp.split(h @ w_in[:, D_INNER:D_INNER + 2 * kv_w], 2, axis=-1)
        q = z = None
    else:
        q, k, v, z = jnp.split(h @ w_in, [D_INNER, D_INNER + kv_w, D_INNER + 2 * kv_w], axis=-1)
        q = rms_norm(q.reshape(B, L, WIN_KV_HEADS, WIN_GROUP, WIN_HEAD_DIM), q_norm)
        if rope is not None:
            q = apply_axial_rope(q, *rope)
    k = rms_norm(k.reshape(B, L, WIN_KV_HEADS, WIN_HEAD_DIM), k_norm)
    if rope is not None:
        k = apply_axial_rope(k, *rope)
    v = v.reshape(B, L, WIN_KV_HEADS, WIN_HEAD_DIM)
    return q, k, v, z


def window_gqa_mixer(h, hc, params, need_ctx_out, rope):
    w_in, q_norm, k_norm, sink, w_out = params
    f32 = jnp.float32
    B, L, _ = h.shape
    scale = WIN_HEAD_DIM ** -0.5
    q, k, v, z = win_project(h, w_in, q_norm, k_norm, rope, False)
    qc, kc, vc, zc = win_project(hc, w_in, q_norm, k_norm, None, not need_ctx_out)
    n_ctx = kc.shape[1]
    sink_logit = sink.astype(f32).reshape(WIN_KV_HEADS, WIN_GROUP, 1, 1)
    pad = ((0, 0), (BLOCK, BLOCK), (0, 0), (0, 0))
    kp, vp = jnp.pad(k, pad), jnp.pad(v, pad)
    band = 3 * BLOCK
    offset = jnp.arange(BLOCK)[:, None] + BLOCK - jnp.arange(band)[None, :]
    in_window = jnp.abs(offset) <= WINDOW

    def block(i):
        qb = lax.dynamic_slice_in_dim(q, i * BLOCK, BLOCK, axis=1)
        kb = lax.dynamic_slice_in_dim(kp, i * BLOCK, band, axis=1)
        vb = lax.dynamic_slice_in_dim(vp, i * BLOCK, band, axis=1)
        kpos = i * BLOCK - BLOCK + jnp.arange(band)
        valid = in_window & ((kpos >= 0) & (kpos < L))[None, :]
        s_band = jnp.einsum("bqngd,bknd->bngqk", qb, kb).astype(f32) * scale
        s_band = jnp.where(valid, s_band, NEG_INF)
        s_ctx = jnp.einsum("bqngd,bknd->bngqk", qb, kc).astype(f32) * scale
        s_sink = jnp.broadcast_to(sink_logit, s_band.shape[:-1] + (1,))
        p = jax.nn.softmax(jnp.concatenate([s_band, s_ctx, s_sink], axis=-1), axis=-1)
        p_band = p[..., :band].astype(v.dtype)
        p_ctx = p[..., band:band + n_ctx].astype(v.dtype)
        return (jnp.einsum("bngqk,bknd->bqngd", p_band, vb)
                + jnp.einsum("bngqk,bknd->bqngd", p_ctx, vc))

    o = lax.map(block, jnp.arange(L // BLOCK))
    o = jnp.moveaxis(o, 0, 1).reshape(B, L, D_INNER)
    y = (o * jax.nn.silu(z)) @ w_out
    yc = None
    if need_ctx_out:
        s = jnp.einsum("bqngd,bknd->bngqk", qc, kc).astype(f32) * scale
        s_sink = jnp.broadcast_to(sink_logit, s.shape[:-1] + (1,))
        p = jax.nn.softmax(jnp.concatenate([s, s_sink], axis=-1), axis=-1)[..., :n_ctx]
        oc = jnp.einsum("bngqk,bknd->bqngd", p.astype(vc.dtype), vc).reshape(B, n_ctx, D_INNER)
        yc = (oc * jax.nn.silu(zc)) @ w_out
    return y, yc


def setup_inputs(seed: int = 0) -> dict:
    key = jax.random.key(seed)
    keys = iter(jax.random.split(key, 64))

    def rnd(shape, scale):
        return jax.random.normal(next(keys), shape, dtype=jnp.float32) * scale

    d_scale = D_MODEL ** -0.5
    e_scale = D_INNER ** -0.5
    kv_w = WIN_KV_HEADS * WIN_HEAD_DIM
    inp = {
        "x": rnd((BATCH, SEQ, D_MODEL), 1.0),
        "c": rnd((BATCH, D_MODEL), 1.0),
        "ctx": rnd((BATCH, CTX_LEN, D_MODEL), 1.0),
        "c_ctx": rnd((D_MODEL,), 1.0),
    }
    for i in range(DEPTH):
        p = f"l{i}_"
        kind = i % N_MIXERS
        inp[p + "norm"] = 1.0 + rnd((D_MODEL,), 0.05)
        inp[p + "w_mod"] = rnd((D_MODEL, 3 * D_MODEL), 0.5 * d_scale)
        inp[p + "b_mod"] = rnd((3 * D_MODEL,), 0.02)
        if kind == 0:
            inp[p + "w_in"] = rnd((D_MODEL, 4 * D_INNER), d_scale)
            inp[p + "conv_w"] = rnd((CONV_W, D_INNER), CONV_W ** -0.5)
            inp[p + "conv_b"] = rnd((D_INNER,), 0.02)
        elif kind == 1:
            inp[p + "w_in"] = rnd((D_MODEL, 4 * D_INNER), d_scale)
            inp[p + "q_norm"] = 1.0 + rnd((DIFF_HEAD_DIM,), 0.05)
            inp[p + "k_norm"] = 1.0 + rnd((DIFF_HEAD_DIM,), 0.05)
            inp[p + "lam_q1"] = rnd((DIFF_HEAD_DIM,), 0.1)
            inp[p + "lam_k1"] = rnd((DIFF_HEAD_DIM,), 0.1)
            inp[p + "lam_q2"] = rnd((DIFF_HEAD_DIM,), 0.1)
            inp[p + "lam_k2"] = rnd((DIFF_HEAD_DIM,), 0.1)
            inp[p + "sub_norm"] = 1.0 + rnd((DIFF_V_DIM,), 0.05)
        else:
            inp[p + "w_in"] = rnd((D_MODEL, 2 * D_INNER + 2 * kv_w), d_scale)
            inp[p + "q_norm"] = 1.0 + rnd((WIN_HEAD_DIM,), 0.05)
            inp[p + "k_norm"] = 1.0 + rnd((WIN_HEAD_DIM,), 0.05)
            inp[p + "sink"] = rnd((WIN_HEADS,), 1.0)
        inp[p + "w_out"] = rnd((D_INNER, D_MODEL), e_scale)
    return inp


def reference(x, c, ctx, c_ctx,
              l0_norm, l0_w_mod, l0_b_mod, l0_w_in, l0_conv_w, l0_conv_b, l0_w_out,
              l1_norm, l1_w_mod, l1_b_mod, l1_w_in, l1_q_norm, l1_k_norm,
              l1_lam_q1, l1_lam_k1, l1_lam_q2, l1_lam_k2, l1_sub_norm, l1_w_out,
              l2_norm, l2_w_mod, l2_b_mod, l2_w_in, l2_q_norm, l2_k_norm, l2_sink, l2_w_out,
              l3_norm, l3_w_mod, l3_b_mod, l3_w_in, l3_conv_w, l3_conv_b, l3_w_out):
    n_tok = x.shape[1]
    rows = n_tok // GRID_W
    rope_diff = axial_rope_tables(rows, DIFF_HEAD_DIM, x.dtype)
    rope_win = axial_rope_tables(rows, WIN_HEAD_DIM, x.dtype)
    layers = [
        (l0_norm, l0_w_mod, l0_b_mod, (l0_w_in, l0_conv_w, l0_conv_b, l0_w_out)),
        (l1_norm, l1_w_mod, l1_b_mod, (l1_w_in, l1_q_norm, l1_k_norm, l1_lam_q1, l1_lam_k1,
                                       l1_lam_q2, l1_lam_k2, l1_sub_norm, l1_w_out)),
        (l2_norm, l2_w_mod, l2_b_mod, (l2_w_in, l2_q_norm, l2_k_norm, l2_sink, l2_w_out)),
        (l3_norm, l3_w_mod, l3_b_mod, (l3_w_in, l3_conv_w, l3_conv_b, l3_w_out)),
    ]
    for i in range(DEPTH):
        norm_g, w_mod, b_mod, mix_p = layers[i]
        kind = i % N_MIXERS
        reads_ctx = kind != 0
        need_ctx_out = any((j % N_MIXERS) != 0 for j in range(i + 1, DEPTH))
        shift, scale, gate = modulation(c, w_mod, b_mod)
        h = rms_norm(x, norm_g) * (1 + scale[:, None, :]) + shift[:, None, :]
        hc = None
        if reads_ctx or need_ctx_out:
            shift_c, scale_c, gate_c = modulation(c_ctx, w_mod, b_mod)
            hc = rms_norm(ctx, norm_g) * (1 + scale_c) + shift_c
        if kind == 0:
            y, yc = short_conv_mixer(h, hc, mix_p, need_ctx_out)
        elif kind == 1:
            y, yc = diff_attention_mixer(h, hc, mix_p, i, need_ctx_out, rope_diff)
        else:
            y, yc = window_gqa_mixer(h, hc, mix_p, need_ctx_out, rope_win)
        x = x + gate[:, None, :] * y
        if need_ctx_out:
            ctx = ctx + gate_c * yc
    return x
```

```python
import functools
import math

import jax
import jax.numpy as jnp
from jax import lax
from jax.experimental import pallas as pl
from jax.experimental.pallas import tpu as pltpu

LANES = 128
BF16_ROWS = 16
GRID_W = 64
ROPE_BASE = 10000.0
EPS = 1e-6
NEG_INF = -1e30
ATTN_BLOCK = 128
MOD_ROWS = 8
VMEM_LIMIT = 56 * 1024 * 1024

F32 = jnp.float32
BF16 = jnp.bfloat16


def _params(*sem):
    return pltpu.CompilerParams(dimension_semantics=sem, vmem_limit_bytes=VMEM_LIMIT)


def _tile(n, target):
    if n <= target:
        return n
    t = target - target % LANES
    while n % t:
        t -= LANES
    return t


def _dot(a, b):
    return jnp.dot(a, b, preferred_element_type=F32)


def _dot_nt(a, b):
    return lax.dot_general(a, b, (((1,), (1,)), ((), ())), preferred_element_type=F32)


def _norm_mod(t, g, shift, scale):
    ms = jnp.mean(t * t, axis=-1, keepdims=True)
    y = t * lax.rsqrt(ms + EPS) * g
    return (y * (1.0 + scale) + shift).astype(BF16)


def _mod_kernel(c_ref, w_ref, b_ref, o_ref):
    a = jax.nn.silu(c_ref[...]).astype(BF16)
    o_ref[...] = _dot(a, w_ref[...].astype(BF16)) + b_ref[...]


def _modulation(cc, w_mod, b_mod):
    d, n = w_mod.shape
    tn = _tile(n, 768)
    out = pl.pallas_call(
        _mod_kernel,
        grid=(n // tn,),
        in_specs=[
            pl.BlockSpec((MOD_ROWS, d), lambda j: (0, 0)),
            pl.BlockSpec((d, tn), lambda j: (0, j)),
            pl.BlockSpec((1, tn), lambda j: (0, j)),
        ],
        out_specs=pl.BlockSpec((MOD_ROWS, tn), lambda j: (0, j)),
        out_shape=jax.ShapeDtypeStruct((MOD_ROWS, n), F32),
        compiler_params=_params("arbitrary"),
        name="modulation",
    )(cc, w_mod, b_mod.reshape(1, n))
    return out.reshape(MOD_ROWS, 1, n)


def _conv_kernel(xp_ref, x_ref, xn_ref, mod_ref, g_ref, wb_ref, wc_ref, wx_ref, wz_ref, cw_ref, cb_ref,
                 wo_ref, o_ref, h_scr, *, tm, d, tiles_per_seq):
    i = pl.program_id(0)
    j = pl.program_id(1)
    halo = BF16_ROWS
    mod = mod_ref[0]
    shift, scale, gate = mod[:, :d], mod[:, d:2 * d], mod[:, 2 * d:]

    @pl.when(j == 0)
    def _():
        g = g_ref[...]
        h_scr[0:halo, :] = _norm_mod(xp_ref[...], g, shift, scale)
        h_scr[halo:halo + tm, :] = _norm_mod(x_ref[...], g, shift, scale)
        h_scr[halo + tm:, :] = _norm_mod(xn_ref[...], g, shift, scale)
        o_ref[...] = jnp.zeros_like(o_ref)

    n = tm + 2 * halo
    h = h_scr[...]
    hm = h_scr[halo:halo + tm, :]
    u = _dot(h, wc_ref[...]) * _dot(h, wx_ref[...])
    pos = i % tiles_per_seq
    rows = lax.broadcasted_iota(jnp.int32, (n, 1), 0)
    keep = ((rows >= halo) | (pos != 0)) & ((rows < halo + tm) | (pos != tiles_per_seq - 1))
    u = jnp.where(keep, u, 0.0)
    u_prev = pltpu.roll(u, 1, 0)[halo:halo + tm]
    u_next = pltpu.roll(u, n - 1, 0)[halo:halo + tm]
    cw = cw_ref[...]
    y = u_prev * cw[0:1] + u[halo:halo + tm] * cw[1:2] + u_next * cw[2:3] + cb_ref[...]
    gated = _dot(hm, wb_ref[...]) * y * jax.nn.silu(_dot(hm, wz_ref[...]))
    o_ref[...] += _dot(gated.astype(BF16), wo_ref[...])

    @pl.when(j == pl.num_programs(1) - 1)
    def _():
        o_ref[...] = x_ref[...] + gate * o_ref[...]


def _conv_layer(x2, seq, mod, mod_row, norm_g, w_in, conv_w, conv_b, w_out):
    m, d = x2.shape
    di = w_out.shape[0]
    tm = min(seq, 512)
    tn = _tile(di, 256)
    nj = di // tn
    tiles_per_seq = seq // tm
    hb = tm // BF16_ROWS
    last_hb = m // BF16_ROWS - 1
    w_spec = lambda c: pl.BlockSpec((d, tn), lambda i, j: (0, c * nj + j))
    return pl.pallas_call(
        functools.partial(_conv_kernel, tm=tm, d=d, tiles_per_seq=tiles_per_seq),
        grid=(m // tm, nj),
        in_specs=[
            pl.BlockSpec((BF16_ROWS, d), lambda i, j: (jnp.maximum(i * hb - 1, 0), 0)),
            pl.BlockSpec((tm, d), lambda i, j: (i, 0)),
            pl.BlockSpec((BF16_ROWS, d), lambda i, j: (jnp.minimum((i + 1) * hb, last_hb), 0)),
            pl.BlockSpec((1, 1, 3 * d), lambda i, j: (mod_row(i, tiles_per_seq), 0, 0)),
            pl.BlockSpec((1, d), lambda i, j: (0, 0)),
            w_spec(0), w_spec(1), w_spec(2), w_spec(3),
            pl.BlockSpec((3, tn), lambda i, j: (0, j)),
            pl.BlockSpec((1, tn), lambda i, j: (0, j)),
            pl.BlockSpec((tn, d), lambda i, j: (j, 0)),
        ],
        out_specs=pl.BlockSpec((tm, d), lambda i, j: (i, 0)),
        out_shape=jax.ShapeDtypeStruct((m, d), F32),
        scratch_shapes=[pltpu.VMEM((tm + 2 * BF16_ROWS, d), BF16)],
        compiler_params=_params("parallel", "arbitrary"),
        name="conv_layer",
    )(x2, x2, x2, mod, norm_g.reshape(1, d), w_in, w_in, w_in, w_in, conv_w, conv_b.reshape(1, di), w_out)


def _head_norm_rope(acc, gn, cos, sin, group, out_scale):
    outs = []
    swap = group // 4
    for c in range(acc.shape[1] // LANES):
        t = acc[:, c * LANES:(c + 1) * LANES]
        lane = lax.broadcasted_iota(jnp.int32, t.shape, 1)
        sq = t * t
        if group == LANES:
            ms = jnp.mean(sq, axis=-1, keepdims=True)
        else:
            lo = lane < group
            s_lo = jnp.sum(jnp.where(lo, sq, 0.0), axis=-1, keepdims=True)
            s_hi = jnp.sum(jnp.where(lo, 0.0, sq), axis=-1, keepdims=True)
            ms = jnp.where(lo, s_lo, s_hi) * (1.0 / group)
        y = t * lax.rsqrt(ms + EPS) * gn
        if cos is not None:
            ahead = pltpu.roll(y, LANES - swap, 1)
            behind = pltpu.roll(y, swap, 1)
            y = y * cos + jnp.where((lane & swap) == 0, ahead, behind) * sin
        if out_scale != 1.0:
            y = y * out_scale
        outs.append(y)
    return outs[0] if len(outs) == 1 else jnp.concatenate(outs, axis=1)


def _proj_kernel(*refs, d, q_tiles, k_tiles, group, rope, q_scale):
    if rope:
        x_ref, mod_ref, g_ref, w_ref, qn_ref, kn_ref, cos_ref, sin_ref, o_ref, h_scr = refs
    else:
        x_ref, mod_ref, g_ref, w_ref, qn_ref, kn_ref, o_ref, h_scr = refs
    j = pl.program_id(1)

    @pl.when(j == 0)
    def _():
        mod = mod_ref[0]
        h_scr[...] = _norm_mod(x_ref[...], g_ref[...], mod[:, :d], mod[:, d:2 * d])

    acc = _dot(h_scr[...], w_ref[...])
    cos = cos_ref[...] if rope else None
    sin = sin_ref[...] if rope else None

    @pl.when(j < q_tiles)
    def _():
        o_ref[...] = _head_norm_rope(acc, qn_ref[...], cos, sin, group, q_scale).astype(BF16)

    @pl.when((j >= q_tiles) & (j < q_tiles + k_tiles))
    def _():
        o_ref[...] = _head_norm_rope(acc, kn_ref[...], cos, sin, group, 1.0).astype(BF16)

    @pl.when(j >= q_tiles + k_tiles)
    def _():
        o_ref[...] = acc.astype(BF16)


def _rope_tables(seq, group):
    rows = seq // GRID_W
    row = jnp.repeat(jnp.arange(rows), GRID_W).astype(F32)
    col = jnp.tile(jnp.arange(GRID_W), rows).astype(F32)
    n_freq = group // 4
    inv_freq = ROPE_BASE ** (-(jnp.arange(n_freq, dtype=F32) / n_freq))
    ar, ac = row[:, None] * inv_freq, col[:, None] * inv_freq
    cos = jnp.concatenate([jnp.cos(ar), jnp.cos(ar), jnp.cos(ac), jnp.cos(ac)], axis=-1)
    sin = jnp.concatenate([-jnp.sin(ar), jnp.sin(ar), -jnp.sin(ac), jnp.sin(ac)], axis=-1)
    reps = LANES // group
    return jnp.tile(cos, (1, reps)), jnp.tile(sin, (1, reps))


def _proj_layer(x2, seq, mod, mod_row, norm_g, w_in, q_norm, k_norm, q_cols, k_cols, rope, q_scale):
    m, d = x2.shape
    n = w_in.shape[1]
    group = q_norm.shape[0]
    tm = min(seq, 512)
    tn = math.gcd(math.gcd(q_cols, k_cols), 512)
    tiles_per_seq = seq // tm
    reps = LANES // group
    args = [x2, mod, norm_g.reshape(1, d), w_in,
            jnp.tile(q_norm, reps).reshape(1, LANES), jnp.tile(k_norm, reps).reshape(1, LANES)]
    in_specs = [
        pl.BlockSpec((tm, d), lambda i, j: (i, 0)),
        pl.BlockSpec((1, 1, 3 * d), lambda i, j: (mod_row(i, tiles_per_seq), 0, 0)),
        pl.BlockSpec((1, d), lambda i, j: (0, 0)),
        pl.BlockSpec((d, tn), lambda i, j: (0, j)),
        pl.BlockSpec((1, LANES), lambda i, j: (0, 0)),
        pl.BlockSpec((1, LANES), lambda i, j: (0, 0)),
    ]
    if rope:
        args += list(_rope_tables(seq, group))
        in_specs += [pl.BlockSpec((tm, LANES), lambda i, j: (i % tiles_per_seq, 0))] * 2
    return pl.pallas_call(
        functools.partial(_proj_kernel, d=d, q_tiles=q_cols // tn, k_tiles=k_cols // tn, group=group,
                          rope=rope, q_scale=q_scale),
        grid=(m // tm, n // tn),
        in_specs=in_specs,
        out_specs=pl.BlockSpec((tm, tn), lambda i, j: (i, j)),
        out_shape=jax.ShapeDtypeStruct((m, n), BF16),
        scratch_shapes=[pltpu.VMEM((tm, d), BF16)],
        compiler_params=_params("parallel", "arbitrary"),
        name="proj_layer",
    )(*args)


def _out_kernel(a_ref, w_ref, x_ref, gate_ref, o_ref):
    o_ref[...] = x_ref[...] + gate_ref[0] * _dot(a_ref[...], w_ref[...])


def _out_layer(a2, x2, seq, mod, mod_row, w_out):
    m, d = x2.shape
    di = a2.shape[1]
    tm = min(seq, 512)
    tn = _tile(d, 512)
    tiles_per_seq = seq // tm
    gate_block0 = 2 * d // tn
    return pl.pallas_call(
        _out_kernel,
        grid=(m // tm, d // tn),
        in_specs=[
            pl.BlockSpec((tm, di), lambda i, j: (i, 0)),
            pl.BlockSpec((di, tn), lambda i, j: (0, j)),
            pl.BlockSpec((tm, tn), lambda i, j: (i, j)),
            pl.BlockSpec((1, 1, tn), lambda i, j: (mod_row(i, tiles_per_seq), 0, gate_block0 + j)),
        ],
        out_specs=pl.BlockSpec((tm, tn), lambda i, j: (i, j)),
        out_shape=jax.ShapeDtypeStruct((m, d), F32),
        compiler_params=_params("parallel", "arbitrary"),
        name="out_layer",
    )(a2, w_out, x2, mod)


def _diff_attn_kernel(*refs, tq, tk, n_chunks, has_ctx, lam_init):
    if has_ctx:
        lam_ref, q_ref, k_ref, v_ref, kc_ref, vc_ref, z_ref, sn_ref, o_ref, m_scr, l_scr, acc_scr = refs
    else:
        lam_ref, q_ref, k_ref, v_ref, z_ref, sn_ref, o_ref, m_scr, l_scr, acc_scr = refs
    half = LANES // 2
    q = q_ref[...]
    lane = lax.broadcasted_iota(jnp.int32, q.shape, 1)
    zero = jnp.zeros_like(q)
    q2 = jnp.concatenate([jnp.where(lane < half, q, zero), jnp.where(lane < half, zero, q)], axis=0)
    m_scr[...] = jnp.full_like(m_scr, NEG_INF)
    l_scr[...] = jnp.zeros_like(l_scr)
    acc_scr[...] = jnp.zeros_like(acc_scr)

    def step(kb, vb):
        s = _dot_nt(q2, kb)
        m_prev = m_scr[...]
        m_new = jnp.maximum(m_prev, jnp.max(s, axis=-1, keepdims=True))
        alpha = jnp.exp(m_prev - m_new)
        p = jnp.exp(s - m_new)
        l_scr[...] = alpha * l_scr[...] + jnp.sum(p, axis=-1, keepdims=True)
        acc_scr[...] = alpha * acc_scr[...] + _dot(p.astype(BF16), vb)
        m_scr[...] = m_new

    def body(c, carry):
        off = pl.multiple_of(c * tk, tk)
        step(k_ref[pl.ds(off, tk), :], v_ref[pl.ds(off, tk), :])
        return carry

    lax.fori_loop(0, n_chunks, body, 0)
    if has_ctx:
        step(kc_ref[...], vc_ref[...])

    lv = lam_ref[...]
    lam = (jnp.exp(jnp.sum(lv[0:1] * lv[1:2], axis=-1, keepdims=True))
           - jnp.exp(jnp.sum(lv[2:3] * lv[3:4], axis=-1, keepdims=True)) + lam_init)
    o_all = acc_scr[...] / l_scr[...]
    o = o_all[:tq] - lam * o_all[tq:]
    ms = jnp.mean(o * o, axis=-1, keepdims=True)
    y = o * lax.rsqrt(ms + EPS) * sn_ref[...] * (1.0 - lam_init)
    o_ref[...] = (y * jax.nn.silu(z_ref[...].astype(F32))).astype(BF16)


def _diff_attention(qkvz, seq, batch, ctx_qkvz, ctx_len, lam_vecs, sub_norm, lam_init):
    di = qkvz.shape[1] // 4
    heads = di // LANES
    tq = min(seq, 256)
    tk = min(seq, 512)
    nq = seq // tq
    has_ctx = ctx_qkvz is not None
    args = [lam_vecs, qkvz, qkvz, qkvz]
    in_specs = [
        pl.BlockSpec(lam_vecs.shape, lambda b, h, i: (0, 0)),
        pl.BlockSpec((tq, LANES), lambda b, h, i: (b * nq + i, h)),
        pl.BlockSpec((seq, LANES), lambda b, h, i: (b, heads + h)),
        pl.BlockSpec((seq, LANES), lambda b, h, i: (b, 2 * heads + h)),
    ]
    if has_ctx:
        args += [ctx_qkvz, ctx_qkvz]
        in_specs += [
            pl.BlockSpec((ctx_len, LANES), lambda b, h, i: (b, heads + h)),
            pl.BlockSpec((ctx_len, LANES), lambda b, h, i: (b, 2 * heads + h)),
        ]
    args += [qkvz, sub_norm.reshape(1, LANES)]
    in_specs += [
        pl.BlockSpec((tq, LANES), lambda b, h, i: (b * nq + i, 3 * heads + h)),
        pl.BlockSpec((1, LANES), lambda b, h, i: (0, 0)),
    ]
    return pl.pallas_call(
        functools.partial(_diff_attn_kernel, tq=tq, tk=tk, n_chunks=seq // tk, has_ctx=has_ctx, lam_init=lam_init),
        grid=(batch, heads, nq),
        in_specs=in_specs,
        out_specs=pl.BlockSpec((tq, LANES), lambda b, h, i: (b * nq + i, h)),
        out_shape=jax.ShapeDtypeStruct((batch * seq, di), BF16),
        scratch_shapes=[pltpu.VMEM((2 * tq, 1), F32), pltpu.VMEM((2 * tq, 1), F32), pltpu.VMEM((2 * tq, LANES), F32)],
        compiler_params=_params("parallel", "parallel", "arbitrary"),
        name="diff_attention",
    )(*args)


def _win_attn_kernel(q_ref, kp_ref, kc_ref, kn_ref, vp_ref, vc_ref, vn_ref, kx_ref, vx_ref, sink_ref, z_ref, o_ref,
                     *, group, scale):
    i = pl.program_id(1)
    blk = ATTN_BLOCK
    q = q_ref[...]
    q4 = jnp.concatenate([q[:, g * LANES:(g + 1) * LANES] for g in range(group)], axis=0)
    qpos = lax.broadcasted_iota(jnp.int32, (group * blk, blk), 0) % blk
    kpos = lax.broadcasted_iota(jnp.int32, (group * blk, blk), 1)
    s_prev = jnp.where((kpos >= qpos) & (i > 0), _dot_nt(q4, kp_ref[...]) * scale, NEG_INF)
    s_cur = _dot_nt(q4, kc_ref[...]) * scale
    s_next = jnp.where((kpos <= qpos) & (i < pl.num_programs(1) - 1), _dot_nt(q4, kn_ref[...]) * scale, NEG_INF)
    s_ctx = _dot_nt(q4, kx_ref[...]) * scale
    sink = sink_ref[0]
    m = sink
    for s in (s_prev, s_cur, s_next, s_ctx):
        m = jnp.maximum(m, jnp.max(s, axis=-1, keepdims=True))
    l = jnp.exp(sink - m)
    o = jnp.zeros((group * blk, LANES), F32)
    for s, v_ref in ((s_prev, vp_ref), (s_cur, vc_ref), (s_next, vn_ref), (s_ctx, vx_ref)):
        p = jnp.exp(s - m)
        l = l + jnp.sum(p, axis=-1, keepdims=True)
        o = o + _dot(p.astype(BF16), v_ref[...])
    o = o / l
    o = jnp.concatenate([o[g * blk:(g + 1) * blk] for g in range(group)], axis=1)
    o_ref[...] = (o * jax.nn.silu(z_ref[...].astype(F32))).astype(BF16)


def _win_attention(qkvz, seq, batch, ctx_qkvz, ctx_len, di, kv_heads, sink):
    heads = di // LANES
    group = heads // kv_heads
    gw = group * LANES
    blk = ATTN_BLOCK
    nb = seq // blk
    k_col0 = di // LANES
    v_col0 = k_col0 + kv_heads
    z_col0 = (di + 2 * kv_heads * LANES) // gw
    assert (di + 2 * kv_heads * LANES) % gw == 0
    sink_rows = jnp.repeat(sink.astype(F32).reshape(kv_heads, group), blk, axis=1).reshape(kv_heads, group * blk, 1)
    band = lambda col0, shift: pl.BlockSpec(
        (blk, LANES), lambda b, i, n: (b * nb + jnp.clip(i + shift, 0, nb - 1), col0 + n))
    return pl.pallas_call(
        functools.partial(_win_attn_kernel, group=group, scale=LANES ** -0.5),
        grid=(batch, nb, kv_heads),
        in_specs=[
            pl.BlockSpec((blk, gw), lambda b, i, n: (b * nb + i, n)),
            band(k_col0, -1), band(k_col0, 0), band(k_col0, 1),
            band(v_col0, -1), band(v_col0, 0), band(v_col0, 1),
            pl.BlockSpec((ctx_len, LANES), lambda b, i, n: (b, k_col0 + n)),
            pl.BlockSpec((ctx_len, LANES), lambda b, i, n: (b, v_col0 + n)),
            pl.BlockSpec((1, group * blk, 1), lambda b, i, n: (n, 0, 0)),
            pl.BlockSpec((blk, gw), lambda b, i, n: (b * nb + i, z_col0 + n)),
        ],
        out_specs=pl.BlockSpec((blk, gw), lambda b, i, n: (b * nb + i, n)),
        out_shape=jax.ShapeDtypeStruct((batch * seq, di), BF16),
        compiler_params=_params("parallel", "parallel", "arbitrary"),
        name="win_attention",
    )(qkvz, qkvz, qkvz, qkvz, qkvz, qkvz, qkvz, ctx_qkvz, ctx_qkvz, sink_rows, qkvz)


def _x_mod_row(i, tiles_per_seq):
    return i // tiles_per_seq


def kernel(x, c, ctx, c_ctx, l0_norm, l0_w_mod, l0_b_mod, l0_w_in, l0_conv_w, l0_conv_b, l0_w_out, l1_norm, l1_w_mod, l1_b_mod, l1_w_in, l1_q_norm, l1_k_norm, l1_lam_q1, l1_lam_k1, l1_lam_q2, l1_lam_k2, l1_sub_norm, l1_w_out, l2_norm, l2_w_mod, l2_b_mod, l2_w_in, l2_q_norm, l2_k_norm, l2_sink, l2_w_out, l3_norm, l3_w_mod, l3_b_mod, l3_w_in, l3_conv_w, l3_conv_b, l3_w_out):
    batch, seq, d = x.shape
    ctx_len = ctx.shape[1]
    di = l0_w_out.shape[0]
    assert batch < MOD_ROWS
    ctx_mod_row = lambda i, tiles_per_seq: batch

    x2 = x.reshape(batch * seq, d)
    c2 = ctx.reshape(batch * ctx_len, d)
    cc = jnp.zeros((MOD_ROWS, d), F32).at[:batch].set(c).at[batch].set(c_ctx)
    mod0, mod1, mod2, mod3 = (_modulation(cc, w, b) for w, b in
                              ((l0_w_mod, l0_b_mod), (l1_w_mod, l1_b_mod), (l2_w_mod, l2_b_mod), (l3_w_mod, l3_b_mod)))
    bf = lambda w: w.astype(BF16)

    w_in, w_out = bf(l0_w_in), bf(l0_w_out)
    x2n = _conv_layer(x2, seq, mod0, _x_mod_row, l0_norm, w_in, l0_conv_w, l0_conv_b, w_out)
    c2 = _conv_layer(c2, ctx_len, mod0, ctx_mod_row, l0_norm, w_in, l0_conv_w, l0_conv_b, w_out)
    x2 = x2n

    w_in, w_out = bf(l1_w_in), bf(l1_w_out)
    lam_init = 0.8 - 0.6 * math.exp(-0.3 * 1)
    q_scale = (l1_q_norm.shape[0]) ** -0.5
    proj = functools.partial(_proj_layer, norm_g=l1_norm, w_in=w_in, q_norm=l1_q_norm, k_norm=l1_k_norm,
                             q_cols=di, k_cols=di, q_scale=q_scale)
    qx = proj(x2, seq, mod1, _x_mod_row, rope=True)
    qc = proj(c2, ctx_len, mod1, ctx_mod_row, rope=False)
    lam_vecs = jnp.stack([l1_lam_q1, l1_lam_k1, l1_lam_q2, l1_lam_k2]).astype(F32)
    ax = _diff_attention(qx, seq, batch, qc, ctx_len, lam_vecs, l1_sub_norm, lam_init)
    ac = _diff_attention(qc, ctx_len, batch, None, 0, lam_vecs, l1_sub_norm, lam_init)
    x2 = _out_layer(ax, x2, seq, mod1, _x_mod_row, w_out)
    c2 = _out_layer(ac, c2, ctx_len, mod1, ctx_mod_row, w_out)

    w_in, w_out = bf(l2_w_in), bf(l2_w_out)
    kv_cols = (l2_w_in.shape[1] - 2 * di) // 2
    proj = functools.partial(_proj_layer, norm_g=l2_norm, w_in=w_in, q_norm=l2_q_norm, k_norm=l2_k_norm,
                             q_cols=di, k_cols=kv_cols, q_scale=1.0)
    qx = proj(x2, seq, mod2, _x_mod_row, rope=True)
    qc = proj(c2, ctx_len, mod2, ctx_mod_row, rope=False)
    ax = _win_attention(qx, seq, batch, qc, ctx_len, di, kv_cols // LANES, l2_sink)
    x2 = _out_layer(ax, x2, seq, mod2, _x_mod_row, w_out)

    x2 = _conv_layer(x2, seq, mod3, _x_mod_row, l3_norm, bf(l3_w_in), l3_conv_w, l3_conv_b, bf(l3_w_out))
    return x2.reshape(batch, seq, d)
```

```python
import functools
import math

import jax
import jax.numpy as jnp
from jax import lax
from jax.experimental import pallas as pl
from jax.experimental.pallas import tpu as pltpu

LANES = 128
BF16_ROWS = 16
GRID_W = 64
ROPE_BASE = 10000.0
EPS = 1e-6
NEG_INF = -1e30
ATTN_BLOCK = 128
MOD_ROWS = 8
VMEM_LIMIT = 56 * 1024 * 1024

F32 = jnp.float32
BF16 = jnp.bfloat16


def _params(*sem):
    return pltpu.CompilerParams(dimension_semantics=sem, vmem_limit_bytes=VMEM_LIMIT)


def _tile(n, target):
    if n <= target:
        return n
    t = target - target % LANES
    while n % t:
        t -= LANES
    return t


def _dot(a, b):
    return jnp.dot(a, b, preferred_element_type=F32)


def _dot_nt(a, b):
    return lax.dot_general(a, b, (((1,), (1,)), ((), ())), preferred_element_type=F32)


def _norm_mod(t, g, shift, scale):
    ms = jnp.mean(t * t, axis=-1, keepdims=True)
    y = t * lax.rsqrt(ms + EPS) * g
    return (y * (1.0 + scale) + shift).astype(BF16)


def _mod_kernel(c_ref, w_ref, b_ref, o_ref):
    a = jax.nn.silu(c_ref[...]).astype(BF16)
    o_ref[...] = _dot(a, w_ref[...].astype(BF16)) + b_ref[...]


def _modulation(cc, w_mod, b_mod):
    d, n = w_mod.shape
    tn = _tile(n, 768)
    out = pl.pallas_call(
        _mod_kernel,
        grid=(n // tn,),
        in_specs=[
            pl.BlockSpec((MOD_ROWS, d), lambda j: (0, 0)),
            pl.BlockSpec((d, tn), lambda j: (0, j)),
            pl.BlockSpec((1, tn), lambda j: (0, j)),
        ],
        out_specs=pl.BlockSpec((MOD_ROWS, tn), lambda j: (0, j)),
        out_shape=jax.ShapeDtypeStruct((MOD_ROWS, n), F32),
        compiler_params=_params("arbitrary"),
        name="modulation",
    )(cc, w_mod, b_mod.reshape(1, n))
    return out.reshape(MOD_ROWS, 1, n)


def _conv_kernel(xp_ref, x_ref, xn_ref, mod_ref, g_ref, wb_ref, wc_ref, wx_ref, wz_ref, cw_ref, cb_ref,
                 wo_ref, o_ref, h_scr, *, tm, d, tiles_per_seq):
    i = pl.program_id(0)
    j = pl.program_id(1)
    halo = BF16_ROWS
    mod = mod_ref[0]
    shift, scale, gate = mod[:, :d], mod[:, d:2 * d], mod[:, 2 * d:]

    @pl.when(j == 0)
    def _():
        g = g_ref[...]
        h_scr[0:halo, :] = _norm_mod(xp_ref[...], g, shift, scale)
        h_scr[halo:halo + tm, :] = _norm_mod(x_ref[...], g, shift, scale)
        h_scr[halo + tm:, :] = _norm_mod(xn_ref[...], g, shift, scale)
        o_ref[...] = jnp.zeros_like(o_ref)

    n = tm + 2 * halo
    h = h_scr[...]
    hm = h_scr[halo:halo + tm, :]
    u = _dot(h, wc_ref[...]) * _dot(h, wx_ref[...])
    pos = i % tiles_per_seq
    rows = lax.broadcasted_iota(jnp.int32, (n, 1), 0)
    keep = ((rows >= halo) | (pos != 0)) & ((rows < halo + tm) | (pos != tiles_per_seq - 1))
    u = jnp.where(keep, u, 0.0)
    u_prev = pltpu.roll(u, 1, 0)[halo:halo + tm]
    u_next = pltpu.roll(u, n - 1, 0)[halo:halo + tm]
    cw = cw_ref[...]
    y = u_prev * cw[0:1] + u[halo:halo + tm] * cw[1:2] + u_next * cw[2:3] + cb_ref[...]
    gated = _dot(hm, wb_ref[...]) * y * jax.nn.silu(_dot(hm, wz_ref[...]))
    o_ref[...] += _dot(gated.astype(BF16), wo_ref[...])

    @pl.when(j == pl.num_programs(1) - 1)
    def _():
        o_ref[...] = x_ref[...] + gate * o_ref[...]


def _conv_layer(x2, seq, mod, mod_row, norm_g, w_in, conv_w, conv_b, w_out):
    m, d = x2.shape
    di = w_out.shape[0]
    tm = min(seq, 512)
    tn = _tile(di, 256)
    nj = di // tn
    tiles_per_seq = seq // tm
    hb = tm // BF16_ROWS
    last_hb = m // BF16_ROWS - 1
    w_spec = lambda c: pl.BlockSpec((d, tn), lambda i, j: (0, c * nj + j))
    return pl.pallas_call(
        functools.partial(_conv_kernel, tm=tm, d=d, tiles_per_seq=tiles_per_seq),
        grid=(m // tm, nj),
        in_specs=[
            pl.BlockSpec((BF16_ROWS, d), lambda i, j: (jnp.maximum(i * hb - 1, 0), 0)),
            pl.BlockSpec((tm, d), lambda i, j: (i, 0)),
            pl.BlockSpec((BF16_ROWS, d), lambda i, j: (jnp.minimum((i + 1) * hb, last_hb), 0)),
            pl.BlockSpec((1, 1, 3 * d), lambda i, j: (mod_row(i, tiles_per_seq), 0, 0)),
            pl.BlockSpec((1, d), lambda i, j: (0, 0)),
            w_spec(0), w_spec(1), w_spec(2), w_spec(3),
            pl.BlockSpec((3, tn), lambda i, j: (0, j)),
            pl.BlockSpec((1, tn), lambda i, j: (0, j)),
            pl.BlockSpec((tn, d), lambda i, j: (j, 0)),
        ],
        out_specs=pl.BlockSpec((tm, d), lambda i, j: (i, 0)),
        out_shape=jax.ShapeDtypeStruct((m, d), F32),
        scratch_shapes=[pltpu.VMEM((tm + 2 * BF16_ROWS, d), BF16)],
        compiler_params=_params("parallel", "arbitrary"),
        name="conv_layer",
    )(x2, x2, x2, mod, norm_g.reshape(1, d), w_in, w_in, w_in, w_in, conv_w, conv_b.reshape(1, di), w_out)


def _head_norm_rope(acc, gn, cos, sin, group, out_scale):
    outs = []
    swap = group // 4
    for c in range(acc.shape[1] // LANES):
        t = acc[:, c * LANES:(c + 1) * LANES]
        lane = lax.broadcasted_iota(jnp.int32, t.shape, 1)
        sq = t * t
        if group == LANES:
            ms = jnp.mean(sq, axis=-1, keepdims=True)
        else:
            lo = lane < group
            s_lo = jnp.sum(jnp.where(lo, sq, 0.0), axis=-1, keepdims=True)
            s_hi = jnp.sum(jnp.where(lo, 0.0, sq), axis=-1, keepdims=True)
            ms = jnp.where(lo, s_lo, s_hi) * (1.0 / group)
        y = t * lax.rsqrt(ms + EPS) * gn
        if cos is not None:
            ahead = pltpu.roll(y, LANES - swap, 1)
            behind = pltpu.roll(y, swap, 1)
            y = y * cos + jnp.where((lane & swap) == 0, ahead, behind) * sin
        if out_scale != 1.0:
            y = y * out_scale
        outs.append(y)
    return outs[0] if len(outs) == 1 else jnp.concatenate(outs, axis=1)


def _proj_kernel(*refs, d, q_tiles, k_tiles, group, rope, q_scale):
    if rope:
        x_ref, mod_ref, g_ref, w_ref, qn_ref, kn_ref, cos_ref, sin_ref, o_ref, h_scr = refs
    else:
        x_ref, mod_ref, g_ref, w_ref, qn_ref, kn_ref, o_ref, h_scr = refs
    j = pl.program_id(1)

    @pl.when(j == 0)
    def _():
        mod = mod_ref[0]
        h_scr[...] = _norm_mod(x_ref[...], g_ref[...], mod[:, :d], mod[:, d:2 * d])

    acc = _dot(h_scr[...], w_ref[...])
    cos = cos_ref[...] if rope else None
    sin = sin_ref[...] if rope else None

    @pl.when(j < q_tiles)
    def _():
        o_ref[...] = _head_norm_rope(acc, qn_ref[...], cos, sin, group, q_scale).astype(BF16)

    @pl.when((j >= q_tiles) & (j < q_tiles + k_tiles))
    def _():
        o_ref[...] = _head_norm_rope(acc, kn_ref[...], cos, sin, group, 1.0).astype(BF16)

    @pl.when(j >= q_tiles + k_tiles)
    def _():
        o_ref[...] = acc.astype(BF16)


def _rope_tables(seq, group):
    rows = seq // GRID_W
    row = jnp.repeat(jnp.arange(rows), GRID_W).astype(F32)
    col = jnp.tile(jnp.arange(GRID_W), rows).astype(F32)
    n_freq = group // 4
    inv_freq = ROPE_BASE ** (-(jnp.arange(n_freq, dtype=F32) / n_freq))
    ar, ac = row[:, None] * inv_freq, col[:, None] * inv_freq
    cos = jnp.concatenate([jnp.cos(ar), jnp.cos(ar), jnp.cos(ac), jnp.cos(ac)], axis=-1)
    sin = jnp.concatenate([-jnp.sin(ar), jnp.sin(ar), -jnp.sin(ac), jnp.sin(ac)], axis=-1)
    reps = LANES // group
    return jnp.tile(cos, (1, reps)), jnp.tile(sin, (1, reps))


def _proj_layer(x2, seq, mod, mod_row, norm_g, w_in, q_norm, k_norm, q_cols, k_cols, rope, q_scale):
    m, d = x2.shape
    n = w_in.shape[1]
    group = q_norm.shape[0]
    tm = min(seq, 512)
    tn = math.gcd(math.gcd(q_cols, k_cols), 512)
    tiles_per_seq = seq // tm
    reps = LANES // group
    args = [x2, mod, norm_g.reshape(1, d), w_in,
            jnp.tile(q_norm, reps).reshape(1, LANES), jnp.tile(k_norm, reps).reshape(1, LANES)]
    in_specs = [
        pl.BlockSpec((tm, d), lambda i, j: (i, 0)),
        pl.BlockSpec((1, 1, 3 * d), lambda i, j: (mod_row(i, tiles_per_seq), 0, 0)),
        pl.BlockSpec((1, d), lambda i, j: (0, 0)),
        pl.BlockSpec((d, tn), lambda i, j: (0, j)),
        pl.BlockSpec((1, LANES), lambda i, j: (0, 0)),
        pl.BlockSpec((1, LANES), lambda i, j: (0, 0)),
    ]
    if rope:
        args += list(_rope_tables(seq, group))
        in_specs += [pl.BlockSpec((tm, LANES), lambda i, j: (i % tiles_per_seq, 0))] * 2
    return pl.pallas_call(
        functools.partial(_proj_kernel, d=d, q_tiles=q_cols // tn, k_tiles=k_cols // tn, group=group,
                          rope=rope, q_scale=q_scale),
        grid=(m // tm, n // tn),
        in_specs=in_specs,
        out_specs=pl.BlockSpec((tm, tn), lambda i, j: (i, j)),
        out_shape=jax.ShapeDtypeStruct((m, n), BF16),
        scratch_shapes=[pltpu.VMEM((tm, d), BF16)],
        compiler_params=_params("parallel", "arbitrary"),
        name="proj_layer",
    )(*args)


def _out_kernel(a_ref, w_ref, x_ref, gate_ref, o_ref):
    o_ref[...] = x_ref[...] + gate_ref[0] * _dot(a_ref[...], w_ref[...])


def _out_layer(a2, x2, seq, mod, mod_row, w_out):
    m, d = x2.shape
    di = a2.shape[1]
    tm = min(seq, 512)
    tn = _tile(d, 512)
    tiles_per_seq = seq // tm
    gate_block0 = 2 * d // tn
    return pl.pallas_call(
        _out_kernel,
        grid=(m // tm, d // tn),
        in_specs=[
            pl.BlockSpec((tm, di), lambda i, j: (i, 0)),
            pl.BlockSpec((di, tn), lambda i, j: (0, j)),
            pl.BlockSpec((tm, tn), lambda i, j: (i, j)),
            pl.BlockSpec((1, 1, tn), lambda i, j: (mod_row(i, tiles_per_seq), 0, gate_block0 + j)),
        ],
        out_specs=pl.BlockSpec((tm, tn), lambda i, j: (i, j)),
        out_shape=jax.ShapeDtypeStruct((m, d), F32),
        compiler_params=_params("parallel", "arbitrary"),
        name="out_layer",
    )(a2, w_out, x2, mod)


def _lane_chunks(t):
    return [t[:, c * LANES:(c + 1) * LANES] for c in range(t.shape[1] // LANES)]


def _diff_attn_kernel(*refs, tq, tk, n_chunks, has_ctx, lam_init):
    if has_ctx:
        lam_ref, q_ref, k_ref, v_ref, kc_ref, vc_ref, z_ref, sn_ref, o_ref, s_scr, sc_scr = refs
    else:
        lam_ref, q_ref, k_ref, v_ref, z_ref, sn_ref, o_ref, s_scr = refs
    half = LANES // 2
    q = q_ref[...]
    lane = lax.broadcasted_iota(jnp.int32, q.shape, 1)
    zero = jnp.zeros_like(q)
    q2 = jnp.concatenate([jnp.where(lane < half, q, zero), jnp.where(lane < half, zero, q)], axis=0)

    mx = jnp.full((2 * tq, LANES), NEG_INF, F32)
    for c in range(n_chunks):
        s = _dot_nt(q2, k_ref[c * tk:(c + 1) * tk, :])
        s_scr[c] = s
        for t in _lane_chunks(s):
            mx = jnp.maximum(mx, t)
    if has_ctx:
        s = _dot_nt(q2, kc_ref[...])
        sc_scr[...] = s
        for t in _lane_chunks(s):
            mx = jnp.maximum(mx, t)
    m = jnp.max(mx, axis=-1, keepdims=True)

    lsum = jnp.zeros((2 * tq, LANES), F32)
    acc = jnp.zeros((2 * tq, LANES), F32)
    for c in range(n_chunks):
        p = jnp.exp2(s_scr[c] - m)
        for t in _lane_chunks(p):
            lsum = lsum + t
        acc = acc + _dot(p.astype(BF16), v_ref[c * tk:(c + 1) * tk, :])
    if has_ctx:
        p = jnp.exp2(sc_scr[...] - m)
        for t in _lane_chunks(p):
            lsum = lsum + t
        acc = acc + _dot(p.astype(BF16), vc_ref[...])

    lv = lam_ref[...]
    lam = (jnp.exp(jnp.sum(lv[0:1] * lv[1:2], axis=-1, keepdims=True))
           - jnp.exp(jnp.sum(lv[2:3] * lv[3:4], axis=-1, keepdims=True)) + lam_init)
    o_all = acc / jnp.sum(lsum, axis=-1, keepdims=True)
    o = o_all[:tq] - lam * o_all[tq:]
    ms = jnp.mean(o * o, axis=-1, keepdims=True)
    y = o * lax.rsqrt(ms + EPS) * sn_ref[...] * (1.0 - lam_init)
    o_ref[...] = (y * jax.nn.silu(z_ref[...].astype(F32))).astype(BF16)


def _diff_attention(qkvz, seq, batch, ctx_qkvz, ctx_len, lam_vecs, sub_norm, lam_init):
    di = qkvz.shape[1] // 4
    heads = di // LANES
    tq = min(seq, 256)
    tk = min(seq, 512)
    nq = seq // tq
    has_ctx = ctx_qkvz is not None
    args = [lam_vecs, qkvz, qkvz, qkvz]
    in_specs = [
        pl.BlockSpec(lam_vecs.shape, lambda b, h, i: (0, 0)),
        pl.BlockSpec((tq, LANES), lambda b, h, i: (b * nq + i, h)),
        pl.BlockSpec((seq, LANES), lambda b, h, i: (b, heads + h)),
        pl.BlockSpec((seq, LANES), lambda b, h, i: (b, 2 * heads + h)),
    ]
    scratch = [pltpu.VMEM((seq // tk, 2 * tq, tk), F32)]
    if has_ctx:
        args += [ctx_qkvz, ctx_qkvz]
        in_specs += [
            pl.BlockSpec((ctx_len, LANES), lambda b, h, i: (b, heads + h)),
            pl.BlockSpec((ctx_len, LANES), lambda b, h, i: (b, 2 * heads + h)),
        ]
        scratch += [pltpu.VMEM((2 * tq, ctx_len), F32)]
    args += [qkvz, sub_norm.reshape(1, LANES)]
    in_specs += [
        pl.BlockSpec((tq, LANES), lambda b, h, i: (b * nq + i, 3 * heads + h)),
        pl.BlockSpec((1, LANES), lambda b, h, i: (0, 0)),
    ]
    return pl.pallas_call(
        functools.partial(_diff_attn_kernel, tq=tq, tk=tk, n_chunks=seq // tk, has_ctx=has_ctx, lam_init=lam_init),
        grid=(batch, heads, nq),
        in_specs=in_specs,
        out_specs=pl.BlockSpec((tq, LANES), lambda b, h, i: (b * nq + i, h)),
        out_shape=jax.ShapeDtypeStruct((batch * seq, di), BF16),
        scratch_shapes=scratch,
        compiler_params=_params("parallel", "parallel", "arbitrary"),
        name="diff_attention",
    )(*args)


def _win_attn_kernel(q_ref, kp_ref, kc_ref, kn_ref, vp_ref, vc_ref, vn_ref, kx_ref, vx_ref, sink_ref, z_ref, o_ref,
                     *, group, scale):
    i = pl.program_id(1)
    blk = ATTN_BLOCK
    q = q_ref[...]
    q4 = jnp.concatenate([q[:, g * LANES:(g + 1) * LANES] for g in range(group)], axis=0)
    qpos = lax.broadcasted_iota(jnp.int32, (group * blk, blk), 0) % blk
    kpos = lax.broadcasted_iota(jnp.int32, (group * blk, blk), 1)
    s_prev = jnp.where((kpos >= qpos) & (i > 0), _dot_nt(q4, kp_ref[...]) * scale, NEG_INF)
    s_cur = _dot_nt(q4, kc_ref[...]) * scale
    s_next = jnp.where((kpos <= qpos) & (i < pl.num_programs(1) - 1), _dot_nt(q4, kn_ref[...]) * scale, NEG_INF)
    s_ctx = _dot_nt(q4, kx_ref[...]) * scale
    sink = sink_ref[0]
    m = sink
    for s in (s_prev, s_cur, s_next, s_ctx):
        m = jnp.maximum(m, jnp.max(s, axis=-1, keepdims=True))
    l = jnp.exp(sink - m)
    o = jnp.zeros((group * blk, LANES), F32)
    for s, v_ref in ((s_prev, vp_ref), (s_cur, vc_ref), (s_next, vn_ref), (s_ctx, vx_ref)):
        p = jnp.exp(s - m)
        l = l + jnp.sum(p, axis=-1, keepdims=True)
        o = o + _dot(p.astype(BF16), v_ref[...])
    o = o / l
    o = jnp.concatenate([o[g * blk:(g + 1) * blk] for g in range(group)], axis=1)
    o_ref[...] = (o * jax.nn.silu(z_ref[...].astype(F32))).astype(BF16)


def _win_attention(qkvz, seq, batch, ctx_qkvz, ctx_len, di, kv_heads, sink):
    heads = di // LANES
    group = heads // kv_heads
    gw = group * LANES
    blk = ATTN_BLOCK
    nb = seq // blk
    k_col0 = di // LANES
    v_col0 = k_col0 + kv_heads
    z_col0 = (di + 2 * kv_heads * LANES) // gw
    assert (di + 2 * kv_heads * LANES) % gw == 0
    sink_rows = jnp.repeat(sink.astype(F32).reshape(kv_heads, group), blk, axis=1).reshape(kv_heads, group * blk, 1)
    band = lambda col0, shift: pl.BlockSpec(
        (blk, LANES), lambda b, i, n: (b * nb + jnp.clip(i + shift, 0, nb - 1), col0 + n))
    return pl.pallas_call(
        functools.partial(_win_attn_kernel, group=group, scale=LANES ** -0.5),
        grid=(batch, nb, kv_heads),
        in_specs=[
            pl.BlockSpec((blk, gw), lambda b, i, n: (b * nb + i, n)),
            band(k_col0, -1), band(k_col0, 0), band(k_col0, 1),
            band(v_col0, -1), band(v_col0, 0), band(v_col0, 1),
            pl.BlockSpec((ctx_len, LANES), lambda b, i, n: (b, k_col0 + n)),
            pl.BlockSpec((ctx_len, LANES), lambda b, i, n: (b, v_col0 + n)),
            pl.BlockSpec((1, group * blk, 1), lambda b, i, n: (n, 0, 0)),
            pl.BlockSpec((blk, gw), lambda b, i, n: (b * nb + i, z_col0 + n)),
        ],
        out_specs=pl.BlockSpec((blk, gw), lambda b, i, n: (b * nb + i, n)),
        out_shape=jax.ShapeDtypeStruct((batch * seq, di), BF16),
        compiler_params=_params("parallel", "parallel", "arbitrary"),
        name="win_attention",
    )(qkvz, qkvz, qkvz, qkvz, qkvz, qkvz, qkvz, ctx_qkvz, ctx_qkvz, sink_rows, qkvz)


def _x_mod_row(i, tiles_per_seq):
    return i // tiles_per_seq


def kernel(x, c, ctx, c_ctx, l0_norm, l0_w_mod, l0_b_mod, l0_w_in, l0_conv_w, l0_conv_b, l0_w_out, l1_norm, l1_w_mod, l1_b_mod, l1_w_in, l1_q_norm, l1_k_norm, l1_lam_q1, l1_lam_k1, l1_lam_q2, l1_lam_k2, l1_sub_norm, l1_w_out, l2_norm, l2_w_mod, l2_b_mod, l2_w_in, l2_q_norm, l2_k_norm, l2_sink, l2_w_out, l3_norm, l3_w_mod, l3_b_mod, l3_w_in, l3_conv_w, l3_conv_b, l3_w_out):
    batch, seq, d = x.shape
    ctx_len = ctx.shape[1]
    di = l0_w_out.shape[0]
    assert batch < MOD_ROWS
    ctx_mod_row = lambda i, tiles_per_seq: batch

    x2 = x.reshape(batch * seq, d)
    c2 = ctx.reshape(batch * ctx_len, d)
    cc = jnp.zeros((MOD_ROWS, d), F32).at[:batch].set(c).at[batch].set(c_ctx)
    mod0, mod1, mod2, mod3 = (_modulation(cc, w, b) for w, b in
                              ((l0_w_mod, l0_b_mod), (l1_w_mod, l1_b_mod), (l2_w_mod, l2_b_mod), (l3_w_mod, l3_b_mod)))
    bf = lambda w: w.astype(BF16)

    w_in, w_out = bf(l0_w_in), bf(l0_w_out)
    x2n = _conv_layer(x2, seq, mod0, _x_mod_row, l0_norm, w_in, l0_conv_w, l0_conv_b, w_out)
    c2 = _conv_layer(c2, ctx_len, mod0, ctx_mod_row, l0_norm, w_in, l0_conv_w, l0_conv_b, w_out)
    x2 = x2n

    w_in, w_out = bf(l1_w_in), bf(l1_w_out)
    lam_init = 0.8 - 0.6 * math.exp(-0.3 * 1)
    q_scale = l1_q_norm.shape[0] ** -0.5 * math.log2(math.e)
    proj = functools.partial(_proj_layer, norm_g=l1_norm, w_in=w_in, q_norm=l1_q_norm, k_norm=l1_k_norm,
                             q_cols=di, k_cols=di, q_scale=q_scale)
    qx = proj(x2, seq, mod1, _x_mod_row, rope=True)
    qc = proj(c2, ctx_len, mod1, ctx_mod_row, rope=False)
    lam_vecs = jnp.stack([l1_lam_q1, l1_lam_k1, l1_lam_q2, l1_lam_k2]).astype(F32)
    ax = _diff_attention(qx, seq, batch, qc, ctx_len, lam_vecs, l1_sub_norm, lam_init)
    ac = _diff_attention(qc, ctx_len, batch, None, 0, lam_vecs, l1_sub_norm, lam_init)
    x2 = _out_layer(ax, x2, seq, mod1, _x_mod_row, w_out)
    c2 = _out_layer(ac, c2, ctx_len, mod1, ctx_mod_row, w_out)

    w_in, w_out = bf(l2_w_in), bf(l2_w_out)
    kv_cols = (l2_w_in.shape[1] - 2 * di) // 2
    proj = functools.partial(_proj_layer, norm_g=l2_norm, w_in=w_in, q_norm=l2_q_norm, k_norm=l2_k_norm,
                             q_cols=di, k_cols=kv_cols, q_scale=1.0)
    qx = proj(x2, seq, mod2, _x_mod_row, rope=True)
    qc = proj(c2, ctx_len, mod2, ctx_mod_row, rope=False)
    ax = _win_attention(qx, seq, batch, qc, ctx_len, di, kv_cols // LANES, l2_sink)
    x2 = _out_layer(ax, x2, seq, mod2, _x_mod_row, w_out)

    x2 = _conv_layer(x2, seq, mod3, _x_mod_row, l3_norm, bf(l3_w_in), l3_conv_w, l3_conv_b, bf(l3_w_out))
    return x2.reshape(batch, seq, d)
```

```python
import functools
import math

import jax
import jax.numpy as jnp
from jax import lax
from jax.experimental import pallas as pl
from jax.experimental.pallas import tpu as pltpu

LANES = 128
BF16_ROWS = 16
GRID_W = 64
ROPE_BASE = 10000.0
EPS = 1e-6
NEG_INF = -1e30
ATTN_BLOCK = 128
MOD_ROWS = 8
VMEM_LIMIT = 56 * 1024 * 1024

F32 = jnp.float32
BF16 = jnp.bfloat16


def _params(*sem):
    return pltpu.CompilerParams(dimension_semantics=sem, vmem_limit_bytes=VMEM_LIMIT)


def _tile(n, target):
    if n <= target:
        return n
    t = target - target % LANES
    while n % t:
        t -= LANES
    return t


def _dot(a, b):
    return jnp.dot(a, b, preferred_element_type=F32)


def _dot_nt(a, b):
    return lax.dot_general(a, b, (((1,), (1,)), ((), ())), preferred_element_type=F32)


def _norm_mod(t, g, shift, scale):
    ms = jnp.mean(t * t, axis=-1, keepdims=True)
    y = t * lax.rsqrt(ms + EPS) * g
    return (y * (1.0 + scale) + shift).astype(BF16)


def _mod_kernel(c_ref, w_ref, b_ref, o_ref):
    a = jax.nn.silu(c_ref[...]).astype(BF16)
    o_ref[...] = _dot(a, w_ref[...].astype(BF16)) + b_ref[...]


def _modulation(cc, w_mod, b_mod):
    d, n = w_mod.shape
    tn = _tile(n, 768)
    out = pl.pallas_call(
        _mod_kernel,
        grid=(n // tn,),
        in_specs=[
            pl.BlockSpec((MOD_ROWS, d), lambda j: (0, 0)),
            pl.BlockSpec((d, tn), lambda j: (0, j)),
            pl.BlockSpec((1, tn), lambda j: (0, j)),
        ],
        out_specs=pl.BlockSpec((MOD_ROWS, tn), lambda j: (0, j)),
        out_shape=jax.ShapeDtypeStruct((MOD_ROWS, n), F32),
        compiler_params=_params("arbitrary"),
        name="modulation",
    )(cc, w_mod, b_mod.reshape(1, n))
    return out.reshape(MOD_ROWS, 1, n)


def _conv_kernel(xp_ref, x_ref, xn_ref, mod_ref, g_ref, wb_ref, wc_ref, wx_ref, wz_ref, cw_ref, cb_ref,
                 wo_ref, o_ref, h_scr, *, tm, d, tiles_per_seq):
    i = pl.program_id(0)
    j = pl.program_id(1)
    halo = BF16_ROWS
    mod = mod_ref[0]
    shift, scale, gate = mod[:, :d], mod[:, d:2 * d], mod[:, 2 * d:]

    @pl.when(j == 0)
    def _():
        g = g_ref[...]
        h_scr[0:halo, :] = _norm_mod(xp_ref[...], g, shift, scale)
        h_scr[halo:halo + tm, :] = _norm_mod(x_ref[...], g, shift, scale)
        h_scr[halo + tm:, :] = _norm_mod(xn_ref[...], g, shift, scale)
        o_ref[...] = jnp.zeros_like(o_ref)

    n = tm + 2 * halo
    h = h_scr[...]
    hm = h_scr[halo:halo + tm, :]
    u = _dot(h, wc_ref[...]) * _dot(h, wx_ref[...])
    pos = i % tiles_per_seq
    rows = lax.broadcasted_iota(jnp.int32, (n, 1), 0)
    keep = ((rows >= halo) | (pos != 0)) & ((rows < halo + tm) | (pos != tiles_per_seq - 1))
    u = jnp.where(keep, u, 0.0)
    u_prev = pltpu.roll(u, 1, 0)[halo:halo + tm]
    u_next = pltpu.roll(u, n - 1, 0)[halo:halo + tm]
    cw = cw_ref[...]
    y = u_prev * cw[0:1] + u[halo:halo + tm] * cw[1:2] + u_next * cw[2:3] + cb_ref[...]
    gated = _dot(hm, wb_ref[...]) * y * jax.nn.silu(_dot(hm, wz_ref[...]))
    o_ref[...] += _dot(gated.astype(BF16), wo_ref[...])

    @pl.when(j == pl.num_programs(1) - 1)
    def _():
        o_ref[...] = x_ref[...] + gate * o_ref[...]


def _conv_layer(x2, seq, mod, mod_row, norm_g, w_in, conv_w, conv_b, w_out):
    m, d = x2.shape
    di = w_out.shape[0]
    tm = min(seq, 512)
    tn = _tile(di, 256)
    nj = di // tn
    tiles_per_seq = seq // tm
    hb = tm // BF16_ROWS
    last_hb = m // BF16_ROWS - 1
    w_spec = lambda c: pl.BlockSpec((d, tn), lambda i, j: (0, c * nj + j))
    return pl.pallas_call(
        functools.partial(_conv_kernel, tm=tm, d=d, tiles_per_seq=tiles_per_seq),
        grid=(m // tm, nj),
        in_specs=[
            pl.BlockSpec((BF16_ROWS, d), lambda i, j: (jnp.maximum(i * hb - 1, 0), 0)),
            pl.BlockSpec((tm, d), lambda i, j: (i, 0)),
            pl.BlockSpec((BF16_ROWS, d), lambda i, j: (jnp.minimum((i + 1) * hb, last_hb), 0)),
            pl.BlockSpec((1, 1, 3 * d), lambda i, j: (mod_row(i, tiles_per_seq), 0, 0)),
            pl.BlockSpec((1, d), lambda i, j: (0, 0)),
            w_spec(0), w_spec(1), w_spec(2), w_spec(3),
            pl.BlockSpec((3, tn), lambda i, j: (0, j)),
            pl.BlockSpec((1, tn), lambda i, j: (0, j)),
            pl.BlockSpec((tn, d), lambda i, j: (j, 0)),
        ],
        out_specs=pl.BlockSpec((tm, d), lambda i, j: (i, 0)),
        out_shape=jax.ShapeDtypeStruct((m, d), F32),
        scratch_shapes=[pltpu.VMEM((tm + 2 * BF16_ROWS, d), BF16)],
        compiler_params=_params("parallel", "arbitrary"),
        name="conv_layer",
    )(x2, x2, x2, mod, norm_g.reshape(1, d), w_in, w_in, w_in, w_in, conv_w, conv_b.reshape(1, di), w_out)


def _head_norm_rope(acc, gn, cos, sin, group, out_scale):
    outs = []
    swap = group // 4
    for c in range(acc.shape[1] // LANES):
        t = acc[:, c * LANES:(c + 1) * LANES]
        lane = lax.broadcasted_iota(jnp.int32, t.shape, 1)
        sq = t * t
        if group == LANES:
            ms = jnp.mean(sq, axis=-1, keepdims=True)
        else:
            lo = lane < group
            s_lo = jnp.sum(jnp.where(lo, sq, 0.0), axis=-1, keepdims=True)
            s_hi = jnp.sum(jnp.where(lo, 0.0, sq), axis=-1, keepdims=True)
            ms = jnp.where(lo, s_lo, s_hi) * (1.0 / group)
        y = t * lax.rsqrt(ms + EPS) * gn
        if cos is not None:
            ahead = pltpu.roll(y, LANES - swap, 1)
            behind = pltpu.roll(y, swap, 1)
            y = y * cos + jnp.where((lane & swap) == 0, ahead, behind) * sin
        if out_scale != 1.0:
            y = y * out_scale
        outs.append(y)
    return outs[0] if len(outs) == 1 else jnp.concatenate(outs, axis=1)


def _proj_kernel(*refs, d, q_tiles, k_tiles, group, rope, q_scale):
    if rope:
        x_ref, mod_ref, g_ref, w_ref, qn_ref, kn_ref, cos_ref, sin_ref, o_ref, h_scr = refs
    else:
        x_ref, mod_ref, g_ref, w_ref, qn_ref, kn_ref, o_ref, h_scr = refs
    j = pl.program_id(1)

    @pl.when(j == 0)
    def _():
        mod = mod_ref[0]
        h_scr[...] = _norm_mod(x_ref[...], g_ref[...], mod[:, :d], mod[:, d:2 * d])

    acc = _dot(h_scr[...], w_ref[...])
    cos = cos_ref[...] if rope else None
    sin = sin_ref[...] if rope else None

    @pl.when(j < q_tiles)
    def _():
        o_ref[...] = _head_norm_rope(acc, qn_ref[...], cos, sin, group, q_scale).astype(BF16)

    @pl.when((j >= q_tiles) & (j < q_tiles + k_tiles))
    def _():
        o_ref[...] = _head_norm_rope(acc, kn_ref[...], cos, sin, group, 1.0).astype(BF16)

    @pl.when(j >= q_tiles + k_tiles)
    def _():
        o_ref[...] = acc.astype(BF16)


def _rope_tables(seq, group):
    rows = seq // GRID_W
    row = jnp.repeat(jnp.arange(rows), GRID_W).astype(F32)
    col = jnp.tile(jnp.arange(GRID_W), rows).astype(F32)
    n_freq = group // 4
    inv_freq = ROPE_BASE ** (-(jnp.arange(n_freq, dtype=F32) / n_freq))
    ar, ac = row[:, None] * inv_freq, col[:, None] * inv_freq
    cos = jnp.concatenate([jnp.cos(ar), jnp.cos(ar), jnp.cos(ac), jnp.cos(ac)], axis=-1)
    sin = jnp.concatenate([-jnp.sin(ar), jnp.sin(ar), -jnp.sin(ac), jnp.sin(ac)], axis=-1)
    reps = LANES // group
    return jnp.tile(cos, (1, reps)), jnp.tile(sin, (1, reps))


def _proj_layer(x2, seq, mod, mod_row, norm_g, w_in, q_norm, k_norm, q_cols, k_cols, rope, q_scale):
    m, d = x2.shape
    n = w_in.shape[1]
    group = q_norm.shape[0]
    tm = min(seq, 512)
    tn = math.gcd(math.gcd(q_cols, k_cols), 512)
    tiles_per_seq = seq // tm
    reps = LANES // group
    args = [x2, mod, norm_g.reshape(1, d), w_in,
            jnp.tile(q_norm, reps).reshape(1, LANES), jnp.tile(k_norm, reps).reshape(1, LANES)]
    in_specs = [
        pl.BlockSpec((tm, d), lambda i, j: (i, 0)),
        pl.BlockSpec((1, 1, 3 * d), lambda i, j: (mod_row(i, tiles_per_seq), 0, 0)),
        pl.BlockSpec((1, d), lambda i, j: (0, 0)),
        pl.BlockSpec((d, tn), lambda i, j: (0, j)),
        pl.BlockSpec((1, LANES), lambda i, j: (0, 0)),
        pl.BlockSpec((1, LANES), lambda i, j: (0, 0)),
    ]
    if rope:
        args += list(_rope_tables(seq, group))
        in_specs += [pl.BlockSpec((tm, LANES), lambda i, j: (i % tiles_per_seq, 0))] * 2
    return pl.pallas_call(
        functools.partial(_proj_kernel, d=d, q_tiles=q_cols // tn, k_tiles=k_cols // tn, group=group,
                          rope=rope, q_scale=q_scale),
        grid=(m // tm, n // tn),
        in_specs=in_specs,
        out_specs=pl.BlockSpec((tm, tn), lambda i, j: (i, j)),
        out_shape=jax.ShapeDtypeStruct((m, n), BF16),
        scratch_shapes=[pltpu.VMEM((tm, d), BF16)],
        compiler_params=_params("parallel", "arbitrary"),
        name="proj_layer",
    )(*args)


def _out_kernel(a_ref, w_ref, x_ref, gate_ref, o_ref):
    o_ref[...] = x_ref[...] + gate_ref[0] * _dot(a_ref[...], w_ref[...])


def _out_layer(a2, x2, seq, mod, mod_row, w_out):
    m, d = x2.shape
    di = a2.shape[1]
    tm = min(seq, 512)
    tn = _tile(d, 512)
    tiles_per_seq = seq // tm
    gate_block0 = 2 * d // tn
    return pl.pallas_call(
        _out_kernel,
        grid=(m // tm, d // tn),
        in_specs=[
            pl.BlockSpec((tm, di), lambda i, j: (i, 0)),
            pl.BlockSpec((di, tn), lambda i, j: (0, j)),
            pl.BlockSpec((tm, tn), lambda i, j: (i, j)),
            pl.BlockSpec((1, 1, tn), lambda i, j: (mod_row(i, tiles_per_seq), 0, gate_block0 + j)),
        ],
        out_specs=pl.BlockSpec((tm, tn), lambda i, j: (i, j)),
        out_shape=jax.ShapeDtypeStruct((m, d), F32),
        compiler_params=_params("parallel", "arbitrary"),
        name="out_layer",
    )(a2, w_out, x2, mod)


def _lane_chunks(t):
    return [t[:, c * LANES:(c + 1) * LANES] for c in range(t.shape[1] // LANES)]


def _diff_attn_kernel(*refs, tq, tk, n_chunks, has_ctx, tiles, lam_init):
    if tiles == 2:
        lam_ref, q_ref, qn_ref = refs[:3]
        rest = refs[3:]
    else:
        lam_ref, q_ref = refs[:2]
        qn_ref = None
        rest = refs[2:]
    if has_ctx:
        k_ref, v_ref, kc_ref, vc_ref, z_ref, sn_ref, o_ref = rest[:7]
        bufs = rest[7:]
        bufs = [(bufs[3 * t], bufs[3 * t + 1], bufs[3 * t + 2]) for t in range(tiles)]
    else:
        k_ref, v_ref, z_ref, sn_ref, o_ref = rest[:5]
        kc_ref = vc_ref = None
        bufs = rest[5:]
        bufs = [(bufs[2 * t], None, bufs[2 * t + 1]) for t in range(tiles)]
    half = LANES // 2

    def scores(q, buf):
        s_buf, c_buf, m_buf = buf
        lane = lax.broadcasted_iota(jnp.int32, q.shape, 1)
        zero = jnp.zeros_like(q)
        q2 = jnp.concatenate([jnp.where(lane < half, q, zero), jnp.where(lane < half, zero, q)], axis=0)
        mx = jnp.full((2 * tq, LANES), NEG_INF, F32)
        for c in range(n_chunks):
            s = _dot_nt(q2, k_ref[c * tk:(c + 1) * tk, :])
            s_buf[c] = s
            for t in _lane_chunks(s):
                mx = jnp.maximum(mx, t)
        if has_ctx:
            s = _dot_nt(q2, kc_ref[...])
            c_buf[...] = s
            for t in _lane_chunks(s):
                mx = jnp.maximum(mx, t)
        m_buf[...] = jnp.broadcast_to(jnp.max(mx, axis=-1, keepdims=True), (2 * tq, LANES))

    def weighted(buf):
        s_buf, c_buf, m_buf = buf
        m = m_buf[...]
        lsum = jnp.zeros((2 * tq, LANES), F32)
        acc = jnp.zeros((2 * tq, LANES), F32)
        blocks = [(s_buf[c], v_ref[c * tk:(c + 1) * tk, :]) for c in range(n_chunks)]
        if has_ctx:
            blocks.append((c_buf[...], vc_ref[...]))
        for s, v in blocks:
            ps = [jnp.exp2(t - m) for t in _lane_chunks(s)]
            for t in ps:
                lsum = lsum + t
            acc = acc + _dot(jnp.concatenate(ps, axis=1).astype(BF16), v)
        return acc / jnp.sum(lsum, axis=-1, keepdims=True)

    lv = lam_ref[...]
    lam = (jnp.exp(jnp.sum(lv[0:1] * lv[1:2], axis=-1, keepdims=True))
           - jnp.exp(jnp.sum(lv[2:3] * lv[3:4], axis=-1, keepdims=True)) + lam_init)

    def finish(o_all, z):
        o = o_all[:tq] - lam * o_all[tq:]
        ms = jnp.mean(o * o, axis=-1, keepdims=True)
        y = o * lax.rsqrt(ms + EPS) * sn_ref[...] * (1.0 - lam_init)
        return (y * jax.nn.silu(z.astype(F32))).astype(BF16)

    if tiles == 1:
        scores(q_ref[...], bufs[0])
        o_ref[...] = finish(weighted(bufs[0]), z_ref[...])
        return

    @pl.when(pl.program_id(2) == 0)
    def _():
        scores(q_ref[0:tq, :], bufs[0])

    scores(q_ref[tq:2 * tq, :], bufs[1])
    o_ref[0:tq, :] = finish(weighted(bufs[0]), z_ref[0:tq, :])
    scores(qn_ref[...], bufs[0])
    o_ref[tq:2 * tq, :] = finish(weighted(bufs[1]), z_ref[tq:2 * tq, :])


def _diff_attention(qkvz, seq, batch, ctx_qkvz, ctx_len, lam_vecs, sub_norm, lam_init):
    di = qkvz.shape[1] // 4
    heads = di // LANES
    tq = min(seq, 256)
    tk = min(seq, 512)
    tiles = 2 if seq >= 2 * tq else 1
    rows = tiles * tq
    nq = seq // rows
    has_ctx = ctx_qkvz is not None
    args = [lam_vecs, qkvz]
    in_specs = [
        pl.BlockSpec(lam_vecs.shape, lambda b, h, i: (0, 0)),
        pl.BlockSpec((rows, LANES), lambda b, h, i: (b * nq + i, h)),
    ]
    if tiles == 2:
        args += [qkvz]
        in_specs += [pl.BlockSpec((tq, LANES), lambda b, h, i: (tiles * (b * nq + jnp.minimum(i + 1, nq - 1)), h))]
    args += [qkvz, qkvz]
    in_specs += [
        pl.BlockSpec((seq, LANES), lambda b, h, i: (b, heads + h)),
        pl.BlockSpec((seq, LANES), lambda b, h, i: (b, 2 * heads + h)),
    ]
    if has_ctx:
        args += [ctx_qkvz, ctx_qkvz]
        in_specs += [
            pl.BlockSpec((ctx_len, LANES), lambda b, h, i: (b, heads + h)),
            pl.BlockSpec((ctx_len, LANES), lambda b, h, i: (b, 2 * heads + h)),
        ]
    args += [qkvz, sub_norm.reshape(1, LANES)]
    in_specs += [
        pl.BlockSpec((rows, LANES), lambda b, h, i: (b * nq + i, 3 * heads + h)),
        pl.BlockSpec((1, LANES), lambda b, h, i: (0, 0)),
    ]
    scratch = []
    for _ in range(tiles):
        scratch += [pltpu.VMEM((seq // tk, 2 * tq, tk), F32)]
        if has_ctx:
            scratch += [pltpu.VMEM((2 * tq, ctx_len), F32)]
        scratch += [pltpu.VMEM((2 * tq, LANES), F32)]
    return pl.pallas_call(
        functools.partial(_diff_attn_kernel, tq=tq, tk=tk, n_chunks=seq // tk, has_ctx=has_ctx, tiles=tiles,
                          lam_init=lam_init),
        grid=(batch, heads, nq),
        in_specs=in_specs,
        out_specs=pl.BlockSpec((rows, LANES), lambda b, h, i: (b * nq + i, h)),
        out_shape=jax.ShapeDtypeStruct((batch * seq, di), BF16),
        scratch_shapes=scratch,
        compiler_params=_params("parallel", "parallel", "arbitrary"),
        name="diff_attention",
    )(*args)


def _win_attn_kernel(q_ref, kp_ref, kc_ref, kn_ref, vp_ref, vc_ref, vn_ref, kx_ref, vx_ref, sink_ref, z_ref, o_ref,
                     *, group, scale):
    i = pl.program_id(1)
    blk = ATTN_BLOCK
    q = q_ref[...]
    q4 = jnp.concatenate([q[:, g * LANES:(g + 1) * LANES] for g in range(group)], axis=0)
    qpos = lax.broadcasted_iota(jnp.int32, (group * blk, blk), 0) % blk
    kpos = lax.broadcasted_iota(jnp.int32, (group * blk, blk), 1)
    s_prev = jnp.where((kpos >= qpos) & (i > 0), _dot_nt(q4, kp_ref[...]) * scale, NEG_INF)
    s_cur = _dot_nt(q4, kc_ref[...]) * scale
    s_next = jnp.where((kpos <= qpos) & (i < pl.num_programs(1) - 1), _dot_nt(q4, kn_ref[...]) * scale, NEG_INF)
    s_ctx = _dot_nt(q4, kx_ref[...]) * scale
    sink = sink_ref[0]
    m = sink
    for s in (s_prev, s_cur, s_next, s_ctx):
        m = jnp.maximum(m, jnp.max(s, axis=-1, keepdims=True))
    l = jnp.exp(sink - m)
    o = jnp.zeros((group * blk, LANES), F32)
    for s, v_ref in ((s_prev, vp_ref), (s_cur, vc_ref), (s_next, vn_ref), (s_ctx, vx_ref)):
        p = jnp.exp(s - m)
        l = l + jnp.sum(p, axis=-1, keepdims=True)
        o = o + _dot(p.astype(BF16), v_ref[...])
    o = o / l
    o = jnp.concatenate([o[g * blk:(g + 1) * blk] for g in range(group)], axis=1)
    o_ref[...] = (o * jax.nn.silu(z_ref[...].astype(F32))).astype(BF16)


def _win_attention(qkvz, seq, batch, ctx_qkvz, ctx_len, di, kv_heads, sink):
    heads = di // LANES
    group = heads // kv_heads
    gw = group * LANES
    blk = ATTN_BLOCK
    nb = seq // blk
    k_col0 = di // LANES
    v_col0 = k_col0 + kv_heads
    z_col0 = (di + 2 * kv_heads * LANES) // gw
    assert (di + 2 * kv_heads * LANES) % gw == 0
    sink_rows = jnp.repeat(sink.astype(F32).reshape(kv_heads, group), blk, axis=1).reshape(kv_heads, group * blk, 1)
    band = lambda col0, shift: pl.BlockSpec(
        (blk, LANES), lambda b, i, n: (b * nb + jnp.clip(i + shift, 0, nb - 1), col0 + n))
    return pl.pallas_call(
        functools.partial(_win_attn_kernel, group=group, scale=LANES ** -0.5),
        grid=(batch, nb, kv_heads),
        in_specs=[
            pl.BlockSpec((blk, gw), lambda b, i, n: (b * nb + i, n)),
            band(k_col0, -1), band(k_col0, 0), band(k_col0, 1),
            band(v_col0, -1), band(v_col0, 0), band(v_col0, 1),
            pl.BlockSpec((ctx_len, LANES), lambda b, i, n: (b, k_col0 + n)),
            pl.BlockSpec((ctx_len, LANES), lambda b, i, n: (b, v_col0 + n)),
            pl.BlockSpec((1, group * blk, 1), lambda b, i, n: (n, 0, 0)),
            pl.BlockSpec((blk, gw), lambda b, i, n: (b * nb + i, z_col0 + n)),
        ],
        out_specs=pl.BlockSpec((blk, gw), lambda b, i, n: (b * nb + i, n)),
        out_shape=jax.ShapeDtypeStruct((batch * seq, di), BF16),
        compiler_params=_params("parallel", "parallel", "arbitrary"),
        name="win_attention",
    )(qkvz, qkvz, qkvz, qkvz, qkvz, qkvz, qkvz, ctx_qkvz, ctx_qkvz, sink_rows, qkvz)


def _x_mod_row(i, tiles_per_seq):
    return i // tiles_per_seq


def kernel(x, c, ctx, c_ctx, l0_norm, l0_w_mod, l0_b_mod, l0_w_in, l0_conv_w, l0_conv_b, l0_w_out, l1_norm, l1_w_mod, l1_b_mod, l1_w_in, l1_q_norm, l1_k_norm, l1_lam_q1, l1_lam_k1, l1_lam_q2, l1_lam_k2, l1_sub_norm, l1_w_out, l2_norm, l2_w_mod, l2_b_mod, l2_w_in, l2_q_norm, l2_k_norm, l2_sink, l2_w_out, l3_norm, l3_w_mod, l3_b_mod, l3_w_in, l3_conv_w, l3_conv_b, l3_w_out):
    batch, seq, d = x.shape
    ctx_len = ctx.shape[1]
    di = l0_w_out.shape[0]
    assert batch < MOD_ROWS
    ctx_mod_row = lambda i, tiles_per_seq: batch

    x2 = x.reshape(batch * seq, d)
    c2 = ctx.reshape(batch * ctx_len, d)
    cc = jnp.zeros((MOD_ROWS, d), F32).at[:batch].set(c).at[batch].set(c_ctx)
    mod0, mod1, mod2, mod3 = (_modulation(cc, w, b) for w, b in
                              ((l0_w_mod, l0_b_mod), (l1_w_mod, l1_b_mod), (l2_w_mod, l2_b_mod), (l3_w_mod, l3_b_mod)))
    bf = lambda w: w.astype(BF16)

    w_in, w_out = bf(l0_w_in), bf(l0_w_out)
    x2n = _conv_layer(x2, seq, mod0, _x_mod_row, l0_norm, w_in, l0_conv_w, l0_conv_b, w_out)
    c2 = _conv_layer(c2, ctx_len, mod0, ctx_mod_row, l0_norm, w_in, l0_conv_w, l0_conv_b, w_out)
    x2 = x2n

    w_in, w_out = bf(l1_w_in), bf(l1_w_out)
    lam_init = 0.8 - 0.6 * math.exp(-0.3 * 1)
    q_scale = l1_q_norm.shape[0] ** -0.5 * math.log2(math.e)
    proj = functools.partial(_proj_layer, norm_g=l1_norm, w_in=w_in, q_norm=l1_q_norm, k_norm=l1_k_norm,
                             q_cols=di, k_cols=di, q_scale=q_scale)
    qx = proj(x2, seq, mod1, _x_mod_row, rope=True)
    qc = proj(c2, ctx_len, mod1, ctx_mod_row, rope=False)
    lam_vecs = jnp.stack([l1_lam_q1, l1_lam_k1, l1_lam_q2, l1_lam_k2]).astype(F32)
    ax = _diff_attention(qx, seq, batch, qc, ctx_len, lam_vecs, l1_sub_norm, lam_init)
    ac = _diff_attention(qc, ctx_len, batch, None, 0, lam_vecs, l1_sub_norm, lam_init)
    x2 = _out_layer(ax, x2, seq, mod1, _x_mod_row, w_out)
    c2 = _out_layer(ac, c2, ctx_len, mod1, ctx_mod_row, w_out)

    w_in, w_out = bf(l2_w_in), bf(l2_w_out)
    kv_cols = (l2_w_in.shape[1] - 2 * di) // 2
    proj = functools.partial(_proj_layer, norm_g=l2_norm, w_in=w_in, q_norm=l2_q_norm, k_norm=l2_k_norm,
                             q_cols=di, k_cols=kv_cols, q_scale=1.0)
    qx = proj(x2, seq, mod2, _x_mod_row, rope=True)
    qc = proj(c2, ctx_len, mod2, ctx_mod_row, rope=False)
    ax = _win_attention(qx, seq, batch, qc, ctx_len, di, kv_cols // LANES, l2_sink)
    x2 = _out_layer(ax, x2, seq, mod2, _x_mod_row, w_out)

    x2 = _conv_layer(x2, seq, mod3, _x_mod_row, l3_norm, bf(l3_w_in), l3_conv_w, l3_conv_b, bf(l3_w_out))
    return x2.reshape(batch, seq, d)
```

```python
import functools
import math

import jax
import jax.numpy as jnp
from jax import lax
from jax.experimental import pallas as pl
from jax.experimental.pallas import tpu as pltpu

LANES = 128
MXU_COLS = 256
BF16_ROWS = 16
GRID_W = 64
ROPE_BASE = 10000.0
EPS = 1e-6
NEG_INF = -1e30
ATTN_BLOCK = 128
MOD_ROWS = 8
VMEM_LIMIT = 56 * 1024 * 1024

F32 = jnp.float32
BF16 = jnp.bfloat16


def _params(*sem):
    return pltpu.CompilerParams(dimension_semantics=sem, vmem_limit_bytes=VMEM_LIMIT)


def _tile(n, target):
    if n <= target:
        return n
    t = target - target % LANES
    while n % t:
        t -= LANES
    return t


def _dot(a, b):
    return jnp.dot(a, b, preferred_element_type=F32)


def _dot_nt(a, b):
    return lax.dot_general(a, b, (((1,), (1,)), ((), ())), preferred_element_type=F32)


def _norm_mod(t, g, shift, scale):
    ms = jnp.mean(t * t, axis=-1, keepdims=True)
    y = t * lax.rsqrt(ms + EPS) * g
    return (y * (1.0 + scale) + shift).astype(BF16)


def _mod_kernel(c_ref, w_ref, b_ref, o_ref):
    a = jax.nn.silu(c_ref[...]).astype(BF16)
    o_ref[...] = _dot(a, w_ref[...].astype(BF16)) + b_ref[...]


def _modulation(cc, w_mod, b_mod):
    d, n = w_mod.shape
    tn = _tile(n, 768)
    out = pl.pallas_call(
        _mod_kernel,
        grid=(n // tn,),
        in_specs=[
            pl.BlockSpec((MOD_ROWS, d), lambda j: (0, 0)),
            pl.BlockSpec((d, tn), lambda j: (0, j)),
            pl.BlockSpec((1, tn), lambda j: (0, j)),
        ],
        out_specs=pl.BlockSpec((MOD_ROWS, tn), lambda j: (0, j)),
        out_shape=jax.ShapeDtypeStruct((MOD_ROWS, n), F32),
        compiler_params=_params("arbitrary"),
        name="modulation",
    )(cc, w_mod, b_mod.reshape(1, n))
    return out.reshape(MOD_ROWS, 1, n)


def _conv_kernel(xp_ref, x_ref, xn_ref, mod_ref, g_ref, wb_ref, wc_ref, wx_ref, wz_ref, cw_ref, cb_ref,
                 wo_ref, o_ref, h_scr, *, tm, d, tiles_per_seq):
    i = pl.program_id(0)
    j = pl.program_id(1)
    halo = BF16_ROWS
    mod = mod_ref[0]
    shift, scale, gate = mod[:, :d], mod[:, d:2 * d], mod[:, 2 * d:]

    @pl.when(j == 0)
    def _():
        g = g_ref[...]
        h_scr[0:halo, :] = _norm_mod(xp_ref[...], g, shift, scale)
        h_scr[halo:halo + tm, :] = _norm_mod(x_ref[...], g, shift, scale)
        h_scr[halo + tm:, :] = _norm_mod(xn_ref[...], g, shift, scale)
        o_ref[...] = jnp.zeros_like(o_ref)

    n = tm + 2 * halo
    h = h_scr[...]
    hm = h_scr[halo:halo + tm, :]
    u = _dot(h, wc_ref[...]) * _dot(h, wx_ref[...])
    pos = i % tiles_per_seq
    rows = lax.broadcasted_iota(jnp.int32, (n, 1), 0)
    keep = ((rows >= halo) | (pos != 0)) & ((rows < halo + tm) | (pos != tiles_per_seq - 1))
    u = jnp.where(keep, u, 0.0)
    u_prev = pltpu.roll(u, 1, 0)[halo:halo + tm]
    u_next = pltpu.roll(u, n - 1, 0)[halo:halo + tm]
    cw = cw_ref[...]
    y = u_prev * cw[0:1] + u[halo:halo + tm] * cw[1:2] + u_next * cw[2:3] + cb_ref[...]
    gated = _dot(hm, wb_ref[...]) * y * jax.nn.silu(_dot(hm, wz_ref[...]))
    o_ref[...] += _dot(gated.astype(BF16), wo_ref[...])

    @pl.when(j == pl.num_programs(1) - 1)
    def _():
        o_ref[...] = x_ref[...] + gate * o_ref[...]


def _conv_layer(x2, seq, mod, mod_row, norm_g, w_in, conv_w, conv_b, w_out):
    m, d = x2.shape
    di = w_out.shape[0]
    tm = min(seq, 512)
    tn = _tile(di, 256)
    nj = di // tn
    tiles_per_seq = seq // tm
    hb = tm // BF16_ROWS
    last_hb = m // BF16_ROWS - 1
    w_spec = lambda c: pl.BlockSpec((d, tn), lambda i, j: (0, c * nj + j))
    return pl.pallas_call(
        functools.partial(_conv_kernel, tm=tm, d=d, tiles_per_seq=tiles_per_seq),
        grid=(m // tm, nj),
        in_specs=[
            pl.BlockSpec((BF16_ROWS, d), lambda i, j: (jnp.maximum(i * hb - 1, 0), 0)),
            pl.BlockSpec((tm, d), lambda i, j: (i, 0)),
            pl.BlockSpec((BF16_ROWS, d), lambda i, j: (jnp.minimum((i + 1) * hb, last_hb), 0)),
            pl.BlockSpec((1, 1, 3 * d), lambda i, j: (mod_row(i, tiles_per_seq), 0, 0)),
            pl.BlockSpec((1, d), lambda i, j: (0, 0)),
            w_spec(0), w_spec(1), w_spec(2), w_spec(3),
            pl.BlockSpec((3, tn), lambda i, j: (0, j)),
            pl.BlockSpec((1, tn), lambda i, j: (0, j)),
            pl.BlockSpec((tn, d), lambda i, j: (j, 0)),
        ],
        out_specs=pl.BlockSpec((tm, d), lambda i, j: (i, 0)),
        out_shape=jax.ShapeDtypeStruct((m, d), F32),
        scratch_shapes=[pltpu.VMEM((tm + 2 * BF16_ROWS, d), BF16)],
        compiler_params=_params("parallel", "arbitrary"),
        name="conv_layer",
    )(x2, x2, x2, mod, norm_g.reshape(1, d), w_in, w_in, w_in, w_in, conv_w, conv_b.reshape(1, di), w_out)


def _head_norm_rope(acc, gn, cos, sin, group, out_scale):
    outs = []
    swap = group // 4
    for c in range(acc.shape[1] // LANES):
        t = acc[:, c * LANES:(c + 1) * LANES]
        lane = lax.broadcasted_iota(jnp.int32, t.shape, 1)
        sq = t * t
        if group == LANES:
            ms = jnp.mean(sq, axis=-1, keepdims=True)
        else:
            lo = lane < group
            s_lo = jnp.sum(jnp.where(lo, sq, 0.0), axis=-1, keepdims=True)
            s_hi = jnp.sum(jnp.where(lo, 0.0, sq), axis=-1, keepdims=True)
            ms = jnp.where(lo, s_lo, s_hi) * (1.0 / group)
        y = t * lax.rsqrt(ms + EPS) * gn
        if cos is not None:
            ahead = pltpu.roll(y, LANES - swap, 1)
            behind = pltpu.roll(y, swap, 1)
            y = y * cos + jnp.where((lane & swap) == 0, ahead, behind) * sin
        if out_scale != 1.0:
            y = y * out_scale
        outs.append(y)
    return outs[0] if len(outs) == 1 else jnp.concatenate(outs, axis=1)


def _proj_kernel(*refs, d, q_tiles, k_tiles, group, rope, q_scale):
    if rope:
        x_ref, mod_ref, g_ref, w_ref, qn_ref, kn_ref, cos_ref, sin_ref, o_ref, h_scr = refs
    else:
        x_ref, mod_ref, g_ref, w_ref, qn_ref, kn_ref, o_ref, h_scr = refs
    j = pl.program_id(1)

    @pl.when(j == 0)
    def _():
        mod = mod_ref[0]
        h_scr[...] = _norm_mod(x_ref[...], g_ref[...], mod[:, :d], mod[:, d:2 * d])

    cos = cos_ref[...] if rope else None
    sin = sin_ref[...] if rope else None
    tn = o_ref.shape[1]
    cw = min(tn, MXU_COLS)

    def columns(epilogue):
        h = h_scr[...]
        for c in range(tn // cw):
            cols = slice(c * cw, (c + 1) * cw)
            o_ref[:, cols] = epilogue(_dot(h, w_ref[:, cols])).astype(BF16)

    @pl.when(j < q_tiles)
    def _():
        columns(lambda acc: _head_norm_rope(acc, qn_ref[...], cos, sin, group, q_scale))

    @pl.when((j >= q_tiles) & (j < q_tiles + k_tiles))
    def _():
        columns(lambda acc: _head_norm_rope(acc, kn_ref[...], cos, sin, group, 1.0))

    @pl.when(j >= q_tiles + k_tiles)
    def _():
        columns(lambda acc: acc)


def _rope_tables(seq, group):
    rows = seq // GRID_W
    row = jnp.repeat(jnp.arange(rows), GRID_W).astype(F32)
    col = jnp.tile(jnp.arange(GRID_W), rows).astype(F32)
    n_freq = group // 4
    inv_freq = ROPE_BASE ** (-(jnp.arange(n_freq, dtype=F32) / n_freq))
    ar, ac = row[:, None] * inv_freq, col[:, None] * inv_freq
    cos = jnp.concatenate([jnp.cos(ar), jnp.cos(ar), jnp.cos(ac), jnp.cos(ac)], axis=-1)
    sin = jnp.concatenate([-jnp.sin(ar), jnp.sin(ar), -jnp.sin(ac), jnp.sin(ac)], axis=-1)
    reps = LANES // group
    return jnp.tile(cos, (1, reps)), jnp.tile(sin, (1, reps))


def _proj_layer(x2, seq, mod, mod_row, norm_g, w_in, q_norm, k_norm, q_cols, k_cols, rope, q_scale):
    m, d = x2.shape
    n = w_in.shape[1]
    group = q_norm.shape[0]
    tm = min(seq, 512)
    tn = math.gcd(math.gcd(q_cols, k_cols), 1024)
    tiles_per_seq = seq // tm
    reps = LANES // group
    args = [x2, mod, norm_g.reshape(1, d), w_in,
            jnp.tile(q_norm, reps).reshape(1, LANES), jnp.tile(k_norm, reps).reshape(1, LANES)]
    in_specs = [
        pl.BlockSpec((tm, d), lambda i, j: (i, 0)),
        pl.BlockSpec((1, 1, 3 * d), lambda i, j: (mod_row(i, tiles_per_seq), 0, 0)),
        pl.BlockSpec((1, d), lambda i, j: (0, 0)),
        pl.BlockSpec((d, tn), lambda i, j: (0, j)),
        pl.BlockSpec((1, LANES), lambda i, j: (0, 0)),
        pl.BlockSpec((1, LANES), lambda i, j: (0, 0)),
    ]
    if rope:
        args += list(_rope_tables(seq, group))
        in_specs += [pl.BlockSpec((tm, LANES), lambda i, j: (i % tiles_per_seq, 0))] * 2
    return pl.pallas_call(
        functools.partial(_proj_kernel, d=d, q_tiles=q_cols // tn, k_tiles=k_cols // tn, group=group,
                          rope=rope, q_scale=q_scale),
        grid=(m // tm, n // tn),
        in_specs=in_specs,
        out_specs=pl.BlockSpec((tm, tn), lambda i, j: (i, j)),
        out_shape=jax.ShapeDtypeStruct((m, n), BF16),
        scratch_shapes=[pltpu.VMEM((tm, d), BF16)],
        compiler_params=_params("parallel", "arbitrary"),
        name="proj_layer",
    )(*args)


def _out_kernel(a_ref, w_ref, x_ref, gate_ref, o_ref):
    o_ref[...] = x_ref[...] + gate_ref[0] * _dot(a_ref[...], w_ref[...])


def _out_layer(a2, x2, seq, mod, mod_row, w_out):
    m, d = x2.shape
    di = a2.shape[1]
    tm = min(seq, 512)
    tn = _tile(d, 512)
    tiles_per_seq = seq // tm
    gate_block0 = 2 * d // tn
    return pl.pallas_call(
        _out_kernel,
        grid=(m // tm, d // tn),
        in_specs=[
            pl.BlockSpec((tm, di), lambda i, j: (i, 0)),
            pl.BlockSpec((di, tn), lambda i, j: (0, j)),
            pl.BlockSpec((tm, tn), lambda i, j: (i, j)),
            pl.BlockSpec((1, 1, tn), lambda i, j: (mod_row(i, tiles_per_seq), 0, gate_block0 + j)),
        ],
        out_specs=pl.BlockSpec((tm, tn), lambda i, j: (i, j)),
        out_shape=jax.ShapeDtypeStruct((m, d), F32),
        compiler_params=_params("parallel", "arbitrary"),
        name="out_layer",
    )(a2, w_out, x2, mod)


def _lane_chunks(t):
    return [t[:, c * LANES:(c + 1) * LANES] for c in range(t.shape[1] // LANES)]


def _diff_attn_kernel(*refs, tq, tk, n_chunks, has_ctx, tiles, lam_init):
    if tiles == 2:
        lam_ref, q_ref, qn_ref = refs[:3]
        rest = refs[3:]
    else:
        lam_ref, q_ref = refs[:2]
        qn_ref = None
        rest = refs[2:]
    if has_ctx:
        k_ref, v_ref, kc_ref, vc_ref, z_ref, sn_ref, o_ref = rest[:7]
        bufs = rest[7:]
        bufs = [(bufs[3 * t], bufs[3 * t + 1], bufs[3 * t + 2]) for t in range(tiles)]
    else:
        k_ref, v_ref, z_ref, sn_ref, o_ref = rest[:5]
        kc_ref = vc_ref = None
        bufs = rest[5:]
        bufs = [(bufs[2 * t], None, bufs[2 * t + 1]) for t in range(tiles)]
    half = LANES // 2

    def scores(q, buf):
        s_buf, c_buf, m_buf = buf
        lane = lax.broadcasted_iota(jnp.int32, q.shape, 1)
        zero = jnp.zeros_like(q)
        q2 = jnp.concatenate([jnp.where(lane < half, q, zero), jnp.where(lane < half, zero, q)], axis=0)
        mx = jnp.full((2 * tq, LANES), NEG_INF, F32)
        for c in range(n_chunks):
            s = _dot_nt(q2, k_ref[c * tk:(c + 1) * tk, :])
            s_buf[c] = s
            for t in _lane_chunks(s):
                mx = jnp.maximum(mx, t)
        if has_ctx:
            s = _dot_nt(q2, kc_ref[...])
            c_buf[...] = s
            for t in _lane_chunks(s):
                mx = jnp.maximum(mx, t)
        m_buf[...] = jnp.broadcast_to(jnp.max(mx, axis=-1, keepdims=True), (2 * tq, LANES))

    def weighted(buf):
        s_buf, c_buf, m_buf = buf
        m = m_buf[...]
        lsum = jnp.zeros((2 * tq, LANES), F32)
        acc = jnp.zeros((2 * tq, LANES), F32)
        blocks = [(s_buf[c], v_ref[c * tk:(c + 1) * tk, :]) for c in range(n_chunks)]
        if has_ctx:
            blocks.append((c_buf[...], vc_ref[...]))
        for s, v in blocks:
            ps = [jnp.exp2(t - m) for t in _lane_chunks(s)]
            for t in ps:
                lsum = lsum + t
            acc = acc + _dot(jnp.concatenate(ps, axis=1).astype(BF16), v)
        return acc / jnp.sum(lsum, axis=-1, keepdims=True)

    lv = lam_ref[...]
    lam = (jnp.exp(jnp.sum(lv[0:1] * lv[1:2], axis=-1, keepdims=True))
           - jnp.exp(jnp.sum(lv[2:3] * lv[3:4], axis=-1, keepdims=True)) + lam_init)

    def finish(o_all, z):
        o = o_all[:tq] - lam * o_all[tq:]
        ms = jnp.mean(o * o, axis=-1, keepdims=True)
        y = o * lax.rsqrt(ms + EPS) * sn_ref[...] * (1.0 - lam_init)
        return (y * jax.nn.silu(z.astype(F32))).astype(BF16)

    if tiles == 1:
        scores(q_ref[...], bufs[0])
        o_ref[...] = finish(weighted(bufs[0]), z_ref[...])
        return

    @pl.when(pl.program_id(2) == 0)
    def _():
        scores(q_ref[0:tq, :], bufs[0])

    scores(q_ref[tq:2 * tq, :], bufs[1])
    o_ref[0:tq, :] = finish(weighted(bufs[0]), z_ref[0:tq, :])
    scores(qn_ref[...], bufs[0])
    o_ref[tq:2 * tq, :] = finish(weighted(bufs[1]), z_ref[tq:2 * tq, :])


def _diff_attention(qkvz, seq, batch, ctx_qkvz, ctx_len, lam_vecs, sub_norm, lam_init):
    di = qkvz.shape[1] // 4
    heads = di // LANES
    tq = min(seq, 256)
    tk = min(seq, 512)
    tiles = 2 if seq >= 2 * tq else 1
    rows = tiles * tq
    nq = seq // rows
    has_ctx = ctx_qkvz is not None
    args = [lam_vecs, qkvz]
    in_specs = [
        pl.BlockSpec(lam_vecs.shape, lambda b, h, i: (0, 0)),
        pl.BlockSpec((rows, LANES), lambda b, h, i: (b * nq + i, h)),
    ]
    if tiles == 2:
        args += [qkvz]
        in_specs += [pl.BlockSpec((tq, LANES), lambda b, h, i: (tiles * (b * nq + jnp.minimum(i + 1, nq - 1)), h))]
    args += [qkvz, qkvz]
    in_specs += [
        pl.BlockSpec((seq, LANES), lambda b, h, i: (b, heads + h)),
        pl.BlockSpec((seq, LANES), lambda b, h, i: (b, 2 * heads + h)),
    ]
    if has_ctx:
        args += [ctx_qkvz, ctx_qkvz]
        in_specs += [
            pl.BlockSpec((ctx_len, LANES), lambda b, h, i: (b, heads + h)),
            pl.BlockSpec((ctx_len, LANES), lambda b, h, i: (b, 2 * heads + h)),
        ]
    args += [qkvz, sub_norm.reshape(1, LANES)]
    in_specs += [
        pl.BlockSpec((rows, LANES), lambda b, h, i: (b * nq + i, 3 * heads + h)),
        pl.BlockSpec((1, LANES), lambda b, h, i: (0, 0)),
    ]
    scratch = []
    for _ in range(tiles):
        scratch += [pltpu.VMEM((seq // tk, 2 * tq, tk), F32)]
        if has_ctx:
            scratch += [pltpu.VMEM((2 * tq, ctx_len), F32)]
        scratch += [pltpu.VMEM((2 * tq, LANES), F32)]
    return pl.pallas_call(
        functools.partial(_diff_attn_kernel, tq=tq, tk=tk, n_chunks=seq // tk, has_ctx=has_ctx, tiles=tiles,
                          lam_init=lam_init),
        grid=(batch, heads, nq),
        in_specs=in_specs,
        out_specs=pl.BlockSpec((rows, LANES), lambda b, h, i: (b * nq + i, h)),
        out_shape=jax.ShapeDtypeStruct((batch * seq, di), BF16),
        scratch_shapes=scratch,
        compiler_params=_params("parallel", "parallel", "arbitrary"),
        name="diff_attention",
    )(*args)


def _win_attn_kernel(*refs, group, kv_heads, n_z):
    q_ref, kp_ref, kc_ref, kn_ref, vp_ref, vc_ref, vn_ref, kx_ref, vx_ref, sink_ref = refs[:10]
    z_refs = refs[10:10 + n_z]
    o_ref = refs[10 + n_z]
    i = pl.program_id(1)
    blk = ATTN_BLOCK
    rows = group * blk
    zw = z_refs[0].shape[1]
    qpos = lax.broadcasted_iota(jnp.int32, (rows, blk), 0) % blk
    kpos = lax.broadcasted_iota(jnp.int32, (rows, blk), 1)
    keep_prev = (kpos >= qpos) & (i > 0)
    keep_next = (kpos <= qpos) & (i < pl.num_programs(1) - 1)
    sink2 = sink_ref[...] * math.log2(math.e)
    for n in range(kv_heads):
        ks = slice(n * LANES, (n + 1) * LANES)
        q4 = jnp.concatenate([q_ref[:, (n * group + g) * LANES:(n * group + g + 1) * LANES] for g in range(group)],
                             axis=0)
        k_all = jnp.concatenate([kp_ref[:, ks], kc_ref[:, ks], kn_ref[:, ks], kx_ref[:, ks]], axis=0)
        v_all = jnp.concatenate([vp_ref[:, ks], vc_ref[:, ks], vn_ref[:, ks], vx_ref[:, ks]], axis=0)
        s = _lane_chunks(_dot_nt(q4, k_all))
        s[0] = jnp.where(keep_prev, s[0], NEG_INF)
        s[2] = jnp.where(keep_next, s[2], NEG_INF)
        sink_b = jnp.concatenate([jnp.broadcast_to(sink2[n, g:g + 1, :], (blk, LANES)) for g in range(group)], axis=0)
        mx = sink_b
        for t in s:
            mx = jnp.maximum(mx, t)
        m = jnp.broadcast_to(jnp.max(mx, axis=-1, keepdims=True), (rows, LANES))
        ps = [jnp.exp2(t - m) for t in s]
        lsum = ps[0]
        for t in ps[1:]:
            lsum = lsum + t
        l = jnp.sum(lsum, axis=-1, keepdims=True) + jnp.exp2(sink_b - m)[:, 0:1]
        o = _dot(jnp.concatenate(ps, axis=1).astype(BF16), v_all) / l
        for g in range(group):
            col = (n * group + g) * LANES
            z = z_refs[col // zw][:, col % zw:col % zw + LANES].astype(F32)
            o_ref[:, col:col + LANES] = (o[g * blk:(g + 1) * blk] * jax.nn.silu(z)).astype(BF16)


def _win_attention(qkvz, seq, batch, ctx_qkvz, ctx_len, di, kv_heads, sink):
    heads = di // LANES
    group = heads // kv_heads
    kvw = kv_heads * LANES
    blk = ATTN_BLOCK
    nb = seq // blk
    z_col0 = di + 2 * kvw
    zw = math.gcd(z_col0, di)
    n_z = di // zw
    assert di % kvw == 0
    sink_rep = jnp.broadcast_to(sink.astype(F32).reshape(kv_heads, group, 1), (kv_heads, group, LANES))
    band = lambda col_block, shift: pl.BlockSpec(
        (blk, kvw), lambda b, i: (b * nb + jnp.clip(i + shift, 0, nb - 1), col_block))
    k_blk, v_blk = di // kvw, di // kvw + 1
    z_spec = lambda t: pl.BlockSpec((blk, zw), lambda b, i: (b * nb + i, z_col0 // zw + t))
    return pl.pallas_call(
        functools.partial(_win_attn_kernel, group=group, kv_heads=kv_heads, n_z=n_z),
        grid=(batch, nb),
        in_specs=[
            pl.BlockSpec((blk, di), lambda b, i: (b * nb + i, 0)),
            band(k_blk, -1), band(k_blk, 0), band(k_blk, 1),
            band(v_blk, -1), band(v_blk, 0), band(v_blk, 1),
            pl.BlockSpec((ctx_len, kvw), lambda b, i: (b, k_blk)),
            pl.BlockSpec((ctx_len, kvw), lambda b, i: (b, v_blk)),
            pl.BlockSpec((kv_heads, group, LANES), lambda b, i: (0, 0, 0)),
        ] + [z_spec(t) for t in range(n_z)],
        out_specs=pl.BlockSpec((blk, di), lambda b, i: (b * nb + i, 0)),
        out_shape=jax.ShapeDtypeStruct((batch * seq, di), BF16),
        compiler_params=_params("parallel", "arbitrary"),
        name="win_attention",
    )(*([qkvz] * 7 + [ctx_qkvz, ctx_qkvz, sink_rep] + [qkvz] * n_z))


def _x_mod_row(i, tiles_per_seq):
    return i // tiles_per_seq


def kernel(x, c, ctx, c_ctx, l0_norm, l0_w_mod, l0_b_mod, l0_w_in, l0_conv_w, l0_conv_b, l0_w_out, l1_norm, l1_w_mod, l1_b_mod, l1_w_in, l1_q_norm, l1_k_norm, l1_lam_q1, l1_lam_k1, l1_lam_q2, l1_lam_k2, l1_sub_norm, l1_w_out, l2_norm, l2_w_mod, l2_b_mod, l2_w_in, l2_q_norm, l2_k_norm, l2_sink, l2_w_out, l3_norm, l3_w_mod, l3_b_mod, l3_w_in, l3_conv_w, l3_conv_b, l3_w_out):
    batch, seq, d = x.shape
    ctx_len = ctx.shape[1]
    di = l0_w_out.shape[0]
    assert batch < MOD_ROWS
    ctx_mod_row = lambda i, tiles_per_seq: batch

    x2 = x.reshape(batch * seq, d)
    c2 = ctx.reshape(batch * ctx_len, d)
    cc = jnp.zeros((MOD_ROWS, d), F32).at[:batch].set(c).at[batch].set(c_ctx)
    mod0, mod1, mod2, mod3 = (_modulation(cc, w, b) for w, b in
                              ((l0_w_mod, l0_b_mod), (l1_w_mod, l1_b_mod), (l2_w_mod, l2_b_mod), (l3_w_mod, l3_b_mod)))
    bf = lambda w: w.astype(BF16)

    w_in, w_out = bf(l0_w_in), bf(l0_w_out)
    x2n = _conv_layer(x2, seq, mod0, _x_mod_row, l0_norm, w_in, l0_conv_w, l0_conv_b, w_out)
    c2 = _conv_layer(c2, ctx_len, mod0, ctx_mod_row, l0_norm, w_in, l0_conv_w, l0_conv_b, w_out)
    x2 = x2n

    w_in, w_out = bf(l1_w_in), bf(l1_w_out)
    lam_init = 0.8 - 0.6 * math.exp(-0.3 * 1)
    q_scale = l1_q_norm.shape[0] ** -0.5 * math.log2(math.e)
    proj = functools.partial(_proj_layer, norm_g=l1_norm, w_in=w_in, q_norm=l1_q_norm, k_norm=l1_k_norm,
                             q_cols=di, k_cols=di, q_scale=q_scale)
    qx = proj(x2, seq, mod1, _x_mod_row, rope=True)
    qc = proj(c2, ctx_len, mod1, ctx_mod_row, rope=False)
    lam_vecs = jnp.stack([l1_lam_q1, l1_lam_k1, l1_lam_q2, l1_lam_k2]).astype(F32)
    ax = _diff_attention(qx, seq, batch, qc, ctx_len, lam_vecs, l1_sub_norm, lam_init)
    ac = _diff_attention(qc, ctx_len, batch, None, 0, lam_vecs, l1_sub_norm, lam_init)
    x2 = _out_layer(ax, x2, seq, mod1, _x_mod_row, w_out)
    c2 = _out_layer(ac, c2, ctx_len, mod1, ctx_mod_row, w_out)

    w_in, w_out = bf(l2_w_in), bf(l2_w_out)
    kv_cols = (l2_w_in.shape[1] - 2 * di) // 2
    proj = functools.partial(_proj_layer, norm_g=l2_norm, w_in=w_in, q_norm=l2_q_norm, k_norm=l2_k_norm,
                             q_cols=di, k_cols=kv_cols, q_scale=l2_q_norm.shape[0] ** -0.5 * math.log2(math.e))
    qx = proj(x2, seq, mod2, _x_mod_row, rope=True)
    qc = proj(c2, ctx_len, mod2, ctx_mod_row, rope=False)
    ax = _win_attention(qx, seq, batch, qc, ctx_len, di, kv_cols // LANES, l2_sink)
    x2 = _out_layer(ax, x2, seq, mod2, _x_mod_row, w_out)

    x2 = _conv_layer(x2, seq, mod3, _x_mod_row, l3_norm, bf(l3_w_in), l3_conv_w, l3_conv_b, bf(l3_w_out))
    return x2.reshape(batch, seq, d)
```

```python
import functools
import math

import jax
import jax.numpy as jnp
from jax import lax
from jax.experimental import pallas as pl
from jax.experimental.pallas import tpu as pltpu

LANES = 128
MXU_COLS = 256
BF16_ROWS = 16
GRID_W = 64
ROPE_BASE = 10000.0
EPS = 1e-6
NEG_INF = -1e30
ATTN_BLOCK = 128
MOD_ROWS = 8
VMEM_LIMIT = 56 * 1024 * 1024

F32 = jnp.float32
BF16 = jnp.bfloat16


def _params(*sem, flags=None):
    return pltpu.CompilerParams(dimension_semantics=sem, vmem_limit_bytes=VMEM_LIMIT, flags=flags)


def _tile(n, target):
    if n <= target:
        return n
    t = target - target % LANES
    while n % t:
        t -= LANES
    return t


def _dot(a, b):
    return jnp.dot(a, b, preferred_element_type=F32)


def _dot_nt(a, b):
    return lax.dot_general(a, b, (((1,), (1,)), ((), ())), preferred_element_type=F32)


def _norm_mod(t, g, shift, scale):
    ms = jnp.mean(t * t, axis=-1, keepdims=True)
    y = t * lax.rsqrt(ms + EPS) * g
    return (y * (1.0 + scale) + shift).astype(BF16)


def _mod_kernel(c_ref, w_ref, b_ref, o_ref):
    a = jax.nn.silu(c_ref[...]).astype(BF16)
    o_ref[...] = _dot(a, w_ref[...].astype(BF16)) + b_ref[...]


def _modulation(cc, w_mod, b_mod):
    d, n = w_mod.shape
    tn = _tile(n, 768)
    out = pl.pallas_call(
        _mod_kernel,
        grid=(n // tn,),
        in_specs=[
            pl.BlockSpec((MOD_ROWS, d), lambda j: (0, 0)),
            pl.BlockSpec((d, tn), lambda j: (0, j)),
            pl.BlockSpec((1, tn), lambda j: (0, j)),
        ],
        out_specs=pl.BlockSpec((MOD_ROWS, tn), lambda j: (0, j)),
        out_shape=jax.ShapeDtypeStruct((MOD_ROWS, n), F32),
        compiler_params=_params("arbitrary"),
        name="modulation",
    )(cc, w_mod, b_mod.reshape(1, n))
    return out.reshape(MOD_ROWS, 1, n)


def _conv_kernel(xp_ref, x_ref, xn_ref, mod_ref, g_ref, wb_ref, wc_ref, wx_ref, wz_ref, cw_ref, cb_ref,
                 wo_ref, o_ref, h_scr, *, tm, d, tiles_per_seq):
    i = pl.program_id(0)
    j = pl.program_id(1)
    halo = BF16_ROWS
    mod = mod_ref[0]
    shift, scale, gate = mod[:, :d], mod[:, d:2 * d], mod[:, 2 * d:]

    @pl.when(j == 0)
    def _():
        g = g_ref[...]
        h_scr[0:halo, :] = _norm_mod(xp_ref[...], g, shift, scale)
        h_scr[halo:halo + tm, :] = _norm_mod(x_ref[...], g, shift, scale)
        h_scr[halo + tm:, :] = _norm_mod(xn_ref[...], g, shift, scale)
        o_ref[...] = jnp.zeros_like(o_ref)

    n = tm + 2 * halo
    h = h_scr[...]
    hm = h_scr[halo:halo + tm, :]
    u = _dot(h, wc_ref[...]) * _dot(h, wx_ref[...])
    pos = i % tiles_per_seq
    rows = lax.broadcasted_iota(jnp.int32, (n, 1), 0)
    keep = ((rows >= halo) | (pos != 0)) & ((rows < halo + tm) | (pos != tiles_per_seq - 1))
    u = jnp.where(keep, u, 0.0)
    u_prev = pltpu.roll(u, 1, 0)[halo:halo + tm]
    u_next = pltpu.roll(u, n - 1, 0)[halo:halo + tm]
    cw = cw_ref[...]
    y = u_prev * cw[0:1] + u[halo:halo + tm] * cw[1:2] + u_next * cw[2:3] + cb_ref[...]
    gated = _dot(hm, wb_ref[...]) * y * jax.nn.silu(_dot(hm, wz_ref[...]))
    o_ref[...] += _dot(gated.astype(BF16), wo_ref[...])

    @pl.when(j == pl.num_programs(1) - 1)
    def _():
        o_ref[...] = x_ref[...] + gate * o_ref[...]


def _conv_layer(x2, seq, mod, mod_row, norm_g, w_in, conv_w, conv_b, w_out):
    m, d = x2.shape
    di = w_out.shape[0]
    tm = min(seq, 512)
    tn = _tile(di, 256)
    nj = di // tn
    tiles_per_seq = seq // tm
    hb = tm // BF16_ROWS
    last_hb = m // BF16_ROWS - 1
    w_spec = lambda c: pl.BlockSpec((d, tn), lambda i, j: (0, c * nj + j))
    return pl.pallas_call(
        functools.partial(_conv_kernel, tm=tm, d=d, tiles_per_seq=tiles_per_seq),
        grid=(m // tm, nj),
        in_specs=[
            pl.BlockSpec((BF16_ROWS, d), lambda i, j: (jnp.maximum(i * hb - 1, 0), 0)),
            pl.BlockSpec((tm, d), lambda i, j: (i, 0)),
            pl.BlockSpec((BF16_ROWS, d), lambda i, j: (jnp.minimum((i + 1) * hb, last_hb), 0)),
            pl.BlockSpec((1, 1, 3 * d), lambda i, j: (mod_row(i, tiles_per_seq), 0, 0)),
            pl.BlockSpec((1, d), lambda i, j: (0, 0)),
            w_spec(0), w_spec(1), w_spec(2), w_spec(3),
            pl.BlockSpec((3, tn), lambda i, j: (0, j)),
            pl.BlockSpec((1, tn), lambda i, j: (0, j)),
            pl.BlockSpec((tn, d), lambda i, j: (j, 0)),
        ],
        out_specs=pl.BlockSpec((tm, d), lambda i, j: (i, 0)),
        out_shape=jax.ShapeDtypeStruct((m, d), F32),
        scratch_shapes=[pltpu.VMEM((tm + 2 * BF16_ROWS, d), BF16)],
        compiler_params=_params("parallel", "arbitrary"),
        name="conv_layer",
    )(x2, x2, x2, mod, norm_g.reshape(1, d), w_in, w_in, w_in, w_in, conv_w, conv_b.reshape(1, di), w_out)


def _head_norm_rope(acc, gn, cos, sin, group, out_scale):
    outs = []
    swap = group // 4
    for c in range(acc.shape[1] // LANES):
        t = acc[:, c * LANES:(c + 1) * LANES]
        lane = lax.broadcasted_iota(jnp.int32, t.shape, 1)
        sq = t * t
        if group == LANES:
            ms = jnp.mean(sq, axis=-1, keepdims=True)
        else:
            lo = lane < group
            s_lo = jnp.sum(jnp.where(lo, sq, 0.0), axis=-1, keepdims=True)
            s_hi = jnp.sum(jnp.where(lo, 0.0, sq), axis=-1, keepdims=True)
            ms = jnp.where(lo, s_lo, s_hi) * (1.0 / group)
        y = t * lax.rsqrt(ms + EPS) * gn
        if cos is not None:
            ahead = pltpu.roll(y, LANES - swap, 1)
            behind = pltpu.roll(y, swap, 1)
            y = y * cos + jnp.where((lane & swap) == 0, ahead, behind) * sin
        if out_scale != 1.0:
            y = y * out_scale
        outs.append(y)
    return outs[0] if len(outs) == 1 else jnp.concatenate(outs, axis=1)


def _proj_kernel(*refs, d, q_tiles, k_tiles, group, rope, q_scale):
    if rope:
        x_ref, mod_ref, g_ref, w_ref, qn_ref, kn_ref, cos_ref, sin_ref, o_ref, h_scr = refs
    else:
        x_ref, mod_ref, g_ref, w_ref, qn_ref, kn_ref, o_ref, h_scr = refs
    j = pl.program_id(1)

    @pl.when(j == 0)
    def _():
        mod = mod_ref[0]
        h_scr[...] = _norm_mod(x_ref[...], g_ref[...], mod[:, :d], mod[:, d:2 * d])

    cos = cos_ref[...] if rope else None
    sin = sin_ref[...] if rope else None
    tn = o_ref.shape[1]
    cw = min(tn, MXU_COLS)

    def columns(epilogue):
        h = h_scr[...]
        for c in range(tn // cw):
            cols = slice(c * cw, (c + 1) * cw)
            o_ref[:, cols] = epilogue(_dot(h, w_ref[:, cols])).astype(BF16)

    @pl.when(j < q_tiles)
    def _():
        columns(lambda acc: _head_norm_rope(acc, qn_ref[...], cos, sin, group, q_scale))

    @pl.when((j >= q_tiles) & (j < q_tiles + k_tiles))
    def _():
        columns(lambda acc: _head_norm_rope(acc, kn_ref[...], cos, sin, group, 1.0))

    @pl.when(j >= q_tiles + k_tiles)
    def _():
        columns(lambda acc: acc)


def _rope_tables(seq, group):
    rows = seq // GRID_W
    row = jnp.repeat(jnp.arange(rows), GRID_W).astype(F32)
    col = jnp.tile(jnp.arange(GRID_W), rows).astype(F32)
    n_freq = group // 4
    inv_freq = ROPE_BASE ** (-(jnp.arange(n_freq, dtype=F32) / n_freq))
    ar, ac = row[:, None] * inv_freq, col[:, None] * inv_freq
    cos = jnp.concatenate([jnp.cos(ar), jnp.cos(ar), jnp.cos(ac), jnp.cos(ac)], axis=-1)
    sin = jnp.concatenate([-jnp.sin(ar), jnp.sin(ar), -jnp.sin(ac), jnp.sin(ac)], axis=-1)
    reps = LANES // group
    return jnp.tile(cos, (1, reps)), jnp.tile(sin, (1, reps))


def _proj_layer(x2, seq, mod, mod_row, norm_g, w_in, q_norm, k_norm, q_cols, k_cols, rope, q_scale):
    m, d = x2.shape
    n = w_in.shape[1]
    group = q_norm.shape[0]
    tm = min(seq, 512)
    tn = math.gcd(math.gcd(q_cols, k_cols), 1024)
    tiles_per_seq = seq // tm
    reps = LANES // group
    args = [x2, mod, norm_g.reshape(1, d), w_in,
            jnp.tile(q_norm, reps).reshape(1, LANES), jnp.tile(k_norm, reps).reshape(1, LANES)]
    in_specs = [
        pl.BlockSpec((tm, d), lambda i, j: (i, 0)),
        pl.BlockSpec((1, 1, 3 * d), lambda i, j: (mod_row(i, tiles_per_seq), 0, 0)),
        pl.BlockSpec((1, d), lambda i, j: (0, 0)),
        pl.BlockSpec((d, tn), lambda i, j: (0, j)),
        pl.BlockSpec((1, LANES), lambda i, j: (0, 0)),
        pl.BlockSpec((1, LANES), lambda i, j: (0, 0)),
    ]
    if rope:
        args += list(_rope_tables(seq, group))
        in_specs += [pl.BlockSpec((tm, LANES), lambda i, j: (i % tiles_per_seq, 0))] * 2
    return pl.pallas_call(
        functools.partial(_proj_kernel, d=d, q_tiles=q_cols // tn, k_tiles=k_cols // tn, group=group,
                          rope=rope, q_scale=q_scale),
        grid=(m // tm, n // tn),
        in_specs=in_specs,
        out_specs=pl.BlockSpec((tm, tn), lambda i, j: (i, j)),
        out_shape=jax.ShapeDtypeStruct((m, n), BF16),
        scratch_shapes=[pltpu.VMEM((tm, d), BF16)],
        compiler_params=_params("parallel", "arbitrary"),
        name="proj_layer",
    )(*args)


def _out_kernel(a_ref, w_ref, x_ref, gate_ref, o_ref):
    o_ref[...] = x_ref[...] + gate_ref[0] * _dot(a_ref[...], w_ref[...])


def _out_layer(a2, x2, seq, mod, mod_row, w_out):
    m, d = x2.shape
    di = a2.shape[1]
    tm = min(seq, 512)
    tn = _tile(d, 512)
    tiles_per_seq = seq // tm
    gate_block0 = 2 * d // tn
    return pl.pallas_call(
        _out_kernel,
        grid=(m // tm, d // tn),
        in_specs=[
            pl.BlockSpec((tm, di), lambda i, j: (i, 0)),
            pl.BlockSpec((di, tn), lambda i, j: (0, j)),
            pl.BlockSpec((tm, tn), lambda i, j: (i, j)),
            pl.BlockSpec((1, 1, tn), lambda i, j: (mod_row(i, tiles_per_seq), 0, gate_block0 + j)),
        ],
        out_specs=pl.BlockSpec((tm, tn), lambda i, j: (i, j)),
        out_shape=jax.ShapeDtypeStruct((m, d), F32),
        compiler_params=_params("parallel", "arbitrary"),
        name="out_layer",
    )(a2, w_out, x2, mod)


def _lane_chunks(t):
    return [t[:, c * LANES:(c + 1) * LANES] for c in range(t.shape[1] // LANES)]


def _diff_attn_kernel(*refs, tq, tk, n_chunks, has_ctx, tiles, lam_init):
    pipelined = tiles > 1
    if pipelined:
        lam_ref, q_ref, qn_ref = refs[:3]
        rest = refs[3:]
    else:
        lam_ref, q_ref = refs[:2]
        qn_ref = None
        rest = refs[2:]
    if has_ctx:
        k_ref, v_ref, kc_ref, vc_ref, z_ref, sn_ref, o_ref = rest[:7]
        bufs = rest[7:]
        sets = [tuple(bufs[3 * p:3 * p + 3]) for p in range(len(bufs) // 3)]
    else:
        k_ref, v_ref, z_ref, sn_ref, o_ref = rest[:5]
        kc_ref = vc_ref = None
        bufs = rest[5:]
        sets = [(bufs[2 * p], None, bufs[2 * p + 1]) for p in range(len(bufs) // 2)]
    half = LANES // 2

    def tile_rows(t):
        return pl.ds(pl.multiple_of(t * tq, tq), tq)

    def scores(q, buf, t):
        s_buf, c_buf, m_buf = buf
        lane = lax.broadcasted_iota(jnp.int32, q.shape, 1)
        zero = jnp.zeros_like(q)
        q2 = jnp.concatenate([jnp.where(lane < half, q, zero), jnp.where(lane < half, zero, q)], axis=0)
        mx = jnp.full((2 * tq, LANES), NEG_INF, F32)
        for c in range(n_chunks):
            s = _dot_nt(q2, k_ref[c * tk:(c + 1) * tk, :])
            s_buf[t, c] = s
            for u in _lane_chunks(s):
                mx = jnp.maximum(mx, u)
        if has_ctx:
            s = _dot_nt(q2, kc_ref[...])
            c_buf[t] = s
            for u in _lane_chunks(s):
                mx = jnp.maximum(mx, u)
        m_buf[t] = jnp.broadcast_to(jnp.max(mx, axis=-1, keepdims=True), (2 * tq, LANES))

    def weighted(buf, t):
        s_buf, c_buf, m_buf = buf
        m = m_buf[t]
        lsum = jnp.zeros((2 * tq, LANES), F32)
        acc = jnp.zeros((2 * tq, LANES), F32)
        blocks = [(s_buf[t, c], v_ref[c * tk:(c + 1) * tk, :]) for c in range(n_chunks)]
        if has_ctx:
            blocks.append((c_buf[t], vc_ref[...]))
        for s, v in blocks:
            ps = [jnp.exp2(u - m) for u in _lane_chunks(s)]
            for u in ps:
                lsum = lsum + u
            acc = acc + _dot(jnp.concatenate(ps, axis=1).astype(BF16), v)
        return acc / jnp.sum(lsum, axis=-1, keepdims=True)

    lv = lam_ref[...]
    lam = (jnp.exp(jnp.sum(lv[0:1] * lv[1:2], axis=-1, keepdims=True))
           - jnp.exp(jnp.sum(lv[2:3] * lv[3:4], axis=-1, keepdims=True)) + lam_init)

    def finish(o_all, z):
        o = o_all[:tq] - lam * o_all[tq:]
        ms = jnp.mean(o * o, axis=-1, keepdims=True)
        y = o * lax.rsqrt(ms + EPS) * sn_ref[...] * (1.0 - lam_init)
        return (y * jax.nn.silu(z.astype(F32))).astype(BF16)

    if not pipelined:
        scores(q_ref[...], sets[0], 0)
        o_ref[...] = finish(weighted(sets[0], 0), z_ref[...])
        return

    i = pl.program_id(2)

    @pl.when(i == 0)
    def _():
        @pl.loop(0, tiles)
        def _(t):
            scores(q_ref[tile_rows(t), :], sets[0], t)

    def step(cur, nxt):
        @pl.loop(0, tiles)
        def _(t):
            rows = tile_rows(t)
            scores(qn_ref[rows, :], nxt, t)
            o_ref[rows, :] = finish(weighted(cur, t), z_ref[rows, :])

    @pl.when(i % 2 == 0)
    def _():
        step(sets[0], sets[1])

    @pl.when(i % 2 == 1)
    def _():
        step(sets[1], sets[0])


def _diff_attention(qkvz, seq, batch, ctx_qkvz, ctx_len, lam_vecs, sub_norm, lam_init):
    di = qkvz.shape[1] // 4
    heads = di // LANES
    tq = min(seq, 256)
    tk = min(seq, 512)
    tiles = 2 if seq >= 2 * tq else 1
    rows = tiles * tq
    nq = seq // rows
    has_ctx = ctx_qkvz is not None
    args = [lam_vecs, qkvz]
    in_specs = [
        pl.BlockSpec(lam_vecs.shape, lambda b, h, i: (0, 0)),
        pl.BlockSpec((rows, LANES), lambda b, h, i: (b * nq + i, h)),
    ]
    if tiles == 2:
        args += [qkvz]
        in_specs += [pl.BlockSpec((rows, LANES), lambda b, h, i: (b * nq + jnp.minimum(i + 1, nq - 1), h))]
    args += [qkvz, qkvz]
    in_specs += [
        pl.BlockSpec((seq, LANES), lambda b, h, i: (b, heads + h)),
        pl.BlockSpec((seq, LANES), lambda b, h, i: (b, 2 * heads + h)),
    ]
    if has_ctx:
        args += [ctx_qkvz, ctx_qkvz]
        in_specs += [
            pl.BlockSpec((ctx_len, LANES), lambda b, h, i: (b, heads + h)),
            pl.BlockSpec((ctx_len, LANES), lambda b, h, i: (b, 2 * heads + h)),
        ]
    args += [qkvz, sub_norm.reshape(1, LANES)]
    in_specs += [
        pl.BlockSpec((rows, LANES), lambda b, h, i: (b * nq + i, 3 * heads + h)),
        pl.BlockSpec((1, LANES), lambda b, h, i: (0, 0)),
    ]
    scratch = []
    for _ in range(2 if tiles > 1 else 1):
        scratch += [pltpu.VMEM((tiles, seq // tk, 2 * tq, tk), F32)]
        if has_ctx:
            scratch += [pltpu.VMEM((tiles, 2 * tq, ctx_len), F32)]
        scratch += [pltpu.VMEM((tiles, 2 * tq, LANES), F32)]
    return pl.pallas_call(
        functools.partial(_diff_attn_kernel, tq=tq, tk=tk, n_chunks=seq // tk, has_ctx=has_ctx, tiles=tiles,
                          lam_init=lam_init),
        grid=(batch, heads, nq),
        in_specs=in_specs,
        out_specs=pl.BlockSpec((rows, LANES), lambda b, h, i: (b * nq + i, h)),
        out_shape=jax.ShapeDtypeStruct((batch * seq, di), BF16),
        scratch_shapes=scratch,
        compiler_params=_params("parallel", "parallel", "arbitrary"),
        name="diff_attention",
    )(*args)


def _win_attn_kernel(*refs, group, kv_heads, n_z):
    q_ref, kp_ref, kc_ref, kn_ref, vp_ref, vc_ref, vn_ref, kx_ref, vx_ref, sink_ref = refs[:10]
    z_refs = refs[10:10 + n_z]
    o_ref = refs[10 + n_z]
    i = pl.program_id(1)
    blk = ATTN_BLOCK
    rows = group * blk
    zw = z_refs[0].shape[1]
    qpos = lax.broadcasted_iota(jnp.int32, (rows, blk), 0) % blk
    kpos = lax.broadcasted_iota(jnp.int32, (rows, blk), 1)
    keep_prev = (kpos >= qpos) & (i > 0)
    keep_next = (kpos <= qpos) & (i < pl.num_programs(1) - 1)
    sink2 = sink_ref[...] * math.log2(math.e)
    for n in range(kv_heads):
        ks = slice(n * LANES, (n + 1) * LANES)
        q4 = jnp.concatenate([q_ref[:, (n * group + g) * LANES:(n * group + g + 1) * LANES] for g in range(group)],
                             axis=0)
        k_all = jnp.concatenate([kp_ref[:, ks], kc_ref[:, ks], kn_ref[:, ks], kx_ref[:, ks]], axis=0)
        v_all = jnp.concatenate([vp_ref[:, ks], vc_ref[:, ks], vn_ref[:, ks], vx_ref[:, ks]], axis=0)
        s = _lane_chunks(_dot_nt(q4, k_all))
        s[0] = jnp.where(keep_prev, s[0], NEG_INF)
        s[2] = jnp.where(keep_next, s[2], NEG_INF)
        sink_b = jnp.concatenate([jnp.broadcast_to(sink2[n, g:g + 1, :], (blk, LANES)) for g in range(group)], axis=0)
        mx = sink_b
        for t in s:
            mx = jnp.maximum(mx, t)
        m = jnp.broadcast_to(jnp.max(mx, axis=-1, keepdims=True), (rows, LANES))
        ps = [jnp.exp2(t - m) for t in s]
        lsum = ps[0]
        for t in ps[1:]:
            lsum = lsum + t
        l = jnp.sum(lsum, axis=-1, keepdims=True) + jnp.exp2(sink_b - m)[:, 0:1]
        o = _dot(jnp.concatenate(ps, axis=1).astype(BF16), v_all) / l
        for g in range(group):
            col = (n * group + g) * LANES
            z = z_refs[col // zw][:, col % zw:col % zw + LANES].astype(F32)
            o_ref[:, col:col + LANES] = (o[g * blk:(g + 1) * blk] * jax.nn.silu(z)).astype(BF16)


def _win_attention(qkvz, seq, batch, ctx_qkvz, ctx_len, di, kv_heads, sink):
    heads = di // LANES
    group = heads // kv_heads
    kvw = kv_heads * LANES
    blk = ATTN_BLOCK
    nb = seq // blk
    z_col0 = di + 2 * kvw
    zw = math.gcd(z_col0, di)
    n_z = di // zw
    assert di % kvw == 0
    sink_rep = jnp.broadcast_to(sink.astype(F32).reshape(kv_heads, group, 1), (kv_heads, group, LANES))
    band = lambda col_block, shift: pl.BlockSpec(
        (blk, kvw), lambda b, i: (b * nb + jnp.clip(i + shift, 0, nb - 1), col_block))
    k_blk, v_blk = di // kvw, di // kvw + 1
    z_spec = lambda t: pl.BlockSpec((blk, zw), lambda b, i: (b * nb + i, z_col0 // zw + t))
    return pl.pallas_call(
        functools.partial(_win_attn_kernel, group=group, kv_heads=kv_heads, n_z=n_z),
        grid=(batch, nb),
        in_specs=[
            pl.BlockSpec((blk, di), lambda b, i: (b * nb + i, 0)),
            band(k_blk, -1), band(k_blk, 0), band(k_blk, 1),
            band(v_blk, -1), band(v_blk, 0), band(v_blk, 1),
            pl.BlockSpec((ctx_len, kvw), lambda b, i: (b, k_blk)),
            pl.BlockSpec((ctx_len, kvw), lambda b, i: (b, v_blk)),
            pl.BlockSpec((kv_heads, group, LANES), lambda b, i: (0, 0, 0)),
        ] + [z_spec(t) for t in range(n_z)],
        out_specs=pl.BlockSpec((blk, di), lambda b, i: (b * nb + i, 0)),
        out_shape=jax.ShapeDtypeStruct((batch * seq, di), BF16),
        compiler_params=_params("parallel", "arbitrary"),
        name="win_attention",
    )(*([qkvz] * 7 + [ctx_qkvz, ctx_qkvz, sink_rep] + [qkvz] * n_z))


def _x_mod_row(i, tiles_per_seq):
    return i // tiles_per_seq


def kernel(x, c, ctx, c_ctx, l0_norm, l0_w_mod, l0_b_mod, l0_w_in, l0_conv_w, l0_conv_b, l0_w_out, l1_norm, l1_w_mod, l1_b_mod, l1_w_in, l1_q_norm, l1_k_norm, l1_lam_q1, l1_lam_k1, l1_lam_q2, l1_lam_k2, l1_sub_norm, l1_w_out, l2_norm, l2_w_mod, l2_b_mod, l2_w_in, l2_q_norm, l2_k_norm, l2_sink, l2_w_out, l3_norm, l3_w_mod, l3_b_mod, l3_w_in, l3_conv_w, l3_conv_b, l3_w_out):
    batch, seq, d = x.shape
    ctx_len = ctx.shape[1]
    di = l0_w_out.shape[0]
    assert batch < MOD_ROWS
    ctx_mod_row = lambda i, tiles_per_seq: batch

    x2 = x.reshape(batch * seq, d)
    c2 = ctx.reshape(batch * ctx_len, d)
    cc = jnp.zeros((MOD_ROWS, d), F32).at[:batch].set(c).at[batch].set(c_ctx)
    mod0, mod1, mod2, mod3 = (_modulation(cc, w, b) for w, b in
                              ((l0_w_mod, l0_b_mod), (l1_w_mod, l1_b_mod), (l2_w_mod, l2_b_mod), (l3_w_mod, l3_b_mod)))
    bf = lambda w: w.astype(BF16)

    w_in, w_out = bf(l0_w_in), bf(l0_w_out)
    x2n = _conv_layer(x2, seq, mod0, _x_mod_row, l0_norm, w_in, l0_conv_w, l0_conv_b, w_out)
    c2 = _conv_layer(c2, ctx_len, mod0, ctx_mod_row, l0_norm, w_in, l0_conv_w, l0_conv_b, w_out)
    x2 = x2n

    w_in, w_out = bf(l1_w_in), bf(l1_w_out)
    lam_init = 0.8 - 0.6 * math.exp(-0.3 * 1)
    q_scale = l1_q_norm.shape[0] ** -0.5 * math.log2(math.e)
    proj = functools.partial(_proj_layer, norm_g=l1_norm, w_in=w_in, q_norm=l1_q_norm, k_norm=l1_k_norm,
                             q_cols=di, k_cols=di, q_scale=q_scale)
    qx = proj(x2, seq, mod1, _x_mod_row, rope=True)
    qc = proj(c2, ctx_len, mod1, ctx_mod_row, rope=False)
    lam_vecs = jnp.stack([l1_lam_q1, l1_lam_k1, l1_lam_q2, l1_lam_k2]).astype(F32)
    ax = _diff_attention(qx, seq, batch, qc, ctx_len, lam_vecs, l1_sub_norm, lam_init)
    ac = _diff_attention(qc, ctx_len, batch, None, 0, lam_vecs, l1_sub_norm, lam_init)
    x2 = _out_layer(ax, x2, seq, mod1, _x_mod_row, w_out)
    c2 = _out_layer(ac, c2, ctx_len, mod1, ctx_mod_row, w_out)

    w_in, w_out = bf(l2_w_in), bf(l2_w_out)
    kv_cols = (l2_w_in.shape[1] - 2 * di) // 2
    proj = functools.partial(_proj_layer, norm_g=l2_norm, w_in=w_in, q_norm=l2_q_norm, k_norm=l2_k_norm,
                             q_cols=di, k_cols=kv_cols, q_scale=l2_q_norm.shape[0] ** -0.5 * math.log2(math.e))
    qx = proj(x2, seq, mod2, _x_mod_row, rope=True)
    qc = proj(c2, ctx_len, mod2, ctx_mod_row, rope=False)
    ax = _win_attention(qx, seq, batch, qc, ctx_len, di, kv_cols // LANES, l2_sink)
    x2 = _out_layer(ax, x2, seq, mod2, _x_mod_row, w_out)

    x2 = _conv_layer(x2, seq, mod3, _x_mod_row, l3_norm, bf(l3_w_in), l3_conv_w, l3_conv_b, bf(l3_w_out))
    return x2.reshape(batch, seq, d)
```

```python
import functools
import math

import jax
import jax.numpy as jnp
from jax import lax
from jax.experimental import pallas as pl
from jax.experimental.pallas import tpu as pltpu

LANES = 128
SUBLANES = 8
MXU_COLS = 256
BF16_ROWS = 16
GRID_W = 64
ROPE_BASE = 10000.0
EPS = 1e-6
NEG_INF = -1e30
ATTN_BLOCK = 128
MOD_ROWS = 8
VMEM_LIMIT = 56 * 1024 * 1024

F32 = jnp.float32
BF16 = jnp.bfloat16


def _params(*sem, flags=None):
    return pltpu.CompilerParams(dimension_semantics=sem, vmem_limit_bytes=VMEM_LIMIT, flags=flags)


def _tile(n, target):
    if n <= target:
        return n
    t = target - target % LANES
    while n % t:
        t -= LANES
    return t


def _dot(a, b):
    return jnp.dot(a, b, preferred_element_type=F32)


def _dot_nt(a, b):
    return lax.dot_general(a, b, (((1,), (1,)), ((), ())), preferred_element_type=F32)


def _norm_mod(t, g, shift, scale):
    ms = jnp.mean(t * t, axis=-1, keepdims=True)
    y = t * lax.rsqrt(ms + EPS) * g
    return (y * (1.0 + scale) + shift).astype(BF16)


def _mod_kernel(c_ref, w_ref, b_ref, o_ref):
    a = jax.nn.silu(c_ref[...]).astype(BF16)
    o_ref[...] = _dot(a, w_ref[...].astype(BF16)) + b_ref[...]


def _modulation(cc, w_mod, b_mod):
    d, n = w_mod.shape
    tn = _tile(n, 768)
    out = pl.pallas_call(
        _mod_kernel,
        grid=(n // tn,),
        in_specs=[
            pl.BlockSpec((MOD_ROWS, d), lambda j: (0, 0)),
            pl.BlockSpec((d, tn), lambda j: (0, j)),
            pl.BlockSpec((1, tn), lambda j: (0, j)),
        ],
        out_specs=pl.BlockSpec((MOD_ROWS, tn), lambda j: (0, j)),
        out_shape=jax.ShapeDtypeStruct((MOD_ROWS, n), F32),
        compiler_params=_params("arbitrary"),
        name="modulation",
    )(cc, w_mod, b_mod.reshape(1, n))
    return out.reshape(MOD_ROWS, 1, n)


def _conv_kernel(xp_ref, x_ref, xn_ref, mod_ref, g_ref, wb_ref, wc_ref, wx_ref, wz_ref, cw_ref, cb_ref,
                 wo_ref, o_ref, h_scr, *, tm, d, tiles_per_seq):
    i = pl.program_id(0)
    j = pl.program_id(1)
    halo = BF16_ROWS
    mod = mod_ref[0]
    shift, scale, gate = mod[:, :d], mod[:, d:2 * d], mod[:, 2 * d:]

    @pl.when(j == 0)
    def _():
        g = g_ref[...]
        h_scr[0:halo, :] = _norm_mod(xp_ref[...], g, shift, scale)
        h_scr[halo:halo + tm, :] = _norm_mod(x_ref[...], g, shift, scale)
        h_scr[halo + tm:, :] = _norm_mod(xn_ref[...], g, shift, scale)
        o_ref[...] = jnp.zeros_like(o_ref)

    n = tm + 2 * halo
    h = h_scr[...]
    hm = h_scr[halo:halo + tm, :]
    u = _dot(h, wc_ref[...]) * _dot(h, wx_ref[...])
    pos = i % tiles_per_seq
    rows = lax.broadcasted_iota(jnp.int32, (n, 1), 0)
    keep = ((rows >= halo) | (pos != 0)) & ((rows < halo + tm) | (pos != tiles_per_seq - 1))
    u = jnp.where(keep, u, 0.0)
    u_prev = pltpu.roll(u, 1, 0)[halo:halo + tm]
    u_next = pltpu.roll(u, n - 1, 0)[halo:halo + tm]
    cw = cw_ref[...]
    y = u_prev * cw[0:1] + u[halo:halo + tm] * cw[1:2] + u_next * cw[2:3] + cb_ref[...]
    gated = _dot(hm, wb_ref[...]) * y * jax.nn.silu(_dot(hm, wz_ref[...]))
    o_ref[...] += _dot(gated.astype(BF16), wo_ref[...])

    @pl.when(j == pl.num_programs(1) - 1)
    def _():
        o_ref[...] = x_ref[...] + gate * o_ref[...]


def _conv_layer(x2, seq, mod, mod_row, norm_g, w_in, conv_w, conv_b, w_out):
    m, d = x2.shape
    di = w_out.shape[0]
    tm = min(seq, 512)
    tn = _tile(di, 256)
    nj = di // tn
    tiles_per_seq = seq // tm
    hb = tm // BF16_ROWS
    last_hb = m // BF16_ROWS - 1
    w_spec = lambda c: pl.BlockSpec((d, tn), lambda i, j: (0, c * nj + j))
    return pl.pallas_call(
        functools.partial(_conv_kernel, tm=tm, d=d, tiles_per_seq=tiles_per_seq),
        grid=(m // tm, nj),
        in_specs=[
            pl.BlockSpec((BF16_ROWS, d), lambda i, j: (jnp.maximum(i * hb - 1, 0), 0)),
            pl.BlockSpec((tm, d), lambda i, j: (i, 0)),
            pl.BlockSpec((BF16_ROWS, d), lambda i, j: (jnp.minimum((i + 1) * hb, last_hb), 0)),
            pl.BlockSpec((1, 1, 3 * d), lambda i, j: (mod_row(i, tiles_per_seq), 0, 0)),
            pl.BlockSpec((1, d), lambda i, j: (0, 0)),
            w_spec(0), w_spec(1), w_spec(2), w_spec(3),
            pl.BlockSpec((3, tn), lambda i, j: (0, j)),
            pl.BlockSpec((1, tn), lambda i, j: (0, j)),
            pl.BlockSpec((tn, d), lambda i, j: (j, 0)),
        ],
        out_specs=pl.BlockSpec((tm, d), lambda i, j: (i, 0)),
        out_shape=jax.ShapeDtypeStruct((m, d), F32),
        scratch_shapes=[pltpu.VMEM((tm + 2 * BF16_ROWS, d), BF16)],
        compiler_params=_params("parallel", "arbitrary"),
        name="conv_layer",
    )(x2, x2, x2, mod, norm_g.reshape(1, d), w_in, w_in, w_in, w_in, conv_w, conv_b.reshape(1, di), w_out)


def _head_norm_rope(acc, gn, cos, sin, group, out_scale):
    outs = []
    swap = group // 4
    for c in range(acc.shape[1] // LANES):
        t = acc[:, c * LANES:(c + 1) * LANES]
        lane = lax.broadcasted_iota(jnp.int32, t.shape, 1)
        sq = t * t
        if group == LANES:
            ms = jnp.mean(sq, axis=-1, keepdims=True)
        else:
            lo = lane < group
            s_lo = jnp.sum(jnp.where(lo, sq, 0.0), axis=-1, keepdims=True)
            s_hi = jnp.sum(jnp.where(lo, 0.0, sq), axis=-1, keepdims=True)
            ms = jnp.where(lo, s_lo, s_hi) * (1.0 / group)
        y = t * lax.rsqrt(ms + EPS) * gn
        if cos is not None:
            ahead = pltpu.roll(y, LANES - swap, 1)
            behind = pltpu.roll(y, swap, 1)
            y = y * cos + jnp.where((lane & swap) == 0, ahead, behind) * sin
        if out_scale != 1.0:
            y = y * out_scale
        outs.append(y)
    return outs[0] if len(outs) == 1 else jnp.concatenate(outs, axis=1)


def _proj_kernel(*refs, d, q_tiles, k_tiles, group, rope, q_scale):
    if rope:
        x_ref, mod_ref, g_ref, w_ref, qn_ref, kn_ref, cos_ref, sin_ref, o_ref, h_scr = refs
    else:
        x_ref, mod_ref, g_ref, w_ref, qn_ref, kn_ref, o_ref, h_scr = refs
    j = pl.program_id(1)

    @pl.when(j == 0)
    def _():
        mod = mod_ref[0]
        h_scr[...] = _norm_mod(x_ref[...], g_ref[...], mod[:, :d], mod[:, d:2 * d])

    cos = cos_ref[...] if rope else None
    sin = sin_ref[...] if rope else None
    tn = o_ref.shape[1]
    cw = min(tn, MXU_COLS)

    def columns(epilogue):
        h = h_scr[...]
        for c in range(tn // cw):
            cols = slice(c * cw, (c + 1) * cw)
            o_ref[:, cols] = epilogue(_dot(h, w_ref[:, cols])).astype(BF16)

    @pl.when(j < q_tiles)
    def _():
        columns(lambda acc: _head_norm_rope(acc, qn_ref[...], cos, sin, group, q_scale))

    @pl.when((j >= q_tiles) & (j < q_tiles + k_tiles))
    def _():
        columns(lambda acc: _head_norm_rope(acc, kn_ref[...], cos, sin, group, 1.0))

    @pl.when(j >= q_tiles + k_tiles)
    def _():
        columns(lambda acc: acc)


def _rope_tables(seq, group):
    rows = seq // GRID_W
    row = jnp.repeat(jnp.arange(rows), GRID_W).astype(F32)
    col = jnp.tile(jnp.arange(GRID_W), rows).astype(F32)
    n_freq = group // 4
    inv_freq = ROPE_BASE ** (-(jnp.arange(n_freq, dtype=F32) / n_freq))
    ar, ac = row[:, None] * inv_freq, col[:, None] * inv_freq
    cos = jnp.concatenate([jnp.cos(ar), jnp.cos(ar), jnp.cos(ac), jnp.cos(ac)], axis=-1)
    sin = jnp.concatenate([-jnp.sin(ar), jnp.sin(ar), -jnp.sin(ac), jnp.sin(ac)], axis=-1)
    reps = LANES // group
    return jnp.tile(cos, (1, reps)), jnp.tile(sin, (1, reps))


def _proj_layer(x2, seq, mod, mod_row, norm_g, w_in, q_norm, k_norm, q_cols, k_cols, rope, q_scale):
    m, d = x2.shape
    n = w_in.shape[1]
    group = q_norm.shape[0]
    tm = min(seq, 512)
    tn = math.gcd(math.gcd(q_cols, k_cols), 1024)
    tiles_per_seq = seq // tm
    reps = LANES // group
    args = [x2, mod, norm_g.reshape(1, d), w_in,
            jnp.tile(q_norm, reps).reshape(1, LANES), jnp.tile(k_norm, reps).reshape(1, LANES)]
    in_specs = [
        pl.BlockSpec((tm, d), lambda i, j: (i, 0)),
        pl.BlockSpec((1, 1, 3 * d), lambda i, j: (mod_row(i, tiles_per_seq), 0, 0)),
        pl.BlockSpec((1, d), lambda i, j: (0, 0)),
        pl.BlockSpec((d, tn), lambda i, j: (0, j)),
        pl.BlockSpec((1, LANES), lambda i, j: (0, 0)),
        pl.BlockSpec((1, LANES), lambda i, j: (0, 0)),
    ]
    if rope:
        args += list(_rope_tables(seq, group))
        in_specs += [pl.BlockSpec((tm, LANES), lambda i, j: (i % tiles_per_seq, 0))] * 2
    return pl.pallas_call(
        functools.partial(_proj_kernel, d=d, q_tiles=q_cols // tn, k_tiles=k_cols // tn, group=group,
                          rope=rope, q_scale=q_scale),
        grid=(m // tm, n // tn),
        in_specs=in_specs,
        out_specs=pl.BlockSpec((tm, tn), lambda i, j: (i, j)),
        out_shape=jax.ShapeDtypeStruct((m, n), BF16),
        scratch_shapes=[pltpu.VMEM((tm, d), BF16)],
        compiler_params=_params("parallel", "arbitrary"),
        name="proj_layer",
    )(*args)


def _out_kernel(a_ref, w_ref, x_ref, gate_ref, o_ref):
    o_ref[...] = x_ref[...] + gate_ref[0] * _dot(a_ref[...], w_ref[...])


def _out_layer(a2, x2, seq, mod, mod_row, w_out):
    m, d = x2.shape
    di = a2.shape[1]
    tm = min(seq, 512)
    tn = _tile(d, 512)
    tiles_per_seq = seq // tm
    gate_block0 = 2 * d // tn
    return pl.pallas_call(
        _out_kernel,
        grid=(m // tm, d // tn),
        in_specs=[
            pl.BlockSpec((tm, di), lambda i, j: (i, 0)),
            pl.BlockSpec((di, tn), lambda i, j: (0, j)),
            pl.BlockSpec((tm, tn), lambda i, j: (i, j)),
            pl.BlockSpec((1, 1, tn), lambda i, j: (mod_row(i, tiles_per_seq), 0, gate_block0 + j)),
        ],
        out_specs=pl.BlockSpec((tm, tn), lambda i, j: (i, j)),
        out_shape=jax.ShapeDtypeStruct((m, d), F32),
        compiler_params=_params("parallel", "arbitrary"),
        name="out_layer",
    )(a2, w_out, x2, mod)


def _lane_chunks(t):
    return [t[:, c * LANES:(c + 1) * LANES] for c in range(t.shape[1] // LANES)]


def _diff_attn_kernel(*refs, tq, tk, n_chunks, has_ctx, tiles, lam_init):
    pipelined = tiles > 1
    if pipelined:
        lam_ref, q_ref, qn_ref = refs[:3]
        rest = refs[3:]
    else:
        lam_ref, q_ref = refs[:2]
        qn_ref = None
        rest = refs[2:]
    if has_ctx:
        k_ref, v_ref, kc_ref, vc_ref, z_ref, sn_ref, o_ref, vt_scr, vct_scr = rest[:9]
        bufs = rest[9:]
        sets = [tuple(bufs[3 * p:3 * p + 3]) for p in range(len(bufs) // 3)]
    else:
        k_ref, v_ref, z_ref, sn_ref, o_ref, vt_scr = rest[:6]
        kc_ref = vc_ref = vct_scr = None
        bufs = rest[6:]
        sets = [(bufs[2 * p], None, bufs[2 * p + 1]) for p in range(len(bufs) // 2)]
    half = LANES // 2

    def tile_rows(t):
        return pl.ds(pl.multiple_of(t * tq, tq), tq)

    def transpose_values():
        for c in range(n_chunks):
            vt_scr[:, c * tk:(c + 1) * tk] = v_ref[c * tk:(c + 1) * tk, :].astype(F32).T.astype(BF16)
        if has_ctx:
            vct_scr[...] = vc_ref[...].astype(F32).T.astype(BF16)

    def sublane_groups(s):
        return s.reshape(s.shape[0] // SUBLANES, SUBLANES, s.shape[1])

    def scores(q, buf, t):
        s_buf, c_buf, m_buf = buf
        lane = lax.broadcasted_iota(jnp.int32, q.shape, 1)
        zero = jnp.zeros_like(q)
        subs = (jnp.where(lane < half, q, zero), jnp.where(lane < half, zero, q))
        for h, qh in enumerate(subs):
            cols = slice(h * tq, (h + 1) * tq)
            mx = jnp.full((SUBLANES, tq), NEG_INF, F32)
            for c in range(n_chunks):
                s = _dot_nt(k_ref[c * tk:(c + 1) * tk, :], qh)
                s_buf[t, c, :, cols] = s
                mx = jnp.maximum(mx, jnp.max(sublane_groups(s), axis=0))
            if has_ctx:
                s = _dot_nt(kc_ref[...], qh)
                c_buf[t, :, cols] = s
                mx = jnp.maximum(mx, jnp.max(sublane_groups(s), axis=0))
            m_buf[t, :, cols] = jnp.broadcast_to(jnp.max(mx, axis=0, keepdims=True), (SUBLANES, tq))

    def weighted(buf, t):
        s_buf, c_buf, m_buf = buf
        m = m_buf[t]
        lsum = jnp.zeros((SUBLANES, 2 * tq), F32)
        acc = jnp.zeros((LANES, 2 * tq), F32)
        blocks = [(s_buf[t, c], vt_scr[:, c * tk:(c + 1) * tk]) for c in range(n_chunks)]
        if has_ctx:
            blocks.append((c_buf[t], vct_scr[...]))
        for s, vt in blocks:
            p = jnp.exp2(sublane_groups(s) - m[None])
            lsum = lsum + jnp.sum(p, axis=0)
            acc = acc + _dot(vt, p.reshape(s.shape).astype(BF16))
        return acc / jnp.sum(lsum, axis=0, keepdims=True)

    lv = lam_ref[...]
    lam = (jnp.exp(jnp.sum(lv[0:1] * lv[1:2], axis=-1, keepdims=True))
           - jnp.exp(jnp.sum(lv[2:3] * lv[3:4], axis=-1, keepdims=True)) + lam_init)

    def finish(o_all, z):
        o = (o_all[:, :tq] - lam * o_all[:, tq:]).T
        ms = jnp.mean(o * o, axis=-1, keepdims=True)
        y = o * lax.rsqrt(ms + EPS) * sn_ref[...] * (1.0 - lam_init)
        return (y * jax.nn.silu(z.astype(F32))).astype(BF16)

    if not pipelined:
        transpose_values()
        scores(q_ref[...], sets[0], 0)
        o_ref[...] = finish(weighted(sets[0], 0), z_ref[...])
        return

    i = pl.program_id(2)

    @pl.when(i == 0)
    def _():
        transpose_values()

        @pl.loop(0, tiles)
        def _(t):
            scores(q_ref[tile_rows(t), :], sets[0], t)

    def both(q, cur, nxt, t):
        s_nxt, c_nxt, m_nxt = nxt
        s_cur, c_cur, m_cur = cur
        lane = lax.broadcasted_iota(jnp.int32, q.shape, 1)
        zero = jnp.zeros_like(q)
        subs = (jnp.where(lane < half, q, zero), jnp.where(lane < half, zero, q))
        mxs = [jnp.full((SUBLANES, tq), NEG_INF, F32) for _ in subs]
        m = m_cur[t]
        lsum = jnp.zeros((SUBLANES, 2 * tq), F32)
        acc = jnp.zeros((LANES, 2 * tq), F32)
        tie = None
        for c in range(n_chunks + has_ctx):
            if c < n_chunks:
                k, vt, s_old = k_ref[c * tk:(c + 1) * tk, :], vt_scr[:, c * tk:(c + 1) * tk], s_cur[t, c]
            else:
                k, vt, s_old = kc_ref[...], vct_scr[...], c_cur[t]
            if tie is not None:
                k = k + jnp.tile(tie, (k.shape[0] // SUBLANES, 1)).astype(BF16)
            for h, qh in enumerate(subs):
                cols = slice(h * tq, (h + 1) * tq)
                s = _dot_nt(k, qh)
                if c < n_chunks:
                    s_nxt[t, c, :, cols] = s
                else:
                    c_nxt[t, :, cols] = s
                mxs[h] = jnp.maximum(mxs[h], jnp.max(sublane_groups(s), axis=0))
            p = jnp.exp2(sublane_groups(s_old) - m[None])
            lsum = lsum + jnp.sum(p, axis=0)
            acc = acc + _dot(vt, p.reshape(s_old.shape).astype(BF16))
            bits = pltpu.bitcast(p[-1][:, :LANES], jnp.uint32)
            tie = pltpu.bitcast((bits >> 16) >> 16, F32)
        for h, mx in enumerate(mxs):
            m_nxt[t, :, h * tq:(h + 1) * tq] = jnp.broadcast_to(jnp.max(mx, axis=0, keepdims=True), (SUBLANES, tq))
        return acc / jnp.sum(lsum, axis=0, keepdims=True)

    def step(cur, nxt):
        for t in range(tiles):
            rows = slice(t * tq, (t + 1) * tq)
            o_ref[rows, :] = finish(both(qn_ref[rows, :], cur, nxt, t), z_ref[rows, :])

    @pl.when(i % 2 == 0)
    def _():
        step(sets[0], sets[1])

    @pl.when(i % 2 == 1)
    def _():
        step(sets[1], sets[0])


def _diff_attention(qkvz, seq, batch, ctx_qkvz, ctx_len, lam_vecs, sub_norm, lam_init):
    di = qkvz.shape[1] // 4
    heads = di // LANES
    tq = min(seq, 256)
    tk = min(seq, 512)
    tiles = 2 if seq >= 2 * tq else 1
    rows = tiles * tq
    nq = seq // rows
    has_ctx = ctx_qkvz is not None
    args = [lam_vecs, qkvz]
    in_specs = [
        pl.BlockSpec(lam_vecs.shape, lambda b, h, i: (0, 0)),
        pl.BlockSpec((rows, LANES), lambda b, h, i: (b * nq + i, h)),
    ]
    if tiles > 1:
        args += [qkvz]
        in_specs += [pl.BlockSpec((rows, LANES), lambda b, h, i: (b * nq + jnp.minimum(i + 1, nq - 1), h))]
    args += [qkvz, qkvz]
    in_specs += [
        pl.BlockSpec((seq, LANES), lambda b, h, i: (b, heads + h)),
        pl.BlockSpec((seq, LANES), lambda b, h, i: (b, 2 * heads + h)),
    ]
    if has_ctx:
        args += [ctx_qkvz, ctx_qkvz]
        in_specs += [
            pl.BlockSpec((ctx_len, LANES), lambda b, h, i: (b, heads + h)),
            pl.BlockSpec((ctx_len, LANES), lambda b, h, i: (b, 2 * heads + h)),
        ]
    args += [qkvz, sub_norm.reshape(1, LANES)]
    in_specs += [
        pl.BlockSpec((rows, LANES), lambda b, h, i: (b * nq + i, 3 * heads + h)),
        pl.BlockSpec((1, LANES), lambda b, h, i: (0, 0)),
    ]
    scratch = [pltpu.VMEM((LANES, seq), BF16)] + ([pltpu.VMEM((LANES, ctx_len), BF16)] if has_ctx else [])
    for _ in range(2 if tiles > 1 else 1):
        scratch += [pltpu.VMEM((tiles, seq // tk, tk, 2 * tq), F32)]
        if has_ctx:
            scratch += [pltpu.VMEM((tiles, ctx_len, 2 * tq), F32)]
        scratch += [pltpu.VMEM((tiles, SUBLANES, 2 * tq), F32)]
    return pl.pallas_call(
        functools.partial(_diff_attn_kernel, tq=tq, tk=tk, n_chunks=seq // tk, has_ctx=has_ctx, tiles=tiles,
                          lam_init=lam_init),
        grid=(batch, heads, nq),
        in_specs=in_specs,
        out_specs=pl.BlockSpec((rows, LANES), lambda b, h, i: (b * nq + i, h)),
        out_shape=jax.ShapeDtypeStruct((batch * seq, di), BF16),
        scratch_shapes=scratch,
        compiler_params=_params("parallel", "parallel", "arbitrary"),
        name="diff_attention",
    )(*args)


def _win_attn_kernel(*refs, group, kv_heads, n_z):
    q_ref, kp_ref, kc_ref, kn_ref, vp_ref, vc_ref, vn_ref, kx_ref, vx_ref, sink_ref = refs[:10]
    z_refs = refs[10:10 + n_z]
    o_ref = refs[10 + n_z]
    i = pl.program_id(1)
    blk = ATTN_BLOCK
    rows = group * blk
    zw = z_refs[0].shape[1]
    qpos = lax.broadcasted_iota(jnp.int32, (rows, blk), 0) % blk
    kpos = lax.broadcasted_iota(jnp.int32, (rows, blk), 1)
    keep_prev = (kpos >= qpos) & (i > 0)
    keep_next = (kpos <= qpos) & (i < pl.num_programs(1) - 1)
    sink2 = sink_ref[...] * math.log2(math.e)
    for n in range(kv_heads):
        ks = slice(n * LANES, (n + 1) * LANES)
        q4 = jnp.concatenate([q_ref[:, (n * group + g) * LANES:(n * group + g + 1) * LANES] for g in range(group)],
                             axis=0)
        k_all = jnp.concatenate([kp_ref[:, ks], kc_ref[:, ks], kn_ref[:, ks], kx_ref[:, ks]], axis=0)
        v_all = jnp.concatenate([vp_ref[:, ks], vc_ref[:, ks], vn_ref[:, ks], vx_ref[:, ks]], axis=0)
        s = _lane_chunks(_dot_nt(q4, k_all))
        s[0] = jnp.where(keep_prev, s[0], NEG_INF)
        s[2] = jnp.where(keep_next, s[2], NEG_INF)
        sink_b = jnp.concatenate([jnp.broadcast_to(sink2[n, g:g + 1, :], (blk, LANES)) for g in range(group)], axis=0)
        mx = sink_b
        for t in s:
            mx = jnp.maximum(mx, t)
        m = jnp.broadcast_to(jnp.max(mx, axis=-1, keepdims=True), (rows, LANES))
        ps = [jnp.exp2(t - m) for t in s]
        lsum = ps[0]
        for t in ps[1:]:
            lsum = lsum + t
        l = jnp.sum(lsum, axis=-1, keepdims=True) + jnp.exp2(sink_b - m)[:, 0:1]
        o = _dot(jnp.concatenate(ps, axis=1).astype(BF16), v_all) / l
        for g in range(group):
            col = (n * group + g) * LANES
            z = z_refs[col // zw][:, col % zw:col % zw + LANES].astype(F32)
            o_ref[:, col:col + LANES] = (o[g * blk:(g + 1) * blk] * jax.nn.silu(z)).astype(BF16)


def _win_attention(qkvz, seq, batch, ctx_qkvz, ctx_len, di, kv_heads, sink):
    heads = di // LANES
    group = heads // kv_heads
    kvw = kv_heads * LANES
    blk = ATTN_BLOCK
    nb = seq // blk
    z_col0 = di + 2 * kvw
    zw = math.gcd(z_col0, di)
    n_z = di // zw
    assert di % kvw == 0
    sink_rep = jnp.broadcast_to(sink.astype(F32).reshape(kv_heads, group, 1), (kv_heads, group, LANES))
    band = lambda col_block, shift: pl.BlockSpec(
        (blk, kvw), lambda b, i: (b * nb + jnp.clip(i + shift, 0, nb - 1), col_block))
    k_blk, v_blk = di // kvw, di // kvw + 1
    z_spec = lambda t: pl.BlockSpec((blk, zw), lambda b, i: (b * nb + i, z_col0 // zw + t))
    return pl.pallas_call(
        functools.partial(_win_attn_kernel, group=group, kv_heads=kv_heads, n_z=n_z),
        grid=(batch, nb),
        in_specs=[
            pl.BlockSpec((blk, di), lambda b, i: (b * nb + i, 0)),
            band(k_blk, -1), band(k_blk, 0), band(k_blk, 1),
            band(v_blk, -1), band(v_blk, 0), band(v_blk, 1),
            pl.BlockSpec((ctx_len, kvw), lambda b, i: (b, k_blk)),
            pl.BlockSpec((ctx_len, kvw), lambda b, i: (b, v_blk)),
            pl.BlockSpec((kv_heads, group, LANES), lambda b, i: (0, 0, 0)),
        ] + [z_spec(t) for t in range(n_z)],
        out_specs=pl.BlockSpec((blk, di), lambda b, i: (b * nb + i, 0)),
        out_shape=jax.ShapeDtypeStruct((batch * seq, di), BF16),
        compiler_params=_params("parallel", "arbitrary"),
        name="win_attention",
    )(*([qkvz] * 7 + [ctx_qkvz, ctx_qkvz, sink_rep] + [qkvz] * n_z))


def _x_mod_row(i, tiles_per_seq):
    return i // tiles_per_seq


def kernel(x, c, ctx, c_ctx, l0_norm, l0_w_mod, l0_b_mod, l0_w_in, l0_conv_w, l0_conv_b, l0_w_out, l1_norm, l1_w_mod, l1_b_mod, l1_w_in, l1_q_norm, l1_k_norm, l1_lam_q1, l1_lam_k1, l1_lam_q2, l1_lam_k2, l1_sub_norm, l1_w_out, l2_norm, l2_w_mod, l2_b_mod, l2_w_in, l2_q_norm, l2_k_norm, l2_sink, l2_w_out, l3_norm, l3_w_mod, l3_b_mod, l3_w_in, l3_conv_w, l3_conv_b, l3_w_out):
    batch, seq, d = x.shape
    ctx_len = ctx.shape[1]
    di = l0_w_out.shape[0]
    assert batch < MOD_ROWS
    ctx_mod_row = lambda i, tiles_per_seq: batch

    x2 = x.reshape(batch * seq, d)
    c2 = ctx.reshape(batch * ctx_len, d)
    cc = jnp.zeros((MOD_ROWS, d), F32).at[:batch].set(c).at[batch].set(c_ctx)
    mod0, mod1, mod2, mod3 = (_modulation(cc, w, b) for w, b in
                              ((l0_w_mod, l0_b_mod), (l1_w_mod, l1_b_mod), (l2_w_mod, l2_b_mod), (l3_w_mod, l3_b_mod)))
    bf = lambda w: w.astype(BF16)

    w_in, w_out = bf(l0_w_in), bf(l0_w_out)
    x2n = _conv_layer(x2, seq, mod0, _x_mod_row, l0_norm, w_in, l0_conv_w, l0_conv_b, w_out)
    c2 = _conv_layer(c2, ctx_len, mod0, ctx_mod_row, l0_norm, w_in, l0_conv_w, l0_conv_b, w_out)
    x2 = x2n

    w_in, w_out = bf(l1_w_in), bf(l1_w_out)
    lam_init = 0.8 - 0.6 * math.exp(-0.3 * 1)
    q_scale = l1_q_norm.shape[0] ** -0.5 * math.log2(math.e)
    proj = functools.partial(_proj_layer, norm_g=l1_norm, w_in=w_in, q_norm=l1_q_norm, k_norm=l1_k_norm,
                             q_cols=di, k_cols=di, q_scale=q_scale)
    qx = proj(x2, seq, mod1, _x_mod_row, rope=True)
    qc = proj(c2, ctx_len, mod1, ctx_mod_row, rope=False)
    lam_vecs = jnp.stack([l1_lam_q1, l1_lam_k1, l1_lam_q2, l1_lam_k2]).astype(F32)
    ax = _diff_attention(qx, seq, batch, qc, ctx_len, lam_vecs, l1_sub_norm, lam_init)
    ac = _diff_attention(qc, ctx_len, batch, None, 0, lam_vecs, l1_sub_norm, lam_init)
    x2 = _out_layer(ax, x2, seq, mod1, _x_mod_row, w_out)
    c2 = _out_layer(ac, c2, ctx_len, mod1, ctx_mod_row, w_out)

    w_in, w_out = bf(l2_w_in), bf(l2_w_out)
    kv_cols = (l2_w_in.shape[1] - 2 * di) // 2
    proj = functools.partial(_proj_layer, norm_g=l2_norm, w_in=w_in, q_norm=l2_q_norm, k_norm=l2_k_norm,
                             q_cols=di, k_cols=kv_cols, q_scale=l2_q_norm.shape[0] ** -0.5 * math.log2(math.e))
    qx = proj(x2, seq, mod2, _x_mod_row, rope=True)
    qc = proj(c2, ctx_len, mod2, ctx_mod_row, rope=False)
    ax = _win_attention(qx, seq, batch, qc, ctx_len, di, kv_cols // LANES, l2_sink)
    x2 = _out_layer(ax, x2, seq, mod2, _x_mod_row, w_out)

    x2 = _conv_layer(x2, seq, mod3, _x_mod_row, l3_norm, bf(l3_w_in), l3_conv_w, l3_conv_b, bf(l3_w_out))
    return x2.reshape(batch, seq, d)
```

```python
import functools
import math
from typing import Callable, NamedTuple

import jax
import jax.numpy as jnp
from jax import lax
from jax.experimental import pallas as pl
from jax.experimental.pallas import tpu as pltpu

LANES = 128
SUBLANES = 8
MXU_COLS = 256
BF16_ROWS = 16
GRID_W = 64
ROPE_BASE = 10000.0
EPS = 1e-6
NEG_INF = -1e30
ATTN_BLOCK = 128
ROW_TILE = 512
MOD_ROWS = 8
VMEM_LIMIT = 56 * 1024 * 1024

F32 = jnp.float32
BF16 = jnp.bfloat16


def _params(*sem):
    return pltpu.CompilerParams(dimension_semantics=sem, vmem_limit_bytes=VMEM_LIMIT)


def _tile(n, target):
    if n <= target:
        return n
    t = target - target % LANES
    while n % t:
        t -= LANES
    return t


def _dot(a, b):
    return jnp.dot(a, b, preferred_element_type=F32)


def _dot_nt(a, b):
    return lax.dot_general(a, b, (((1,), (1,)), ((), ())), preferred_element_type=F32)


def _norm_mod(t, g, shift, scale):
    ms = jnp.mean(t * t, axis=-1, keepdims=True)
    y = t * lax.rsqrt(ms + EPS) * g
    return (y * (1.0 + scale) + shift).astype(BF16)


def _mod_kernel(c_ref, w_ref, b_ref, o_ref):
    a = jax.nn.silu(c_ref[...]).astype(BF16)
    o_ref[...] = _dot(a, w_ref[...].astype(BF16)) + b_ref[...]


def _modulation(cc, w_mod, b_mod):
    d, n = w_mod.shape
    tn = _tile(n, 768)
    out = pl.pallas_call(
        _mod_kernel,
        grid=(n // tn,),
        in_specs=[
            pl.BlockSpec((MOD_ROWS, d), lambda j: (0, 0)),
            pl.BlockSpec((d, tn), lambda j: (0, j)),
            pl.BlockSpec((1, tn), lambda j: (0, j)),
        ],
        out_specs=pl.BlockSpec((MOD_ROWS, tn), lambda j: (0, j)),
        out_shape=jax.ShapeDtypeStruct((MOD_ROWS, n), F32),
        compiler_params=_params("arbitrary"),
        name="modulation",
    )(cc, w_mod, b_mod.reshape(1, n))
    return out.reshape(MOD_ROWS, 1, n)


def _conv_kernel(xp_ref, x_ref, xn_ref, mod_ref, g_ref, wb_ref, wc_ref, wx_ref, wz_ref, cw_ref, cb_ref,
                 wo_ref, o_ref, h_scr, *, tm, d, seq):
    i = pl.program_id(0)
    j = pl.program_id(1)
    halo = BF16_ROWS
    mod = mod_ref[0]
    shift, scale, gate = mod[:, :d], mod[:, d:2 * d], mod[:, 2 * d:]

    @pl.when(j == 0)
    def _():
        g = g_ref[...]
        h_scr[0:halo, :] = _norm_mod(xp_ref[...], g, shift, scale)
        h_scr[halo:halo + tm, :] = _norm_mod(x_ref[...], g, shift, scale)
        h_scr[halo + tm:, :] = _norm_mod(xn_ref[...], g, shift, scale)
        o_ref[...] = jnp.zeros_like(o_ref)

    n = tm + 2 * halo
    h = h_scr[...]
    hm = h_scr[halo:halo + tm, :]
    u = _dot(h, wc_ref[...]) * _dot(h, wx_ref[...])
    pos = lax.rem(i * tm + lax.broadcasted_iota(jnp.int32, (tm, 1), 0), seq)
    u_prev = jnp.where(pos == 0, 0.0, pltpu.roll(u, 1, 0)[halo:halo + tm])
    u_next = jnp.where(pos == seq - 1, 0.0, pltpu.roll(u, n - 1, 0)[halo:halo + tm])
    cw = cw_ref[...]
    y = u_prev * cw[0:1] + u[halo:halo + tm] * cw[1:2] + u_next * cw[2:3] + cb_ref[...]
    gated = _dot(hm, wb_ref[...]) * y * jax.nn.silu(_dot(hm, wz_ref[...]))
    o_ref[...] += _dot(gated.astype(BF16), wo_ref[...])

    @pl.when(j == pl.num_programs(1) - 1)
    def _():
        o_ref[...] = x_ref[...] + gate * o_ref[...]


def _conv_layer(x2, stream, mod, norm_g, w_in, conv_w, conv_b, w_out):
    m, d = x2.shape
    di = w_out.shape[0]
    tm = min(stream.span, ROW_TILE)
    tn = _tile(di, 512)
    nj = di // tn
    hb = tm // BF16_ROWS
    last_hb = m // BF16_ROWS - 1
    w_spec = lambda c: pl.BlockSpec((d, tn), lambda i, j: (0, c * nj + j))
    return pl.pallas_call(
        functools.partial(_conv_kernel, tm=tm, d=d, seq=stream.seq),
        grid=(m // tm, nj),
        in_specs=[
            pl.BlockSpec((BF16_ROWS, d), lambda i, j: (jnp.maximum(i * hb - 1, 0), 0)),
            pl.BlockSpec((tm, d), lambda i, j: (i, 0)),
            pl.BlockSpec((BF16_ROWS, d), lambda i, j: (jnp.minimum((i + 1) * hb, last_hb), 0)),
            pl.BlockSpec((1, 1, 3 * d), lambda i, j: (stream.mod_row(i * tm), 0, 0)),
            pl.BlockSpec((1, d), lambda i, j: (0, 0)),
            w_spec(0), w_spec(1), w_spec(2), w_spec(3),
            pl.BlockSpec((3, tn), lambda i, j: (0, j)),
            pl.BlockSpec((1, tn), lambda i, j: (0, j)),
            pl.BlockSpec((tn, d), lambda i, j: (j, 0)),
        ],
        out_specs=pl.BlockSpec((tm, d), lambda i, j: (i, 0)),
        out_shape=jax.ShapeDtypeStruct((m, d), F32),
        scratch_shapes=[pltpu.VMEM((tm + 2 * BF16_ROWS, d), BF16)],
        compiler_params=_params("parallel", "arbitrary"),
        name="conv_layer",
    )(x2, x2, x2, mod, norm_g.reshape(1, d), w_in, w_in, w_in, w_in, conv_w, conv_b.reshape(1, di), w_out)


def _head_norm_rope(acc, gn, cos, sin, group, out_scale):
    outs = []
    swap = group // 4
    for c in range(acc.shape[1] // LANES):
        t = acc[:, c * LANES:(c + 1) * LANES]
        lane = lax.broadcasted_iota(jnp.int32, t.shape, 1)
        sq = t * t
        if group == LANES:
            ms = jnp.mean(sq, axis=-1, keepdims=True)
        else:
            lo = lane < group
            s_lo = jnp.sum(jnp.where(lo, sq, 0.0), axis=-1, keepdims=True)
            s_hi = jnp.sum(jnp.where(lo, 0.0, sq), axis=-1, keepdims=True)
            ms = jnp.where(lo, s_lo, s_hi) * (1.0 / group)
        y = t * lax.rsqrt(ms + EPS) * gn
        if cos is not None:
            ahead = pltpu.roll(y, LANES - swap, 1)
            behind = pltpu.roll(y, swap, 1)
            y = y * cos + jnp.where((lane & swap) == 0, ahead, behind) * sin
        if out_scale != 1.0:
            y = y * out_scale
        outs.append(y)
    return outs[0] if len(outs) == 1 else jnp.concatenate(outs, axis=1)


def _proj_kernel(*refs, d, q_tiles, k_tiles, group, rope, q_scale):
    if rope:
        x_ref, mod_ref, g_ref, w_ref, qn_ref, kn_ref, cos_ref, sin_ref, o_ref, h_scr = refs
    else:
        x_ref, mod_ref, g_ref, w_ref, qn_ref, kn_ref, o_ref, h_scr = refs
    j = pl.program_id(1)

    @pl.when(j == 0)
    def _():
        mod = mod_ref[0]
        h_scr[...] = _norm_mod(x_ref[...], g_ref[...], mod[:, :d], mod[:, d:2 * d])

    cos = cos_ref[...] if rope else None
    sin = sin_ref[...] if rope else None
    tn = o_ref.shape[1]
    cw = min(tn, MXU_COLS)

    def columns(epilogue):
        h = h_scr[...]
        for c in range(tn // cw):
            cols = slice(c * cw, (c + 1) * cw)
            o_ref[:, cols] = epilogue(_dot(h, w_ref[:, cols])).astype(BF16)

    @pl.when(j < q_tiles)
    def _():
        columns(lambda acc: _head_norm_rope(acc, qn_ref[...], cos, sin, group, q_scale))

    @pl.when((j >= q_tiles) & (j < q_tiles + k_tiles))
    def _():
        columns(lambda acc: _head_norm_rope(acc, kn_ref[...], cos, sin, group, 1.0))

    @pl.when(j >= q_tiles + k_tiles)
    def _():
        columns(lambda acc: acc)


def _rope_tables(seq, group):
    rows = seq // GRID_W
    row = jnp.repeat(jnp.arange(rows), GRID_W).astype(F32)
    col = jnp.tile(jnp.arange(GRID_W), rows).astype(F32)
    n_freq = group // 4
    inv_freq = ROPE_BASE ** (-(jnp.arange(n_freq, dtype=F32) / n_freq))
    ar, ac = row[:, None] * inv_freq, col[:, None] * inv_freq
    cos = jnp.concatenate([jnp.cos(ar), jnp.cos(ar), jnp.cos(ac), jnp.cos(ac)], axis=-1)
    sin = jnp.concatenate([-jnp.sin(ar), jnp.sin(ar), -jnp.sin(ac), jnp.sin(ac)], axis=-1)
    reps = LANES // group
    return jnp.tile(cos, (1, reps)), jnp.tile(sin, (1, reps))


def _proj_layer(x2, stream, mod, norm_g, w_in, q_norm, k_norm, q_cols, k_cols, rope, q_scale):
    m, d = x2.shape
    n = w_in.shape[1]
    group = q_norm.shape[0]
    tm = min(stream.span, ROW_TILE)
    tn = math.gcd(math.gcd(q_cols, k_cols), 1024)
    reps = LANES // group
    args = [x2, mod, norm_g.reshape(1, d), w_in,
            jnp.tile(q_norm, reps).reshape(1, LANES), jnp.tile(k_norm, reps).reshape(1, LANES)]
    in_specs = [
        pl.BlockSpec((tm, d), lambda i, j: (i, 0)),
        pl.BlockSpec((1, 1, 3 * d), lambda i, j: (stream.mod_row(i * tm), 0, 0)),
        pl.BlockSpec((1, d), lambda i, j: (0, 0)),
        pl.BlockSpec((d, tn), lambda i, j: (0, j)),
        pl.BlockSpec((1, LANES), lambda i, j: (0, 0)),
        pl.BlockSpec((1, LANES), lambda i, j: (0, 0)),
    ]
    if rope:
        tiles_per_seq = stream.seq // tm
        args += list(_rope_tables(stream.seq, group))
        in_specs += [pl.BlockSpec((tm, LANES), lambda i, j: (i % tiles_per_seq, 0))] * 2
    return pl.pallas_call(
        functools.partial(_proj_kernel, d=d, q_tiles=q_cols // tn, k_tiles=k_cols // tn, group=group,
                          rope=rope, q_scale=q_scale),
        grid=(m // tm, n // tn),
        in_specs=in_specs,
        out_specs=pl.BlockSpec((tm, tn), lambda i, j: (i, j)),
        out_shape=jax.ShapeDtypeStruct((m, n), BF16),
        scratch_shapes=[pltpu.VMEM((tm, d), BF16)],
        compiler_params=_params("parallel", "arbitrary"),
        name="proj_layer",
    )(*args)


def _out_kernel(a_ref, w_ref, x_ref, gate_ref, o_ref):
    o_ref[...] = x_ref[...] + gate_ref[0] * _dot(a_ref[...], w_ref[...])


def _out_layer(a2, x2, stream, mod, w_out):
    m, d = x2.shape
    di = a2.shape[1]
    tm = min(stream.span, ROW_TILE)
    tn = _tile(d, 512)
    gate_block0 = 2 * d // tn
    return pl.pallas_call(
        _out_kernel,
        grid=(m // tm, d // tn),
        in_specs=[
            pl.BlockSpec((tm, di), lambda i, j: (i, 0)),
            pl.BlockSpec((di, tn), lambda i, j: (0, j)),
            pl.BlockSpec((tm, tn), lambda i, j: (i, j)),
            pl.BlockSpec((1, 1, tn), lambda i, j: (stream.mod_row(i * tm), 0, gate_block0 + j)),
        ],
        out_specs=pl.BlockSpec((tm, tn), lambda i, j: (i, j)),
        out_shape=jax.ShapeDtypeStruct((m, d), F32),
        compiler_params=_params("parallel", "arbitrary"),
        name="out_layer",
    )(a2, w_out, x2, mod)


def _lane_chunks(t):
    return [t[:, c * LANES:(c + 1) * LANES] for c in range(t.shape[1] // LANES)]


def _diff_attn_kernel(*refs, tq, tk, n_chunks, has_ctx, tiles, lam_init):
    pipelined = tiles > 1
    if pipelined:
        lam_ref, q_ref, qn_ref = refs[:3]
        rest = refs[3:]
    else:
        lam_ref, q_ref = refs[:2]
        qn_ref = None
        rest = refs[2:]
    if has_ctx:
        k_ref, v_ref, kc_ref, vc_ref, z_ref, sn_ref, o_ref, vt_scr, vct_scr = rest[:9]
        bufs = rest[9:]
        sets = [tuple(bufs[3 * p:3 * p + 3]) for p in range(len(bufs) // 3)]
    else:
        k_ref, v_ref, z_ref, sn_ref, o_ref, vt_scr = rest[:6]
        kc_ref = vc_ref = vct_scr = None
        bufs = rest[6:]
        sets = [(bufs[2 * p], None, bufs[2 * p + 1]) for p in range(len(bufs) // 2)]
    half = LANES // 2

    def tile_rows(t):
        return pl.ds(pl.multiple_of(t * tq, tq), tq)

    def transpose_values():
        for c in range(n_chunks):
            vt_scr[:, c * tk:(c + 1) * tk] = v_ref[c * tk:(c + 1) * tk, :].astype(F32).T.astype(BF16)
        if has_ctx:
            vct_scr[...] = vc_ref[...].astype(F32).T.astype(BF16)

    def sublane_groups(s):
        return s.reshape(s.shape[0] // SUBLANES, SUBLANES, s.shape[1])

    def scores(q, buf, t):
        s_buf, c_buf, m_buf = buf
        lane = lax.broadcasted_iota(jnp.int32, q.shape, 1)
        zero = jnp.zeros_like(q)
        subs = (jnp.where(lane < half, q, zero), jnp.where(lane < half, zero, q))
        for h, qh in enumerate(subs):
            cols = slice(h * tq, (h + 1) * tq)
            mx = jnp.full((SUBLANES, tq), NEG_INF, F32)
            for c in range(n_chunks):
                s = _dot_nt(k_ref[c * tk:(c + 1) * tk, :], qh)
                s_buf[t, c, :, cols] = s
                mx = jnp.maximum(mx, jnp.max(sublane_groups(s), axis=0))
            if has_ctx:
                s = _dot_nt(kc_ref[...], qh)
                c_buf[t, :, cols] = s
                mx = jnp.maximum(mx, jnp.max(sublane_groups(s), axis=0))
            m_buf[t, :, cols] = jnp.broadcast_to(jnp.max(mx, axis=0, keepdims=True), (SUBLANES, tq))

    def weighted(buf, t):
        s_buf, c_buf, m_buf = buf
        m = m_buf[t]
        lsum = jnp.zeros((SUBLANES, 2 * tq), F32)
        acc = jnp.zeros((LANES, 2 * tq), F32)
        blocks = [(s_buf[t, c], vt_scr[:, c * tk:(c + 1) * tk]) for c in range(n_chunks)]
        if has_ctx:
            blocks.append((c_buf[t], vct_scr[...]))
        for s, vt in blocks:
            p = jnp.exp2(sublane_groups(s) - m[None])
            lsum = lsum + jnp.sum(p, axis=0)
            acc = acc + _dot(vt, p.reshape(s.shape).astype(BF16))
        return acc / jnp.sum(lsum, axis=0, keepdims=True)

    lv = lam_ref[...]
    lam = (jnp.exp(jnp.sum(lv[0:1] * lv[1:2], axis=-1, keepdims=True))
           - jnp.exp(jnp.sum(lv[2:3] * lv[3:4], axis=-1, keepdims=True)) + lam_init)

    def finish(o_all, z):
        o = (o_all[:, :tq] - lam * o_all[:, tq:]).T
        ms = jnp.mean(o * o, axis=-1, keepdims=True)
        y = o * lax.rsqrt(ms + EPS) * sn_ref[...] * (1.0 - lam_init)
        return (y * jax.nn.silu(z.astype(F32))).astype(BF16)

    if not pipelined:
        transpose_values()
        scores(q_ref[...], sets[0], 0)
        o_ref[...] = finish(weighted(sets[0], 0), z_ref[...])
        return

    i = pl.program_id(2)

    @pl.when(i == 0)
    def _():
        transpose_values()

        @pl.loop(0, tiles)
        def _(t):
            scores(q_ref[tile_rows(t), :], sets[0], t)

    def both(q, cur, nxt, t):
        s_nxt, c_nxt, m_nxt = nxt
        s_cur, c_cur, m_cur = cur
        lane = lax.broadcasted_iota(jnp.int32, q.shape, 1)
        zero = jnp.zeros_like(q)
        subs = (jnp.where(lane < half, q, zero), jnp.where(lane < half, zero, q))
        mxs = [jnp.full((SUBLANES, tq), NEG_INF, F32) for _ in subs]
        m = m_cur[t]
        lsum = jnp.zeros((SUBLANES, 2 * tq), F32)
        acc = jnp.zeros((LANES, 2 * tq), F32)
        tie = None
        for c in range(n_chunks + has_ctx):
            if c < n_chunks:
                k, vt, s_old = k_ref[c * tk:(c + 1) * tk, :], vt_scr[:, c * tk:(c + 1) * tk], s_cur[t, c]
            else:
                k, vt, s_old = kc_ref[...], vct_scr[...], c_cur[t]
            if tie is not None:
                k = k + jnp.tile(tie, (k.shape[0] // SUBLANES, 1)).astype(BF16)
            for h, qh in enumerate(subs):
                cols = slice(h * tq, (h + 1) * tq)
                s = _dot_nt(k, qh)
                if c < n_chunks:
                    s_nxt[t, c, :, cols] = s
                else:
                    c_nxt[t, :, cols] = s
                mxs[h] = jnp.maximum(mxs[h], jnp.max(sublane_groups(s), axis=0))
            p = jnp.exp2(sublane_groups(s_old) - m[None])
            lsum = lsum + jnp.sum(p, axis=0)
            acc = acc + _dot(vt, p.reshape(s_old.shape).astype(BF16))
            bits = pltpu.bitcast(p[-1][:, :LANES], jnp.uint32)
            tie = pltpu.bitcast((bits >> 16) >> 16, F32)
        for h, mx in enumerate(mxs):
            m_nxt[t, :, h * tq:(h + 1) * tq] = jnp.broadcast_to(jnp.max(mx, axis=0, keepdims=True), (SUBLANES, tq))
        return acc / jnp.sum(lsum, axis=0, keepdims=True)

    def step(cur, nxt):
        for t in range(tiles):
            rows = slice(t * tq, (t + 1) * tq)
            o_ref[rows, :] = finish(both(qn_ref[rows, :], cur, nxt, t), z_ref[rows, :])

    @pl.when(i % 2 == 0)
    def _():
        step(sets[0], sets[1])

    @pl.when(i % 2 == 1)
    def _():
        step(sets[1], sets[0])


def _diff_attention(qkvz, seq, batch, ctx_qkvz, ctx_len, lam_vecs, sub_norm, lam_init):
    di = qkvz.shape[1] // 4
    heads = di // LANES
    tq = min(seq, 256)
    tk = min(seq, 512)
    tiles = 2 if seq >= 2 * tq else 1
    rows = tiles * tq
    nq = seq // rows
    has_ctx = ctx_qkvz is not None
    args = [lam_vecs, qkvz]
    in_specs = [
        pl.BlockSpec(lam_vecs.shape, lambda b, h, i: (0, 0)),
        pl.BlockSpec((rows, LANES), lambda b, h, i: (b * nq + i, h)),
    ]
    if tiles > 1:
        args += [qkvz]
        in_specs += [pl.BlockSpec((rows, LANES), lambda b, h, i: (b * nq + jnp.minimum(i + 1, nq - 1), h))]
    args += [qkvz, qkvz]
    in_specs += [
        pl.BlockSpec((seq, LANES), lambda b, h, i: (b, heads + h)),
        pl.BlockSpec((seq, LANES), lambda b, h, i: (b, 2 * heads + h)),
    ]
    if has_ctx:
        args += [ctx_qkvz, ctx_qkvz]
        in_specs += [
            pl.BlockSpec((ctx_len, LANES), lambda b, h, i: (b, heads + h)),
            pl.BlockSpec((ctx_len, LANES), lambda b, h, i: (b, 2 * heads + h)),
        ]
    args += [qkvz, sub_norm.reshape(1, LANES)]
    in_specs += [
        pl.BlockSpec((rows, LANES), lambda b, h, i: (b * nq + i, 3 * heads + h)),
        pl.BlockSpec((1, LANES), lambda b, h, i: (0, 0)),
    ]
    scratch = [pltpu.VMEM((LANES, seq), BF16)] + ([pltpu.VMEM((LANES, ctx_len), BF16)] if has_ctx else [])
    for _ in range(2 if tiles > 1 else 1):
        scratch += [pltpu.VMEM((tiles, seq // tk, tk, 2 * tq), F32)]
        if has_ctx:
            scratch += [pltpu.VMEM((tiles, ctx_len, 2 * tq), F32)]
        scratch += [pltpu.VMEM((tiles, SUBLANES, 2 * tq), F32)]
    return pl.pallas_call(
        functools.partial(_diff_attn_kernel, tq=tq, tk=tk, n_chunks=seq // tk, has_ctx=has_ctx, tiles=tiles,
                          lam_init=lam_init),
        grid=(batch, heads, nq),
        in_specs=in_specs,
        out_specs=pl.BlockSpec((rows, LANES), lambda b, h, i: (b * nq + i, h)),
        out_shape=jax.ShapeDtypeStruct((batch * seq, di), BF16),
        scratch_shapes=scratch,
        compiler_params=_params("parallel", "parallel", "arbitrary"),
        name="diff_attention",
    )(*args)


def _win_attn_kernel(*refs, group, kv_heads, n_z):
    q_ref, kp_ref, kc_ref, kn_ref, vp_ref, vc_ref, vn_ref, kx_ref, vx_ref, sink_ref = refs[:10]
    z_refs = refs[10:10 + n_z]
    o_ref = refs[10 + n_z]
    i = pl.program_id(1)
    blk = ATTN_BLOCK
    rows = group * blk
    zw = z_refs[0].shape[1]
    qpos = lax.broadcasted_iota(jnp.int32, (rows, blk), 0) % blk
    kpos = lax.broadcasted_iota(jnp.int32, (rows, blk), 1)
    keep_prev = (kpos >= qpos) & (i > 0)
    keep_next = (kpos <= qpos) & (i < pl.num_programs(1) - 1)
    sink2 = sink_ref[...] * math.log2(math.e)
    for n in range(kv_heads):
        ks = slice(n * LANES, (n + 1) * LANES)
        q4 = jnp.concatenate([q_ref[:, (n * group + g) * LANES:(n * group + g + 1) * LANES] for g in range(group)],
                             axis=0)
        k_all = jnp.concatenate([kp_ref[:, ks], kc_ref[:, ks], kn_ref[:, ks], kx_ref[:, ks]], axis=0)
        v_all = jnp.concatenate([vp_ref[:, ks], vc_ref[:, ks], vn_ref[:, ks], vx_ref[:, ks]], axis=0)
        s = _lane_chunks(_dot_nt(q4, k_all))
        s[0] = jnp.where(keep_prev, s[0], NEG_INF)
        s[2] = jnp.where(keep_next, s[2], NEG_INF)
        sink_b = jnp.concatenate([jnp.broadcast_to(sink2[n, g:g + 1, :], (blk, LANES)) for g in range(group)], axis=0)
        mx = sink_b
        for t in s:
            mx = jnp.maximum(mx, t)
        m = jnp.broadcast_to(jnp.max(mx, axis=-1, keepdims=True), (rows, LANES))
        ps = [jnp.exp2(t - m) for t in s]
        lsum = ps[0]
        for t in ps[1:]:
            lsum = lsum + t
        l = jnp.sum(lsum, axis=-1, keepdims=True) + jnp.exp2(sink_b - m)[:, 0:1]
        o = _dot(jnp.concatenate(ps, axis=1).astype(BF16), v_all) / l
        for g in range(group):
            col = (n * group + g) * LANES
            z = z_refs[col // zw][:, col % zw:col % zw + LANES].astype(F32)
            o_ref[:, col:col + LANES] = (o[g * blk:(g + 1) * blk] * jax.nn.silu(z)).astype(BF16)


def _win_attention(qkvz, seq, batch, ctx_qkvz, ctx_len, di, kv_heads, sink):
    heads = di // LANES
    group = heads // kv_heads
    kvw = kv_heads * LANES
    blk = ATTN_BLOCK
    nb = seq // blk
    z_col0 = di + 2 * kvw
    zw = math.gcd(z_col0, di)
    n_z = di // zw
    assert di % kvw == 0
    sink_rep = jnp.broadcast_to(sink.astype(F32).reshape(kv_heads, group, 1), (kv_heads, group, LANES))
    band = lambda col_block, shift: pl.BlockSpec(
        (blk, kvw), lambda b, i: (b * nb + jnp.clip(i + shift, 0, nb - 1), col_block))
    k_blk, v_blk = di // kvw, di // kvw + 1
    z_spec = lambda t: pl.BlockSpec((blk, zw), lambda b, i: (b * nb + i, z_col0 // zw + t))
    return pl.pallas_call(
        functools.partial(_win_attn_kernel, group=group, kv_heads=kv_heads, n_z=n_z),
        grid=(batch, nb),
        in_specs=[
            pl.BlockSpec((blk, di), lambda b, i: (b * nb + i, 0)),
            band(k_blk, -1), band(k_blk, 0), band(k_blk, 1),
            band(v_blk, -1), band(v_blk, 0), band(v_blk, 1),
            pl.BlockSpec((ctx_len, kvw), lambda b, i: (b, k_blk)),
            pl.BlockSpec((ctx_len, kvw), lambda b, i: (b, v_blk)),
            pl.BlockSpec((kv_heads, group, LANES), lambda b, i: (0, 0, 0)),
        ] + [z_spec(t) for t in range(n_z)],
        out_specs=pl.BlockSpec((blk, di), lambda b, i: (b * nb + i, 0)),
        out_shape=jax.ShapeDtypeStruct((batch * seq, di), BF16),
        compiler_params=_params("parallel", "arbitrary"),
        name="win_attention",
    )(*([qkvz] * 7 + [ctx_qkvz, ctx_qkvz, sink_rep] + [qkvz] * n_z))


class _Stream(NamedTuple):
    seq: int
    span: int
    mod_row: Callable


def kernel(x, c, ctx, c_ctx, l0_norm, l0_w_mod, l0_b_mod, l0_w_in, l0_conv_w, l0_conv_b, l0_w_out, l1_norm, l1_w_mod, l1_b_mod, l1_w_in, l1_q_norm, l1_k_norm, l1_lam_q1, l1_lam_k1, l1_lam_q2, l1_lam_k2, l1_sub_norm, l1_w_out, l2_norm, l2_w_mod, l2_b_mod, l2_w_in, l2_q_norm, l2_k_norm, l2_sink, l2_w_out, l3_norm, l3_w_mod, l3_b_mod, l3_w_in, l3_conv_w, l3_conv_b, l3_w_out):
    batch, seq, d = x.shape
    ctx_len = ctx.shape[1]
    di = l0_w_out.shape[0]
    assert batch < MOD_ROWS
    xs = _Stream(seq, seq, lambda row: row // seq)
    cs = _Stream(ctx_len, batch * ctx_len, lambda row: batch)

    x2 = x.reshape(batch * seq, d)
    c2 = ctx.reshape(batch * ctx_len, d)
    cc = jnp.zeros((MOD_ROWS, d), F32).at[:batch].set(c).at[batch].set(c_ctx)
    mod0, mod1, mod2, mod3 = (_modulation(cc, w, b) for w, b in
                              ((l0_w_mod, l0_b_mod), (l1_w_mod, l1_b_mod), (l2_w_mod, l2_b_mod), (l3_w_mod, l3_b_mod)))
    bf = lambda w: w.astype(BF16)

    w_in, w_out = bf(l0_w_in), bf(l0_w_out)
    x2n = _conv_layer(x2, xs, mod0, l0_norm, w_in, l0_conv_w, l0_conv_b, w_out)
    c2 = _conv_layer(c2, cs, mod0, l0_norm, w_in, l0_conv_w, l0_conv_b, w_out)
    x2 = x2n

    w_in, w_out = bf(l1_w_in), bf(l1_w_out)
    lam_init = 0.8 - 0.6 * math.exp(-0.3 * 1)
    q_scale = l1_q_norm.shape[0] ** -0.5 * math.log2(math.e)
    proj = functools.partial(_proj_layer, norm_g=l1_norm, w_in=w_in, q_norm=l1_q_norm, k_norm=l1_k_norm,
                             q_cols=di, k_cols=di, q_scale=q_scale)
    qx = proj(x2, xs, mod1, rope=True)
    qc = proj(c2, cs, mod1, rope=False)
    lam_vecs = jnp.stack([l1_lam_q1, l1_lam_k1, l1_lam_q2, l1_lam_k2]).astype(F32)
    ax = _diff_attention(qx, seq, batch, qc, ctx_len, lam_vecs, l1_sub_norm, lam_init)
    ac = _diff_attention(qc, ctx_len, batch, None, 0, lam_vecs, l1_sub_norm, lam_init)
    x2 = _out_layer(ax, x2, xs, mod1, w_out)
    c2 = _out_layer(ac, c2, cs, mod1, w_out)

    w_in, w_out = bf(l2_w_in), bf(l2_w_out)
    kv_cols = (l2_w_in.shape[1] - 2 * di) // 2
    proj = functools.partial(_proj_layer, norm_g=l2_norm, w_in=w_in, q_norm=l2_q_norm, k_norm=l2_k_norm,
                             q_cols=di, k_cols=kv_cols, q_scale=l2_q_norm.shape[0] ** -0.5 * math.log2(math.e))
    qx = proj(x2, xs, mod2, rope=True)
    qc = proj(c2, cs, mod2, rope=False)
    ax = _win_attention(qx, seq, batch, qc, ctx_len, di, kv_cols // LANES, l2_sink)
    x2 = _out_layer(ax, x2, xs, mod2, w_out)

    x2 = _conv_layer(x2, xs, mod3, l3_norm, bf(l3_w_in), l3_conv_w, l3_conv_b, bf(l3_w_out))
    return x2.reshape(batch, seq, d)
```

```python
import functools
import math
from typing import Callable, NamedTuple

import jax
import jax.numpy as jnp
from jax import lax
from jax.experimental import pallas as pl
from jax.experimental.pallas import tpu as pltpu

LANES = 128
SUBLANES = 8
MXU_COLS = 256
BF16_ROWS = 16
GRID_W = 64
ROPE_BASE = 10000.0
EPS = 1e-6
NEG_INF = -1e30
ATTN_BLOCK = 128
EPILOGUE_ROWS = 256
ROW_TILE = 512
MOD_ROWS = 8
VMEM_LIMIT = 56 * 1024 * 1024

F32 = jnp.float32
BF16 = jnp.bfloat16


def _params(*sem):
    return pltpu.CompilerParams(dimension_semantics=sem, vmem_limit_bytes=VMEM_LIMIT)


def _tile(n, target):
    if n <= target:
        return n
    t = target - target % LANES
    while n % t:
        t -= LANES
    return t


def _dot(a, b):
    return jnp.dot(a, b, preferred_element_type=F32)


def _dot_nt(a, b):
    return lax.dot_general(a, b, (((1,), (1,)), ((), ())), preferred_element_type=F32)


def _norm_mod(t, g, shift, scale):
    ms = jnp.mean(t * t, axis=-1, keepdims=True)
    y = t * lax.rsqrt(ms + EPS) * g
    return (y * (1.0 + scale) + shift).astype(BF16)


def _mod_kernel(c_ref, w_ref, b_ref, o_ref):
    a = jax.nn.silu(c_ref[...]).astype(BF16)
    o_ref[...] = _dot(a, w_ref[...].astype(BF16)) + b_ref[...]


def _modulation(cc, w_mod, b_mod):
    d, n = w_mod.shape
    tn = _tile(n, 768)
    out = pl.pallas_call(
        _mod_kernel,
        grid=(n // tn,),
        in_specs=[
            pl.BlockSpec((MOD_ROWS, d), lambda j: (0, 0)),
            pl.BlockSpec((d, tn), lambda j: (0, j)),
            pl.BlockSpec((1, tn), lambda j: (0, j)),
        ],
        out_specs=pl.BlockSpec((MOD_ROWS, tn), lambda j: (0, j)),
        out_shape=jax.ShapeDtypeStruct((MOD_ROWS, n), F32),
        compiler_params=_params("arbitrary"),
        name="modulation",
    )(cc, w_mod, b_mod.reshape(1, n))
    return out.reshape(MOD_ROWS, 1, n)


def _conv_kernel(xp_ref, x_ref, xn_ref, mod_ref, g_ref, wb_ref, wc_ref, wx_ref, wz_ref, cw_ref, cb_ref,
                 wo_ref, o_ref, h_scr, *, tm, d, seq):
    i = pl.program_id(0)
    j = pl.program_id(1)
    halo = BF16_ROWS
    mod = mod_ref[0]
    shift, scale, gate = mod[:, :d], mod[:, d:2 * d], mod[:, 2 * d:]

    @pl.when(j == 0)
    def _():
        g = g_ref[...]
        h_scr[0:halo, :] = _norm_mod(xp_ref[...], g, shift, scale)
        h_scr[halo:halo + tm, :] = _norm_mod(x_ref[...], g, shift, scale)
        h_scr[halo + tm:, :] = _norm_mod(xn_ref[...], g, shift, scale)
        o_ref[...] = jnp.zeros_like(o_ref)

    n = tm + 2 * halo
    h = h_scr[...]
    hm = h_scr[halo:halo + tm, :]
    u = _dot(h, wc_ref[...]) * _dot(h, wx_ref[...])
    pos = lax.rem(i * tm + lax.broadcasted_iota(jnp.int32, (tm, 1), 0), seq)
    u_prev = jnp.where(pos == 0, 0.0, pltpu.roll(u, 1, 0)[halo:halo + tm])
    u_next = jnp.where(pos == seq - 1, 0.0, pltpu.roll(u, n - 1, 0)[halo:halo + tm])
    cw = cw_ref[...]
    y = u_prev * cw[0:1] + u[halo:halo + tm] * cw[1:2] + u_next * cw[2:3] + cb_ref[...]
    gated = _dot(hm, wb_ref[...]) * y * jax.nn.silu(_dot(hm, wz_ref[...]))
    o_ref[...] += _dot(gated.astype(BF16), wo_ref[...])

    @pl.when(j == pl.num_programs(1) - 1)
    def _():
        o_ref[...] = x_ref[...] + gate * o_ref[...]


def _conv_layer(x2, stream, mod, norm_g, w_in, conv_w, conv_b, w_out):
    m, d = x2.shape
    di = w_out.shape[0]
    tm = min(stream.span, ROW_TILE)
    tn = _tile(di, 512)
    nj = di // tn
    hb = tm // BF16_ROWS
    last_hb = m // BF16_ROWS - 1
    w_spec = lambda c: pl.BlockSpec((d, tn), lambda i, j: (0, c * nj + j))
    return pl.pallas_call(
        functools.partial(_conv_kernel, tm=tm, d=d, seq=stream.seq),
        grid=(m // tm, nj),
        in_specs=[
            pl.BlockSpec((BF16_ROWS, d), lambda i, j: (jnp.maximum(i * hb - 1, 0), 0)),
            pl.BlockSpec((tm, d), lambda i, j: (i, 0)),
            pl.BlockSpec((BF16_ROWS, d), lambda i, j: (jnp.minimum((i + 1) * hb, last_hb), 0)),
            pl.BlockSpec((1, 1, 3 * d), lambda i, j: (stream.mod_row(i * tm), 0, 0)),
            pl.BlockSpec((1, d), lambda i, j: (0, 0)),
            w_spec(0), w_spec(1), w_spec(2), w_spec(3),
            pl.BlockSpec((3, tn), lambda i, j: (0, j)),
            pl.BlockSpec((1, tn), lambda i, j: (0, j)),
            pl.BlockSpec((tn, d), lambda i, j: (j, 0)),
        ],
        out_specs=pl.BlockSpec((tm, d), lambda i, j: (i, 0)),
        out_shape=jax.ShapeDtypeStruct((m, d), F32),
        scratch_shapes=[pltpu.VMEM((tm + 2 * BF16_ROWS, d), BF16)],
        compiler_params=_params("parallel", "arbitrary"),
        name="conv_layer",
    )(x2, x2, x2, mod, norm_g.reshape(1, d), w_in, w_in, w_in, w_in, conv_w, conv_b.reshape(1, di), w_out)


def _head_norm_rope(acc, gn, cos, sin, group, out_scale):
    outs = []
    swap = group // 4
    for c in range(acc.shape[1] // LANES):
        t = acc[:, c * LANES:(c + 1) * LANES]
        lane = lax.broadcasted_iota(jnp.int32, t.shape, 1)
        sq = t * t
        if group == LANES:
            ms = jnp.mean(sq, axis=-1, keepdims=True)
        else:
            lo = lane < group
            s_lo = jnp.sum(jnp.where(lo, sq, 0.0), axis=-1, keepdims=True)
            s_hi = jnp.sum(jnp.where(lo, 0.0, sq), axis=-1, keepdims=True)
            ms = jnp.where(lo, s_lo, s_hi) * (1.0 / group)
        y = t * lax.rsqrt(ms + EPS) * gn
        if cos is not None:
            ahead = pltpu.roll(y, LANES - swap, 1)
            behind = pltpu.roll(y, swap, 1)
            y = y * cos + jnp.where((lane & swap) == 0, ahead, behind) * sin
        if out_scale != 1.0:
            y = y * out_scale
        outs.append(y)
    return outs[0] if len(outs) == 1 else jnp.concatenate(outs, axis=1)


def _proj_kernel(*refs, d, q_tiles, k_tiles, group, rope, q_scale):
    if rope:
        x_ref, mod_ref, g_ref, w_ref, qn_ref, kn_ref, cos_ref, sin_ref, o_ref, h_scr = refs
    else:
        x_ref, mod_ref, g_ref, w_ref, qn_ref, kn_ref, o_ref, h_scr = refs
    j = pl.program_id(1)

    @pl.when(j == 0)
    def _():
        mod = mod_ref[0]
        h_scr[...] = _norm_mod(x_ref[...], g_ref[...], mod[:, :d], mod[:, d:2 * d])

    tm, tn = o_ref.shape
    cw = min(tn, MXU_COLS)

    def units(rh, epilogue):
        for r in range(tm // rh):
            rows = slice(r * rh, (r + 1) * rh)
            h = h_scr[rows, :]
            for c in range(tn // cw):
                cols = slice(c * cw, (c + 1) * cw)
                o_ref[rows, cols] = epilogue(_dot(h, w_ref[:, cols]), rows).astype(BF16)

    def head_epilogue(gn_ref, scale):
        def epilogue(acc, rows):
            cos = cos_ref[rows, :] if rope else None
            sin = sin_ref[rows, :] if rope else None
            return _head_norm_rope(acc, gn_ref[...], cos, sin, group, scale)
        return epilogue

    @pl.when(j < q_tiles)
    def _():
        units(min(tm, EPILOGUE_ROWS), head_epilogue(qn_ref, q_scale))

    @pl.when((j >= q_tiles) & (j < q_tiles + k_tiles))
    def _():
        units(min(tm, EPILOGUE_ROWS), head_epilogue(kn_ref, 1.0))

    @pl.when(j >= q_tiles + k_tiles)
    def _():
        units(tm, lambda acc, rows: acc)


def _rope_tables(seq, group):
    rows = seq // GRID_W
    row = jnp.repeat(jnp.arange(rows), GRID_W).astype(F32)
    col = jnp.tile(jnp.arange(GRID_W), rows).astype(F32)
    n_freq = group // 4
    inv_freq = ROPE_BASE ** (-(jnp.arange(n_freq, dtype=F32) / n_freq))
    ar, ac = row[:, None] * inv_freq, col[:, None] * inv_freq
    cos = jnp.concatenate([jnp.cos(ar), jnp.cos(ar), jnp.cos(ac), jnp.cos(ac)], axis=-1)
    sin = jnp.concatenate([-jnp.sin(ar), jnp.sin(ar), -jnp.sin(ac), jnp.sin(ac)], axis=-1)
    reps = LANES // group
    return jnp.tile(cos, (1, reps)), jnp.tile(sin, (1, reps))


def _proj_layer(x2, stream, mod, norm_g, w_in, q_norm, k_norm, q_cols, k_cols, rope, q_scale):
    m, d = x2.shape
    n = w_in.shape[1]
    group = q_norm.shape[0]
    tm = min(stream.span, ROW_TILE)
    tn = math.gcd(math.gcd(q_cols, k_cols), 1024)
    reps = LANES // group
    args = [x2, mod, norm_g.reshape(1, d), w_in,
            jnp.tile(q_norm, reps).reshape(1, LANES), jnp.tile(k_norm, reps).reshape(1, LANES)]
    in_specs = [
        pl.BlockSpec((tm, d), lambda i, j: (i, 0)),
        pl.BlockSpec((1, 1, 3 * d), lambda i, j: (stream.mod_row(i * tm), 0, 0)),
        pl.BlockSpec((1, d), lambda i, j: (0, 0)),
        pl.BlockSpec((d, tn), lambda i, j: (0, j)),
        pl.BlockSpec((1, LANES), lambda i, j: (0, 0)),
        pl.BlockSpec((1, LANES), lambda i, j: (0, 0)),
    ]
    if rope:
        tiles_per_seq = stream.seq // tm
        args += list(_rope_tables(stream.seq, group))
        in_specs += [pl.BlockSpec((tm, LANES), lambda i, j: (i % tiles_per_seq, 0))] * 2
    return pl.pallas_call(
        functools.partial(_proj_kernel, d=d, q_tiles=q_cols // tn, k_tiles=k_cols // tn, group=group,
                          rope=rope, q_scale=q_scale),
        grid=(m // tm, n // tn),
        in_specs=in_specs,
        out_specs=pl.BlockSpec((tm, tn), lambda i, j: (i, j)),
        out_shape=jax.ShapeDtypeStruct((m, n), BF16),
        scratch_shapes=[pltpu.VMEM((tm, d), BF16)],
        compiler_params=_params("parallel", "arbitrary"),
        name="proj_layer",
    )(*args)


def _out_kernel(a_ref, w_ref, x_ref, gate_ref, o_ref):
    o_ref[...] = x_ref[...] + gate_ref[0] * _dot(a_ref[...], w_ref[...])


def _out_layer(a2, x2, stream, mod, w_out):
    m, d = x2.shape
    di = a2.shape[1]
    tm = min(stream.span, ROW_TILE)
    tn = _tile(d, 1024)
    gate_block0 = 2 * d // tn
    return pl.pallas_call(
        _out_kernel,
        grid=(m // tm, d // tn),
        in_specs=[
            pl.BlockSpec((tm, di), lambda i, j: (i, 0)),
            pl.BlockSpec((di, tn), lambda i, j: (0, j)),
            pl.BlockSpec((tm, tn), lambda i, j: (i, j)),
            pl.BlockSpec((1, 1, tn), lambda i, j: (stream.mod_row(i * tm), 0, gate_block0 + j)),
        ],
        out_specs=pl.BlockSpec((tm, tn), lambda i, j: (i, j)),
        out_shape=jax.ShapeDtypeStruct((m, d), F32),
        compiler_params=_params("parallel", "arbitrary"),
        name="out_layer",
    )(a2, w_out, x2, mod)


def _lane_chunks(t):
    return [t[:, c * LANES:(c + 1) * LANES] for c in range(t.shape[1] // LANES)]


def _diff_attn_kernel(*refs, tq, tk, n_chunks, has_ctx, tiles, lam_init):
    pipelined = tiles > 1
    if pipelined:
        lam_ref, q_ref, qn_ref = refs[:3]
        rest = refs[3:]
    else:
        lam_ref, q_ref = refs[:2]
        qn_ref = None
        rest = refs[2:]
    if has_ctx:
        k_ref, v_ref, kc_ref, vc_ref, z_ref, sn_ref, o_ref, vt_scr, vct_scr = rest[:9]
        bufs = rest[9:]
        sets = [tuple(bufs[3 * p:3 * p + 3]) for p in range(len(bufs) // 3)]
    else:
        k_ref, v_ref, z_ref, sn_ref, o_ref, vt_scr = rest[:6]
        kc_ref = vc_ref = vct_scr = None
        bufs = rest[6:]
        sets = [(bufs[2 * p], None, bufs[2 * p + 1]) for p in range(len(bufs) // 2)]
    half = LANES // 2

    def tile_rows(t):
        return pl.ds(pl.multiple_of(t * tq, tq), tq)

    def transpose_values():
        for c in range(n_chunks):
            vt_scr[:, c * tk:(c + 1) * tk] = v_ref[c * tk:(c + 1) * tk, :].astype(F32).T.astype(BF16)
        if has_ctx:
            vct_scr[...] = vc_ref[...].astype(F32).T.astype(BF16)

    def sublane_groups(s):
        return s.reshape(s.shape[0] // SUBLANES, SUBLANES, s.shape[1])

    def scores(q, buf, t):
        s_buf, c_buf, m_buf = buf
        lane = lax.broadcasted_iota(jnp.int32, q.shape, 1)
        zero = jnp.zeros_like(q)
        subs = (jnp.where(lane < half, q, zero), jnp.where(lane < half, zero, q))
        for h, qh in enumerate(subs):
            cols = slice(h * tq, (h + 1) * tq)
            mx = jnp.full((SUBLANES, tq), NEG_INF, F32)
            for c in range(n_chunks):
                s = _dot_nt(k_ref[c * tk:(c + 1) * tk, :], qh)
                s_buf[t, c, :, cols] = s
                mx = jnp.maximum(mx, jnp.max(sublane_groups(s), axis=0))
            if has_ctx:
                s = _dot_nt(kc_ref[...], qh)
                c_buf[t, :, cols] = s
                mx = jnp.maximum(mx, jnp.max(sublane_groups(s), axis=0))
            m_buf[t, :, cols] = jnp.broadcast_to(jnp.max(mx, axis=0, keepdims=True), (SUBLANES, tq))

    def weighted(buf, t):
        s_buf, c_buf, m_buf = buf
        m = m_buf[t]
        lsum = jnp.zeros((SUBLANES, 2 * tq), F32)
        acc = jnp.zeros((LANES, 2 * tq), F32)
        blocks = [(s_buf[t, c], vt_scr[:, c * tk:(c + 1) * tk]) for c in range(n_chunks)]
        if has_ctx:
            blocks.append((c_buf[t], vct_scr[...]))
        for s, vt in blocks:
            p = jnp.exp2(sublane_groups(s) - m[None])
            lsum = lsum + jnp.sum(p, axis=0)
            acc = acc + _dot(vt, p.reshape(s.shape).astype(BF16))
        return acc / jnp.sum(lsum, axis=0, keepdims=True)

    lv = lam_ref[...]
    lam = (jnp.exp(jnp.sum(lv[0:1] * lv[1:2], axis=-1, keepdims=True))
           - jnp.exp(jnp.sum(lv[2:3] * lv[3:4], axis=-1, keepdims=True)) + lam_init)

    def finish(o_all, z):
        o = (o_all[:, :tq] - lam * o_all[:, tq:]).T
        ms = jnp.mean(o * o, axis=-1, keepdims=True)
        y = o * lax.rsqrt(ms + EPS) * sn_ref[...] * (1.0 - lam_init)
        return (y * jax.nn.silu(z.astype(F32))).astype(BF16)

    if not pipelined:
        transpose_values()
        scores(q_ref[...], sets[0], 0)
        o_ref[...] = finish(weighted(sets[0], 0), z_ref[...])
        return

    i = pl.program_id(2)

    @pl.when(i == 0)
    def _():
        transpose_values()

        @pl.loop(0, tiles)
        def _(t):
            scores(q_ref[tile_rows(t), :], sets[0], t)

    def both(q, cur, nxt, t):
        s_nxt, c_nxt, m_nxt = nxt
        s_cur, c_cur, m_cur = cur
        lane = lax.broadcasted_iota(jnp.int32, q.shape, 1)
        zero = jnp.zeros_like(q)
        subs = (jnp.where(lane < half, q, zero), jnp.where(lane < half, zero, q))
        mxs = [jnp.full((SUBLANES, tq), NEG_INF, F32) for _ in subs]
        m = m_cur[t]
        lsum = jnp.zeros((SUBLANES, 2 * tq), F32)
        acc = jnp.zeros((LANES, 2 * tq), F32)
        tie = None
        for c in range(n_chunks + has_ctx):
            if c < n_chunks:
                k, vt, s_old = k_ref[c * tk:(c + 1) * tk, :], vt_scr[:, c * tk:(c + 1) * tk], s_cur[t, c]
            else:
                k, vt, s_old = kc_ref[...], vct_scr[...], c_cur[t]
            if tie is not None:
                k = k + jnp.tile(tie, (k.shape[0] // SUBLANES, 1)).astype(BF16)
            for h, qh in enumerate(subs):
                cols = slice(h * tq, (h + 1) * tq)
                s = _dot_nt(k, qh)
                if c < n_chunks:
                    s_nxt[t, c, :, cols] = s
                else:
                    c_nxt[t, :, cols] = s
                mxs[h] = jnp.maximum(mxs[h], jnp.max(sublane_groups(s), axis=0))
            p = jnp.exp2(sublane_groups(s_old) - m[None])
            lsum = lsum + jnp.sum(p, axis=0)
            acc = acc + _dot(vt, p.reshape(s_old.shape).astype(BF16))
            bits = pltpu.bitcast(p[-1][:, :LANES], jnp.uint32)
            tie = pltpu.bitcast((bits >> 16) >> 16, F32)
        for h, mx in enumerate(mxs):
            m_nxt[t, :, h * tq:(h + 1) * tq] = jnp.broadcast_to(jnp.max(mx, axis=0, keepdims=True), (SUBLANES, tq))
        return acc / jnp.sum(lsum, axis=0, keepdims=True)

    def step(cur, nxt):
        for t in range(tiles):
            rows = slice(t * tq, (t + 1) * tq)
            o_ref[rows, :] = finish(both(qn_ref[rows, :], cur, nxt, t), z_ref[rows, :])

    @pl.when(i % 2 == 0)
    def _():
        step(sets[0], sets[1])

    @pl.when(i % 2 == 1)
    def _():
        step(sets[1], sets[0])


def _diff_attention(qkvz, seq, batch, ctx_qkvz, ctx_len, lam_vecs, sub_norm, lam_init):
    di = qkvz.shape[1] // 4
    heads = di // LANES
    tq = min(seq, 256)
    tk = min(seq, 512)
    tiles = 2 if seq >= 2 * tq else 1
    rows = tiles * tq
    nq = seq // rows
    has_ctx = ctx_qkvz is not None
    args = [lam_vecs, qkvz]
    in_specs = [
        pl.BlockSpec(lam_vecs.shape, lambda b, h, i: (0, 0)),
        pl.BlockSpec((rows, LANES), lambda b, h, i: (b * nq + i, h)),
    ]
    if tiles > 1:
        args += [qkvz]
        in_specs += [pl.BlockSpec((rows, LANES), lambda b, h, i: (b * nq + jnp.minimum(i + 1, nq - 1), h))]
    args += [qkvz, qkvz]
    in_specs += [
        pl.BlockSpec((seq, LANES), lambda b, h, i: (b, heads + h)),
        pl.BlockSpec((seq, LANES), lambda b, h, i: (b, 2 * heads + h)),
    ]
    if has_ctx:
        args += [ctx_qkvz, ctx_qkvz]
        in_specs += [
            pl.BlockSpec((ctx_len, LANES), lambda b, h, i: (b, heads + h)),
            pl.BlockSpec((ctx_len, LANES), lambda b, h, i: (b, 2 * heads + h)),
        ]
    args += [qkvz, sub_norm.reshape(1, LANES)]
    in_specs += [
        pl.BlockSpec((rows, LANES), lambda b, h, i: (b * nq + i, 3 * heads + h)),
        pl.BlockSpec((1, LANES), lambda b, h, i: (0, 0)),
    ]
    scratch = [pltpu.VMEM((LANES, seq), BF16)] + ([pltpu.VMEM((LANES, ctx_len), BF16)] if has_ctx else [])
    for _ in range(2 if tiles > 1 else 1):
        scratch += [pltpu.VMEM((tiles, seq // tk, tk, 2 * tq), F32)]
        if has_ctx:
            scratch += [pltpu.VMEM((tiles, ctx_len, 2 * tq), F32)]
        scratch += [pltpu.VMEM((tiles, SUBLANES, 2 * tq), F32)]
    return pl.pallas_call(
        functools.partial(_diff_attn_kernel, tq=tq, tk=tk, n_chunks=seq // tk, has_ctx=has_ctx, tiles=tiles,
                          lam_init=lam_init),
        grid=(batch, heads, nq),
        in_specs=in_specs,
        out_specs=pl.BlockSpec((rows, LANES), lambda b, h, i: (b * nq + i, h)),
        out_shape=jax.ShapeDtypeStruct((batch * seq, di), BF16),
        scratch_shapes=scratch,
        compiler_params=_params("parallel", "parallel", "arbitrary"),
        name="diff_attention",
    )(*args)


def _win_attn_kernel(*refs, group, kv_heads, n_z):
    q_ref, kp_ref, kc_ref, kn_ref, vp_ref, vc_ref, vn_ref, kx_ref, vx_ref, sink_ref = refs[:10]
    z_refs = refs[10:10 + n_z]
    o_ref = refs[10 + n_z]
    i = pl.program_id(1)
    blk = ATTN_BLOCK
    rows = group * blk
    zw = z_refs[0].shape[1]
    qpos = lax.broadcasted_iota(jnp.int32, (rows, blk), 0) % blk
    kpos = lax.broadcasted_iota(jnp.int32, (rows, blk), 1)
    keep_prev = (kpos >= qpos) & (i > 0)
    keep_next = (kpos <= qpos) & (i < pl.num_programs(1) - 1)
    sink2 = sink_ref[...] * math.log2(math.e)
    for n in range(kv_heads):
        ks = slice(n * LANES, (n + 1) * LANES)
        q4 = jnp.concatenate([q_ref[:, (n * group + g) * LANES:(n * group + g + 1) * LANES] for g in range(group)],
                             axis=0)
        k_all = jnp.concatenate([kp_ref[:, ks], kc_ref[:, ks], kn_ref[:, ks], kx_ref[:, ks]], axis=0)
        v_all = jnp.concatenate([vp_ref[:, ks], vc_ref[:, ks], vn_ref[:, ks], vx_ref[:, ks]], axis=0)
        s = _lane_chunks(_dot_nt(q4, k_all))
        s[0] = jnp.where(keep_prev, s[0], NEG_INF)
        s[2] = jnp.where(keep_next, s[2], NEG_INF)
        sink_b = jnp.concatenate([jnp.broadcast_to(sink2[n, g:g + 1, :], (blk, LANES)) for g in range(group)], axis=0)
        mx = sink_b
        for t in s:
            mx = jnp.maximum(mx, t)
        m = jnp.broadcast_to(jnp.max(mx, axis=-1, keepdims=True), (rows, LANES))
        ps = [jnp.exp2(t - m) for t in s]
        lsum = ps[0]
        for t in ps[1:]:
            lsum = lsum + t
        l = jnp.sum(lsum, axis=-1, keepdims=True) + jnp.exp2(sink_b - m)[:, 0:1]
        o = _dot(jnp.concatenate(ps, axis=1).astype(BF16), v_all) / l
        for g in range(group):
            col = (n * group + g) * LANES
            z = z_refs[col // zw][:, col % zw:col % zw + LANES].astype(F32)
            o_ref[:, col:col + LANES] = (o[g * blk:(g + 1) * blk] * jax.nn.silu(z)).astype(BF16)


def _win_attention(qkvz, seq, batch, ctx_qkvz, ctx_len, di, kv_heads, sink):
    heads = di // LANES
    group = heads // kv_heads
    kvw = kv_heads * LANES
    blk = ATTN_BLOCK
    nb = seq // blk
    z_col0 = di + 2 * kvw
    zw = math.gcd(z_col0, di)
    n_z = di // zw
    assert di % kvw == 0
    sink_rep = jnp.broadcast_to(sink.astype(F32).reshape(kv_heads, group, 1), (kv_heads, group, LANES))
    band = lambda col_block, shift: pl.BlockSpec(
        (blk, kvw), lambda b, i: (b * nb + jnp.clip(i + shift, 0, nb - 1), col_block))
    k_blk, v_blk = di // kvw, di // kvw + 1
    z_spec = lambda t: pl.BlockSpec((blk, zw), lambda b, i: (b * nb + i, z_col0 // zw + t))
    return pl.pallas_call(
        functools.partial(_win_attn_kernel, group=group, kv_heads=kv_heads, n_z=n_z),
        grid=(batch, nb),
        in_specs=[
            pl.BlockSpec((blk, di), lambda b, i: (b * nb + i, 0)),
            band(k_blk, -1), band(k_blk, 0), band(k_blk, 1),
            band(v_blk, -1), band(v_blk, 0), band(v_blk, 1),
            pl.BlockSpec((ctx_len, kvw), lambda b, i: (b, k_blk)),
            pl.BlockSpec((ctx_len, kvw), lambda b, i: (b, v_blk)),
            pl.BlockSpec((kv_heads, group, LANES), lambda b, i: (0, 0, 0)),
        ] + [z_spec(t) for t in range(n_z)],
        out_specs=pl.BlockSpec((blk, di), lambda b, i: (b * nb + i, 0)),
        out_shape=jax.ShapeDtypeStruct((batch * seq, di), BF16),
        compiler_params=_params("parallel", "arbitrary"),
        name="win_attention",
    )(*([qkvz] * 7 + [ctx_qkvz, ctx_qkvz, sink_rep] + [qkvz] * n_z))


class _Stream(NamedTuple):
    seq: int
    span: int
    mod_row: Callable


def kernel(x, c, ctx, c_ctx, l0_norm, l0_w_mod, l0_b_mod, l0_w_in, l0_conv_w, l0_conv_b, l0_w_out, l1_norm, l1_w_mod, l1_b_mod, l1_w_in, l1_q_norm, l1_k_norm, l1_lam_q1, l1_lam_k1, l1_lam_q2, l1_lam_k2, l1_sub_norm, l1_w_out, l2_norm, l2_w_mod, l2_b_mod, l2_w_in, l2_q_norm, l2_k_norm, l2_sink, l2_w_out, l3_norm, l3_w_mod, l3_b_mod, l3_w_in, l3_conv_w, l3_conv_b, l3_w_out):
    batch, seq, d = x.shape
    ctx_len = ctx.shape[1]
    di = l0_w_out.shape[0]
    assert batch < MOD_ROWS
    xs = _Stream(seq, seq, lambda row: row // seq)
    cs = _Stream(ctx_len, batch * ctx_len, lambda row: batch)

    x2 = x.reshape(batch * seq, d)
    c2 = ctx.reshape(batch * ctx_len, d)
    cc = jnp.zeros((MOD_ROWS, d), F32).at[:batch].set(c).at[batch].set(c_ctx)
    mod0, mod1, mod2, mod3 = (_modulation(cc, w, b) for w, b in
                              ((l0_w_mod, l0_b_mod), (l1_w_mod, l1_b_mod), (l2_w_mod, l2_b_mod), (l3_w_mod, l3_b_mod)))
    bf = lambda w: w.astype(BF16)

    w_in, w_out = bf(l0_w_in), bf(l0_w_out)
    x2n = _conv_layer(x2, xs, mod0, l0_norm, w_in, l0_conv_w, l0_conv_b, w_out)
    c2 = _conv_layer(c2, cs, mod0, l0_norm, w_in, l0_conv_w, l0_conv_b, w_out)
    x2 = x2n

    w_in, w_out = bf(l1_w_in), bf(l1_w_out)
    lam_init = 0.8 - 0.6 * math.exp(-0.3 * 1)
    q_scale = l1_q_norm.shape[0] ** -0.5 * math.log2(math.e)
    proj = functools.partial(_proj_layer, norm_g=l1_norm, w_in=w_in, q_norm=l1_q_norm, k_norm=l1_k_norm,
                             q_cols=di, k_cols=di, q_scale=q_scale)
    qx = proj(x2, xs, mod1, rope=True)
    qc = proj(c2, cs, mod1, rope=False)
    lam_vecs = jnp.stack([l1_lam_q1, l1_lam_k1, l1_lam_q2, l1_lam_k2]).astype(F32)
    ax = _diff_attention(qx, seq, batch, qc, ctx_len, lam_vecs, l1_sub_norm, lam_init)
    ac = _diff_attention(qc, ctx_len, batch, None, 0, lam_vecs, l1_sub_norm, lam_init)
    x2 = _out_layer(ax, x2, xs, mod1, w_out)
    c2 = _out_layer(ac, c2, cs, mod1, w_out)

    w_in, w_out = bf(l2_w_in), bf(l2_w_out)
    kv_cols = (l2_w_in.shape[1] - 2 * di) // 2
    proj = functools.partial(_proj_layer, norm_g=l2_norm, w_in=w_in, q_norm=l2_q_norm, k_norm=l2_k_norm,
                             q_cols=di, k_cols=kv_cols, q_scale=l2_q_norm.shape[0] ** -0.5 * math.log2(math.e))
    qx = proj(x2, xs, mod2, rope=True)
    qc = proj(c2, cs, mod2, rope=False)
    ax = _win_attention(qx, seq, batch, qc, ctx_len, di, kv_cols // LANES, l2_sink)
    x2 = _out_layer(ax, x2, xs, mod2, w_out)

    x2 = _conv_layer(x2, xs, mod3, l3_norm, bf(l3_w_in), l3_conv_w, l3_conv_b, bf(l3_w_out))
    return x2.reshape(batch, seq, d)
```

```python
import functools
import math
from typing import Callable, NamedTuple

import jax
import jax.numpy as jnp
from jax import lax
from jax.experimental import pallas as pl
from jax.experimental.pallas import tpu as pltpu

LANES = 128
SUBLANES = 8
MXU_COLS = 256
BF16_ROWS = 16
GRID_W = 64
ROPE_BASE = 10000.0
EPS = 1e-6
NEG_INF = -1e30
ATTN_BLOCK = 128
EPILOGUE_ROWS = 256
ROW_TILE = 512
MOD_ROWS = 8
VMEM_LIMIT = 56 * 1024 * 1024

F32 = jnp.float32
BF16 = jnp.bfloat16


def _params(*sem):
    return pltpu.CompilerParams(dimension_semantics=sem, vmem_limit_bytes=VMEM_LIMIT)


def _tile(n, target):
    if n <= target:
        return n
    t = target - target % LANES
    while n % t:
        t -= LANES
    return t


def _dot(a, b):
    return jnp.dot(a, b, preferred_element_type=F32)


def _dot_nt(a, b):
    return lax.dot_general(a, b, (((1,), (1,)), ((), ())), preferred_element_type=F32)


def _norm_mod(t, g, shift, scale):
    ms = jnp.mean(t * t, axis=-1, keepdims=True)
    y = t * lax.rsqrt(ms + EPS) * g
    return (y * (1.0 + scale) + shift).astype(BF16)


def _mod_kernel(c_ref, w_ref, b_ref, o_ref):
    a = jax.nn.silu(c_ref[...]).astype(BF16)
    o_ref[...] = _dot(a, w_ref[...].astype(BF16)) + b_ref[...]


def _modulation(cc, w_mod, b_mod):
    d, n = w_mod.shape
    tn = _tile(n, 768)
    out = pl.pallas_call(
        _mod_kernel,
        grid=(n // tn,),
        in_specs=[
            pl.BlockSpec((MOD_ROWS, d), lambda j: (0, 0)),
            pl.BlockSpec((d, tn), lambda j: (0, j)),
            pl.BlockSpec((1, tn), lambda j: (0, j)),
        ],
        out_specs=pl.BlockSpec((MOD_ROWS, tn), lambda j: (0, j)),
        out_shape=jax.ShapeDtypeStruct((MOD_ROWS, n), F32),
        compiler_params=_params("arbitrary"),
        name="modulation",
    )(cc, w_mod, b_mod.reshape(1, n))
    return out.reshape(MOD_ROWS, 1, n)


def _conv_kernel(xp_ref, x_ref, xn_ref, mod_ref, g_ref, wb_ref, wc_ref, wx_ref, wz_ref, cw_ref, cb_ref,
                 wo_ref, o_ref, h_scr, *, tm, d, seq):
    i = pl.program_id(0)
    j = pl.program_id(1)
    halo = BF16_ROWS
    mod = mod_ref[0]
    shift, scale, gate = mod[:, :d], mod[:, d:2 * d], mod[:, 2 * d:]

    @pl.when(j == 0)
    def _():
        g = g_ref[...]
        h_scr[0:halo, :] = _norm_mod(xp_ref[...], g, shift, scale)
        h_scr[halo:halo + tm, :] = _norm_mod(x_ref[...], g, shift, scale)
        h_scr[halo + tm:, :] = _norm_mod(xn_ref[...], g, shift, scale)
        o_ref[...] = jnp.zeros_like(o_ref)

    n = tm + 2 * halo
    h = h_scr[...]
    hm = h_scr[halo:halo + tm, :]
    u = _dot(h, wc_ref[...]) * _dot(h, wx_ref[...])
    pos = lax.rem(i * tm + lax.broadcasted_iota(jnp.int32, (tm, 1), 0), seq)
    u_prev = jnp.where(pos == 0, 0.0, pltpu.roll(u, 1, 0)[halo:halo + tm])
    u_next = jnp.where(pos == seq - 1, 0.0, pltpu.roll(u, n - 1, 0)[halo:halo + tm])
    cw = cw_ref[...]
    y = u_prev * cw[0:1] + u[halo:halo + tm] * cw[1:2] + u_next * cw[2:3] + cb_ref[...]
    gated = _dot(hm, wb_ref[...]) * y * jax.nn.silu(_dot(hm, wz_ref[...]))
    o_ref[...] += _dot(gated.astype(BF16), wo_ref[...])

    @pl.when(j == pl.num_programs(1) - 1)
    def _():
        o_ref[...] = x_ref[...] + gate * o_ref[...]


def _conv_layer(x2, stream, mod, norm_g, w_in, conv_w, conv_b, w_out):
    m, d = x2.shape
    di = w_out.shape[0]
    tm = min(stream.span, ROW_TILE)
    tn = _tile(di, 512)
    nj = di // tn
    hb = tm // BF16_ROWS
    last_hb = m // BF16_ROWS - 1
    w_spec = lambda c: pl.BlockSpec((d, tn), lambda i, j: (0, c * nj + j))
    return pl.pallas_call(
        functools.partial(_conv_kernel, tm=tm, d=d, seq=stream.seq),
        grid=(m // tm, nj),
        in_specs=[
            pl.BlockSpec((BF16_ROWS, d), lambda i, j: (jnp.maximum(i * hb - 1, 0), 0)),
            pl.BlockSpec((tm, d), lambda i, j: (i, 0)),
            pl.BlockSpec((BF16_ROWS, d), lambda i, j: (jnp.minimum((i + 1) * hb, last_hb), 0)),
            pl.BlockSpec((1, 1, 3 * d), lambda i, j: (stream.mod_row(i * tm), 0, 0)),
            pl.BlockSpec((1, d), lambda i, j: (0, 0)),
            w_spec(0), w_spec(1), w_spec(2), w_spec(3),
            pl.BlockSpec((3, tn), lambda i, j: (0, j)),
            pl.BlockSpec((1, tn), lambda i, j: (0, j)),
            pl.BlockSpec((tn, d), lambda i, j: (j, 0)),
        ],
        out_specs=pl.BlockSpec((tm, d), lambda i, j: (i, 0)),
        out_shape=jax.ShapeDtypeStruct((m, d), F32),
        scratch_shapes=[pltpu.VMEM((tm + 2 * BF16_ROWS, d), BF16)],
        compiler_params=_params("parallel", "arbitrary"),
        name="conv_layer",
    )(x2, x2, x2, mod, norm_g.reshape(1, d), w_in, w_in, w_in, w_in, conv_w, conv_b.reshape(1, di), w_out)


def _head_norm_rope(acc, gn, cos, sin, group, out_scale):
    outs = []
    swap = group // 4
    for c in range(acc.shape[1] // LANES):
        t = acc[:, c * LANES:(c + 1) * LANES]
        lane = lax.broadcasted_iota(jnp.int32, t.shape, 1)
        sq = t * t
        if group == LANES:
            ms = jnp.mean(sq, axis=-1, keepdims=True)
        else:
            lo = lane < group
            s_lo = jnp.sum(jnp.where(lo, sq, 0.0), axis=-1, keepdims=True)
            s_hi = jnp.sum(jnp.where(lo, 0.0, sq), axis=-1, keepdims=True)
            ms = jnp.where(lo, s_lo, s_hi) * (1.0 / group)
        y = t * lax.rsqrt(ms + EPS) * gn
        if cos is not None:
            ahead = pltpu.roll(y, LANES - swap, 1)
            behind = pltpu.roll(y, swap, 1)
            y = y * cos + jnp.where((lane & swap) == 0, ahead, behind) * sin
        if out_scale != 1.0:
            y = y * out_scale
        outs.append(y)
    return outs[0] if len(outs) == 1 else jnp.concatenate(outs, axis=1)


def _proj_kernel(*refs, d, q_tiles, k_tiles, group, rope, q_scale):
    if rope:
        x_ref, mod_ref, g_ref, w_ref, qn_ref, kn_ref, cos_ref, sin_ref, o_ref, h_scr = refs
    else:
        x_ref, mod_ref, g_ref, w_ref, qn_ref, kn_ref, o_ref, h_scr = refs
    j = pl.program_id(1)

    @pl.when(j == 0)
    def _():
        mod = mod_ref[0]
        h_scr[...] = _norm_mod(x_ref[...], g_ref[...], mod[:, :d], mod[:, d:2 * d])

    tm, tn = o_ref.shape
    cw = min(tn, MXU_COLS)

    def units(rh, epilogue):
        for r in range(tm // rh):
            rows = slice(r * rh, (r + 1) * rh)
            h = h_scr[rows, :]
            for c in range(tn // cw):
                cols = slice(c * cw, (c + 1) * cw)
                o_ref[rows, cols] = epilogue(_dot(h, w_ref[:, cols]), rows).astype(BF16)

    def head_epilogue(gn_ref, scale):
        def epilogue(acc, rows):
            cos = cos_ref[rows, :] if rope else None
            sin = sin_ref[rows, :] if rope else None
            return _head_norm_rope(acc, gn_ref[...], cos, sin, group, scale)
        return epilogue

    @pl.when(j < q_tiles)
    def _():
        units(min(tm, EPILOGUE_ROWS), head_epilogue(qn_ref, q_scale))

    @pl.when((j >= q_tiles) & (j < q_tiles + k_tiles))
    def _():
        units(min(tm, EPILOGUE_ROWS), head_epilogue(kn_ref, 1.0))

    @pl.when(j >= q_tiles + k_tiles)
    def _():
        units(tm, lambda acc, rows: acc)


def _rope_tables(seq, group):
    rows = seq // GRID_W
    row = jnp.repeat(jnp.arange(rows), GRID_W).astype(F32)
    col = jnp.tile(jnp.arange(GRID_W), rows).astype(F32)
    n_freq = group // 4
    inv_freq = ROPE_BASE ** (-(jnp.arange(n_freq, dtype=F32) / n_freq))
    ar, ac = row[:, None] * inv_freq, col[:, None] * inv_freq
    cos = jnp.concatenate([jnp.cos(ar), jnp.cos(ar), jnp.cos(ac), jnp.cos(ac)], axis=-1)
    sin = jnp.concatenate([-jnp.sin(ar), jnp.sin(ar), -jnp.sin(ac), jnp.sin(ac)], axis=-1)
    reps = LANES // group
    return jnp.tile(cos, (1, reps)), jnp.tile(sin, (1, reps))


def _proj_layer(x2, stream, mod, norm_g, w_in, q_norm, k_norm, q_cols, k_cols, rope, q_scale):
    m, d = x2.shape
    n = w_in.shape[1]
    group = q_norm.shape[0]
    tm = min(stream.span, ROW_TILE)
    tn = math.gcd(math.gcd(q_cols, k_cols), 1024)
    reps = LANES // group
    args = [x2, mod, norm_g.reshape(1, d), w_in,
            jnp.tile(q_norm, reps).reshape(1, LANES), jnp.tile(k_norm, reps).reshape(1, LANES)]
    in_specs = [
        pl.BlockSpec((tm, d), lambda i, j: (i, 0)),
        pl.BlockSpec((1, 1, 3 * d), lambda i, j: (stream.mod_row(i * tm), 0, 0)),
        pl.BlockSpec((1, d), lambda i, j: (0, 0)),
        pl.BlockSpec((d, tn), lambda i, j: (0, j)),
        pl.BlockSpec((1, LANES), lambda i, j: (0, 0)),
        pl.BlockSpec((1, LANES), lambda i, j: (0, 0)),
    ]
    if rope:
        tiles_per_seq = stream.seq // tm
        args += list(_rope_tables(stream.seq, group))
        in_specs += [pl.BlockSpec((tm, LANES), lambda i, j: (i % tiles_per_seq, 0))] * 2
    return pl.pallas_call(
        functools.partial(_proj_kernel, d=d, q_tiles=q_cols // tn, k_tiles=k_cols // tn, group=group,
                          rope=rope, q_scale=q_scale),
        grid=(m // tm, n // tn),
        in_specs=in_specs,
        out_specs=pl.BlockSpec((tm, tn), lambda i, j: (i, j)),
        out_shape=jax.ShapeDtypeStruct((m, n), BF16),
        scratch_shapes=[pltpu.VMEM((tm, d), BF16)],
        compiler_params=_params("parallel", "arbitrary"),
        name="proj_layer",
    )(*args)


def _out_kernel(a_ref, w_ref, x_ref, gate_ref, o_ref):
    o_ref[...] = x_ref[...] + gate_ref[0] * _dot(a_ref[...], w_ref[...])


def _out_layer(a2, x2, stream, mod, w_out):
    m, d = x2.shape
    di = a2.shape[1]
    tm = min(stream.span, ROW_TILE)
    tn = _tile(d, 1024)
    gate_block0 = 2 * d // tn
    return pl.pallas_call(
        _out_kernel,
        grid=(m // tm, d // tn),
        in_specs=[
            pl.BlockSpec((tm, di), lambda i, j: (i, 0)),
            pl.BlockSpec((di, tn), lambda i, j: (0, j)),
            pl.BlockSpec((tm, tn), lambda i, j: (i, j)),
            pl.BlockSpec((1, 1, tn), lambda i, j: (stream.mod_row(i * tm), 0, gate_block0 + j)),
        ],
        out_specs=pl.BlockSpec((tm, tn), lambda i, j: (i, j)),
        out_shape=jax.ShapeDtypeStruct((m, d), F32),
        compiler_params=_params("parallel", "arbitrary"),
        name="out_layer",
    )(a2, w_out, x2, mod)


def _lane_chunks(t):
    return [t[:, c * LANES:(c + 1) * LANES] for c in range(t.shape[1] // LANES)]


def _diff_attn_kernel(*refs, tq, tk, n_chunks, has_ctx, tiles, lam_init):
    pipelined = tiles > 1
    refs = iter(refs)
    lam_ref, q_ref = next(refs), next(refs)
    qn_ref = kn_ref = kcn_ref = kc_ref = vc_ref = vct_scr = None
    if pipelined:
        qn_ref, kn_ref = next(refs), next(refs)
        if has_ctx:
            kcn_ref = next(refs)
    k_ref, v_ref = next(refs), next(refs)
    if has_ctx:
        kc_ref, vc_ref = next(refs), next(refs)
    z_ref, sn_ref, o_ref, vt_scr = next(refs), next(refs), next(refs), next(refs)
    if has_ctx:
        vct_scr = next(refs)
    bufs = list(refs)
    if has_ctx:
        sets = [tuple(bufs[3 * p:3 * p + 3]) for p in range(len(bufs) // 3)]
    else:
        sets = [(bufs[2 * p], None, bufs[2 * p + 1]) for p in range(len(bufs) // 2)]
    half = LANES // 2

    def tile_rows(t):
        return pl.ds(pl.multiple_of(t * tq, tq), tq)

    def transpose_values():
        for c in range(n_chunks):
            vt_scr[:, c * tk:(c + 1) * tk] = v_ref[c * tk:(c + 1) * tk, :].astype(F32).T.astype(BF16)
        if has_ctx:
            vct_scr[...] = vc_ref[...].astype(F32).T.astype(BF16)

    def sublane_groups(s):
        return s.reshape(s.shape[0] // SUBLANES, SUBLANES, s.shape[1])

    def scores(q, buf, t):
        s_buf, c_buf, m_buf = buf
        lane = lax.broadcasted_iota(jnp.int32, q.shape, 1)
        zero = jnp.zeros_like(q)
        subs = (jnp.where(lane < half, q, zero), jnp.where(lane < half, zero, q))
        for h, qh in enumerate(subs):
            cols = slice(h * tq, (h + 1) * tq)
            mx = jnp.full((SUBLANES, tq), NEG_INF, F32)
            for c in range(n_chunks):
                s = _dot_nt(k_ref[c * tk:(c + 1) * tk, :], qh)
                s_buf[t, c, :, cols] = s
                mx = jnp.maximum(mx, jnp.max(sublane_groups(s), axis=0))
            if has_ctx:
                s = _dot_nt(kc_ref[...], qh)
                c_buf[t, :, cols] = s
                mx = jnp.maximum(mx, jnp.max(sublane_groups(s), axis=0))
            m_buf[t, :, cols] = jnp.broadcast_to(jnp.max(mx, axis=0, keepdims=True), (SUBLANES, tq))

    def weighted(buf, t):
        s_buf, c_buf, m_buf = buf
        m = m_buf[t]
        lsum = jnp.zeros((SUBLANES, 2 * tq), F32)
        acc = jnp.zeros((LANES, 2 * tq), F32)
        blocks = [(s_buf[t, c], vt_scr[:, c * tk:(c + 1) * tk]) for c in range(n_chunks)]
        if has_ctx:
            blocks.append((c_buf[t], vct_scr[...]))
        for s, vt in blocks:
            p = jnp.exp2(sublane_groups(s) - m[None])
            lsum = lsum + jnp.sum(p, axis=0)
            acc = acc + _dot(vt, p.reshape(s.shape).astype(BF16))
        return acc / jnp.sum(lsum, axis=0, keepdims=True)

    lv = lam_ref[...]
    lam = (jnp.exp(jnp.sum(lv[0:1] * lv[1:2], axis=-1, keepdims=True))
           - jnp.exp(jnp.sum(lv[2:3] * lv[3:4], axis=-1, keepdims=True)) + lam_init)

    def finish(o_all, z):
        o = (o_all[:, :tq] - lam * o_all[:, tq:]).T
        ms = jnp.mean(o * o, axis=-1, keepdims=True)
        y = o * lax.rsqrt(ms + EPS) * sn_ref[...] * (1.0 - lam_init)
        return (y * jax.nn.silu(z.astype(F32))).astype(BF16)

    if not pipelined:
        transpose_values()
        scores(q_ref[...], sets[0], 0)
        o_ref[...] = finish(weighted(sets[0], 0), z_ref[...])
        return

    i = pl.program_id(2)
    step_index = (pl.program_id(0) * pl.num_programs(1) + pl.program_id(1)) * pl.num_programs(2) + i

    @pl.when(i == 0)
    def _():
        transpose_values()

    @pl.when(step_index == 0)
    def _():
        @pl.loop(0, tiles)
        def _(t):
            scores(q_ref[tile_rows(t), :], sets[0], t)

    def both(q, cur, nxt, t):
        s_nxt, c_nxt, m_nxt = nxt
        s_cur, c_cur, m_cur = cur
        lane = lax.broadcasted_iota(jnp.int32, q.shape, 1)
        zero = jnp.zeros_like(q)
        subs = (jnp.where(lane < half, q, zero), jnp.where(lane < half, zero, q))
        mxs = [jnp.full((SUBLANES, tq), NEG_INF, F32) for _ in subs]
        m = m_cur[t]
        lsum = jnp.zeros((SUBLANES, 2 * tq), F32)
        acc = jnp.zeros((LANES, 2 * tq), F32)
        tie = None
        for c in range(n_chunks + has_ctx):
            if c < n_chunks:
                k, vt, s_old = kn_ref[c * tk:(c + 1) * tk, :], vt_scr[:, c * tk:(c + 1) * tk], s_cur[t, c]
            else:
                k, vt, s_old = kcn_ref[...], vct_scr[...], c_cur[t]
            if tie is not None:
                k = k + jnp.tile(tie, (k.shape[0] // SUBLANES, 1)).astype(BF16)
            for h, qh in enumerate(subs):
                cols = slice(h * tq, (h + 1) * tq)
                s = _dot_nt(k, qh)
                if c < n_chunks:
                    s_nxt[t, c, :, cols] = s
                else:
                    c_nxt[t, :, cols] = s
                mxs[h] = jnp.maximum(mxs[h], jnp.max(sublane_groups(s), axis=0))
            p = jnp.exp2(sublane_groups(s_old) - m[None])
            lsum = lsum + jnp.sum(p, axis=0)
            acc = acc + _dot(vt, p.reshape(s_old.shape).astype(BF16))
            bits = pltpu.bitcast(p[-1][:, :LANES], jnp.uint32)
            tie = pltpu.bitcast((bits >> 16) >> 16, F32)
        for h, mx in enumerate(mxs):
            m_nxt[t, :, h * tq:(h + 1) * tq] = jnp.broadcast_to(jnp.max(mx, axis=0, keepdims=True), (SUBLANES, tq))
        return acc / jnp.sum(lsum, axis=0, keepdims=True)

    def step(cur, nxt):
        for t in range(tiles):
            rows = slice(t * tq, (t + 1) * tq)
            o_ref[rows, :] = finish(both(qn_ref[rows, :], cur, nxt, t), z_ref[rows, :])

    @pl.when(step_index % 2 == 0)
    def _():
        step(sets[0], sets[1])

    @pl.when(step_index % 2 == 1)
    def _():
        step(sets[1], sets[0])


def _diff_attention(qkvz, seq, batch, ctx_qkvz, ctx_len, lam_vecs, sub_norm, lam_init):
    di = qkvz.shape[1] // 4
    heads = di // LANES
    tq = min(seq, 256)
    tk = min(seq, 512)
    tiles = 2 if seq >= 2 * tq else 1
    rows = tiles * tq
    nq = seq // rows
    has_ctx = ctx_qkvz is not None
    args = [lam_vecs, qkvz]
    in_specs = [
        pl.BlockSpec(lam_vecs.shape, lambda b, h, i: (0, 0)),
        pl.BlockSpec((rows, LANES), lambda b, h, i: (b * nq + i, h)),
    ]
    pipelined = tiles > 1

    def successor(b, h, i):
        i1 = i + 1
        h1 = h + i1 // nq
        b1 = jnp.minimum(b + h1 // heads, batch - 1)
        return b1, h1 % heads, i1 % nq

    def next_spec(shape, index):
        return pl.BlockSpec(shape, lambda b, h, i: index(*successor(b, h, i)))

    if pipelined:
        own_k = lambda b, h, i: (0, heads)
        args += [qkvz, qkvz]
        in_specs += [next_spec((rows, LANES), lambda b, h, i: (b * nq + i, h)),
                     next_spec((seq, LANES), lambda b, h, i: (b, heads + h))]
        if has_ctx:
            args += [ctx_qkvz]
            in_specs += [next_spec((ctx_len, LANES), lambda b, h, i: (b, heads + h))]
    else:
        own_k = lambda b, h, i: (b, heads + h)
    args += [qkvz, qkvz]
    in_specs += [
        pl.BlockSpec((seq, LANES), own_k),
        pl.BlockSpec((seq, LANES), lambda b, h, i: (b, 2 * heads + h)),
    ]
    if has_ctx:
        args += [ctx_qkvz, ctx_qkvz]
        in_specs += [
            pl.BlockSpec((ctx_len, LANES), own_k),
            pl.BlockSpec((ctx_len, LANES), lambda b, h, i: (b, 2 * heads + h)),
        ]
    args += [qkvz, sub_norm.reshape(1, LANES)]
    in_specs += [
        pl.BlockSpec((rows, LANES), lambda b, h, i: (b * nq + i, 3 * heads + h)),
        pl.BlockSpec((1, LANES), lambda b, h, i: (0, 0)),
    ]
    scratch = [pltpu.VMEM((LANES, seq), BF16)] + ([pltpu.VMEM((LANES, ctx_len), BF16)] if has_ctx else [])
    for _ in range(2 if tiles > 1 else 1):
        scratch += [pltpu.VMEM((tiles, seq // tk, tk, 2 * tq), F32)]
        if has_ctx:
            scratch += [pltpu.VMEM((tiles, ctx_len, 2 * tq), F32)]
        scratch += [pltpu.VMEM((tiles, SUBLANES, 2 * tq), F32)]
    return pl.pallas_call(
        functools.partial(_diff_attn_kernel, tq=tq, tk=tk, n_chunks=seq // tk, has_ctx=has_ctx, tiles=tiles,
                          lam_init=lam_init),
        grid=(batch, heads, nq),
        in_specs=in_specs,
        out_specs=pl.BlockSpec((rows, LANES), lambda b, h, i: (b * nq + i, h)),
        out_shape=jax.ShapeDtypeStruct((batch * seq, di), BF16),
        scratch_shapes=scratch,
        compiler_params=_params("arbitrary", "arbitrary", "arbitrary"),
        name="diff_attention",
    )(*args)


def _win_attn_kernel(*refs, group, kv_heads, n_z):
    q_ref, kp_ref, kc_ref, kn_ref, vp_ref, vc_ref, vn_ref, kx_ref, vx_ref, sink_ref = refs[:10]
    z_refs = refs[10:10 + n_z]
    o_ref = refs[10 + n_z]
    i = pl.program_id(1)
    blk = ATTN_BLOCK
    rows = group * blk
    zw = z_refs[0].shape[1]
    qpos = lax.broadcasted_iota(jnp.int32, (rows, blk), 0) % blk
    kpos = lax.broadcasted_iota(jnp.int32, (rows, blk), 1)
    keep_prev = (kpos >= qpos) & (i > 0)
    keep_next = (kpos <= qpos) & (i < pl.num_programs(1) - 1)
    sink2 = sink_ref[...] * math.log2(math.e)
    for n in range(kv_heads):
        ks = slice(n * LANES, (n + 1) * LANES)
        q4 = jnp.concatenate([q_ref[:, (n * group + g) * LANES:(n * group + g + 1) * LANES] for g in range(group)],
                             axis=0)
        k_all = jnp.concatenate([kp_ref[:, ks], kc_ref[:, ks], kn_ref[:, ks], kx_ref[:, ks]], axis=0)
        v_all = jnp.concatenate([vp_ref[:, ks], vc_ref[:, ks], vn_ref[:, ks], vx_ref[:, ks]], axis=0)
        s = _lane_chunks(_dot_nt(q4, k_all))
        s[0] = jnp.where(keep_prev, s[0], NEG_INF)
        s[2] = jnp.where(keep_next, s[2], NEG_INF)
        sink_b = jnp.concatenate([jnp.broadcast_to(sink2[n, g:g + 1, :], (blk, LANES)) for g in range(group)], axis=0)
        mx = sink_b
        for t in s:
            mx = jnp.maximum(mx, t)
        m = jnp.broadcast_to(jnp.max(mx, axis=-1, keepdims=True), (rows, LANES))
        ps = [jnp.exp2(t - m) for t in s]
        lsum = ps[0]
        for t in ps[1:]:
            lsum = lsum + t
        l = jnp.sum(lsum, axis=-1, keepdims=True) + jnp.exp2(sink_b - m)[:, 0:1]
        o = _dot(jnp.concatenate(ps, axis=1).astype(BF16), v_all) / l
        for g in range(group):
            col = (n * group + g) * LANES
            z = z_refs[col // zw][:, col % zw:col % zw + LANES].astype(F32)
            o_ref[:, col:col + LANES] = (o[g * blk:(g + 1) * blk] * jax.nn.silu(z)).astype(BF16)


def _win_attention(qkvz, seq, batch, ctx_qkvz, ctx_len, di, kv_heads, sink):
    heads = di // LANES
    group = heads // kv_heads
    kvw = kv_heads * LANES
    blk = ATTN_BLOCK
    nb = seq // blk
    z_col0 = di + 2 * kvw
    zw = math.gcd(z_col0, di)
    n_z = di // zw
    assert di % kvw == 0
    sink_rep = jnp.broadcast_to(sink.astype(F32).reshape(kv_heads, group, 1), (kv_heads, group, LANES))
    band = lambda col_block, shift: pl.BlockSpec(
        (blk, kvw), lambda b, i: (b * nb + jnp.clip(i + shift, 0, nb - 1), col_block))
    k_blk, v_blk = di // kvw, di // kvw + 1
    z_spec = lambda t: pl.BlockSpec((blk, zw), lambda b, i: (b * nb + i, z_col0 // zw + t))
    return pl.pallas_call(
        functools.partial(_win_attn_kernel, group=group, kv_heads=kv_heads, n_z=n_z),
        grid=(batch, nb),
        in_specs=[
            pl.BlockSpec((blk, di), lambda b, i: (b * nb + i, 0)),
            band(k_blk, -1), band(k_blk, 0), band(k_blk, 1),
            band(v_blk, -1), band(v_blk, 0), band(v_blk, 1),
            pl.BlockSpec((ctx_len, kvw), lambda b, i: (b, k_blk)),
            pl.BlockSpec((ctx_len, kvw), lambda b, i: (b, v_blk)),
            pl.BlockSpec((kv_heads, group, LANES), lambda b, i: (0, 0, 0)),
        ] + [z_spec(t) for t in range(n_z)],
        out_specs=pl.BlockSpec((blk, di), lambda b, i: (b * nb + i, 0)),
        out_shape=jax.ShapeDtypeStruct((batch * seq, di), BF16),
        compiler_params=_params("parallel", "arbitrary"),
        name="win_attention",
    )(*([qkvz] * 7 + [ctx_qkvz, ctx_qkvz, sink_rep] + [qkvz] * n_z))


class _Stream(NamedTuple):
    seq: int
    span: int
    mod_row: Callable


def kernel(x, c, ctx, c_ctx, l0_norm, l0_w_mod, l0_b_mod, l0_w_in, l0_conv_w, l0_conv_b, l0_w_out, l1_norm, l1_w_mod, l1_b_mod, l1_w_in, l1_q_norm, l1_k_norm, l1_lam_q1, l1_lam_k1, l1_lam_q2, l1_lam_k2, l1_sub_norm, l1_w_out, l2_norm, l2_w_mod, l2_b_mod, l2_w_in, l2_q_norm, l2_k_norm, l2_sink, l2_w_out, l3_norm, l3_w_mod, l3_b_mod, l3_w_in, l3_conv_w, l3_conv_b, l3_w_out):
    batch, seq, d = x.shape
    ctx_len = ctx.shape[1]
    di = l0_w_out.shape[0]
    assert batch < MOD_ROWS
    xs = _Stream(seq, seq, lambda row: row // seq)
    cs = _Stream(ctx_len, batch * ctx_len, lambda row: batch)

    x2 = x.reshape(batch * seq, d)
    c2 = ctx.reshape(batch * ctx_len, d)
    cc = jnp.zeros((MOD_ROWS, d), F32).at[:batch].set(c).at[batch].set(c_ctx)
    mod0, mod1, mod2, mod3 = (_modulation(cc, w, b) for w, b in
                              ((l0_w_mod, l0_b_mod), (l1_w_mod, l1_b_mod), (l2_w_mod, l2_b_mod), (l3_w_mod, l3_b_mod)))
    bf = lambda w: w.astype(BF16)

    w_in, w_out = bf(l0_w_in), bf(l0_w_out)
    x2n = _conv_layer(x2, xs, mod0, l0_norm, w_in, l0_conv_w, l0_conv_b, w_out)
    c2 = _conv_layer(c2, cs, mod0, l0_norm, w_in, l0_conv_w, l0_conv_b, w_out)
    x2 = x2n

    w_in, w_out = bf(l1_w_in), bf(l1_w_out)
    lam_init = 0.8 - 0.6 * math.exp(-0.3 * 1)
    q_scale = l1_q_norm.shape[0] ** -0.5 * math.log2(math.e)
    proj = functools.partial(_proj_layer, norm_g=l1_norm, w_in=w_in, q_norm=l1_q_norm, k_norm=l1_k_norm,
                             q_cols=di, k_cols=di, q_scale=q_scale)
    qx = proj(x2, xs, mod1, rope=True)
    qc = proj(c2, cs, mod1, rope=False)
    lam_vecs = jnp.stack([l1_lam_q1, l1_lam_k1, l1_lam_q2, l1_lam_k2]).astype(F32)
    ax = _diff_attention(qx, seq, batch, qc, ctx_len, lam_vecs, l1_sub_norm, lam_init)
    ac = _diff_attention(qc, ctx_len, batch, None, 0, lam_vecs, l1_sub_norm, lam_init)
    x2 = _out_layer(ax, x2, xs, mod1, w_out)
    c2 = _out_layer(ac, c2, cs, mod1, w_out)

    w_in, w_out = bf(l2_w_in), bf(l2_w_out)
    kv_cols = (l2_w_in.shape[1] - 2 * di) // 2
    proj = functools.partial(_proj_layer, norm_g=l2_norm, w_in=w_in, q_norm=l2_q_norm, k_norm=l2_k_norm,
                             q_cols=di, k_cols=kv_cols, q_scale=l2_q_norm.shape[0] ** -0.5 * math.log2(math.e))
    qx = proj(x2, xs, mod2, rope=True)
    qc = proj(c2, cs, mod2, rope=False)
    ax = _win_attention(qx, seq, batch, qc, ctx_len, di, kv_cols // LANES, l2_sink)
    x2 = _out_layer(ax, x2, xs, mod2, w_out)

    x2 = _conv_layer(x2, xs, mod3, l3_norm, bf(l3_w_in), l3_conv_w, l3_conv_b, bf(l3_w_out))
    return x2.reshape(batch, seq, d)
```

```python
import functools
import math
from typing import Callable, NamedTuple

import jax
import jax.numpy as jnp
from jax import lax
from jax.experimental import pallas as pl
from jax.experimental.pallas import tpu as pltpu

LANES = 128
SUBLANES = 8
MXU_COLS = 256
BF16_ROWS = 16
GRID_W = 64
ROPE_BASE = 10000.0
EPS = 1e-6
NEG_INF = -1e30
ATTN_BLOCK = 128
EPILOGUE_ROWS = 256
ROW_TILE = 512
WIDE_ROW_TILE = 1024
MOD_ROWS = 8
VMEM_LIMIT = 56 * 1024 * 1024

F32 = jnp.float32
BF16 = jnp.bfloat16


def _params(*sem):
    return pltpu.CompilerParams(dimension_semantics=sem, vmem_limit_bytes=VMEM_LIMIT)


def _tile(n, target):
    if n <= target:
        return n
    t = target - target % LANES
    while n % t:
        t -= LANES
    return t


def _dot(a, b):
    return jnp.dot(a, b, preferred_element_type=F32)


def _dot_nt(a, b):
    return lax.dot_general(a, b, (((1,), (1,)), ((), ())), preferred_element_type=F32)


def _norm_mod(t, g, shift, scale):
    ms = jnp.mean(t * t, axis=-1, keepdims=True)
    y = t * lax.rsqrt(ms + EPS) * g
    return (y * (1.0 + scale) + shift).astype(BF16)


def _mod_kernel(c_ref, w_ref, b_ref, o_ref):
    a = jax.nn.silu(c_ref[...]).astype(BF16)
    o_ref[...] = _dot(a, w_ref[...].astype(BF16)) + b_ref[...]


def _modulation(cc, w_mod, b_mod):
    d, n = w_mod.shape
    tn = _tile(n, 768)
    out = pl.pallas_call(
        _mod_kernel,
        grid=(n // tn,),
        in_specs=[
            pl.BlockSpec((MOD_ROWS, d), lambda j: (0, 0)),
            pl.BlockSpec((d, tn), lambda j: (0, j)),
            pl.BlockSpec((1, tn), lambda j: (0, j)),
        ],
        out_specs=pl.BlockSpec((MOD_ROWS, tn), lambda j: (0, j)),
        out_shape=jax.ShapeDtypeStruct((MOD_ROWS, n), F32),
        compiler_params=_params("arbitrary"),
        name="modulation",
    )(cc, w_mod, b_mod.reshape(1, n))
    return out.reshape(MOD_ROWS, 1, n)


def _conv_kernel(xp_ref, x_ref, xn_ref, mod_ref, g_ref, wb_ref, wc_ref, wx_ref, wz_ref, cw_ref, cb_ref,
                 wo_ref, o_ref, h_scr, *, tm, d, seq):
    i = pl.program_id(0)
    j = pl.program_id(1)
    halo = BF16_ROWS
    mod = mod_ref[0]
    shift, scale, gate = mod[:, :d], mod[:, d:2 * d], mod[:, 2 * d:]

    @pl.when(j == 0)
    def _():
        g = g_ref[...]
        h_scr[0:halo, :] = _norm_mod(xp_ref[...], g, shift, scale)
        h_scr[halo:halo + tm, :] = _norm_mod(x_ref[...], g, shift, scale)
        h_scr[halo + tm:, :] = _norm_mod(xn_ref[...], g, shift, scale)
        o_ref[...] = jnp.zeros_like(o_ref)

    n = tm + 2 * halo
    h = h_scr[...]
    hm = h_scr[halo:halo + tm, :]
    u = _dot(h, wc_ref[...]) * _dot(h, wx_ref[...])
    pos = lax.rem(i * tm + lax.broadcasted_iota(jnp.int32, (tm, 1), 0), seq)
    u_prev = jnp.where(pos == 0, 0.0, pltpu.roll(u, 1, 0)[halo:halo + tm])
    u_next = jnp.where(pos == seq - 1, 0.0, pltpu.roll(u, n - 1, 0)[halo:halo + tm])
    cw = cw_ref[...]
    y = u_prev * cw[0:1] + u[halo:halo + tm] * cw[1:2] + u_next * cw[2:3] + cb_ref[...]
    gated = _dot(hm, wb_ref[...]) * y * jax.nn.silu(_dot(hm, wz_ref[...]))
    o_ref[...] += _dot(gated.astype(BF16), wo_ref[...])

    @pl.when(j == pl.num_programs(1) - 1)
    def _():
        o_ref[...] = x_ref[...] + gate * o_ref[...]


def _conv_layer(x2, stream, mod, norm_g, w_in, conv_w, conv_b, w_out):
    m, d = x2.shape
    di = w_out.shape[0]
    tm = min(stream.span, ROW_TILE)
    tn = _tile(di, 512)
    nj = di // tn
    hb = tm // BF16_ROWS
    last_hb = m // BF16_ROWS - 1
    w_spec = lambda c: pl.BlockSpec((d, tn), lambda i, j: (0, c * nj + j))
    return pl.pallas_call(
        functools.partial(_conv_kernel, tm=tm, d=d, seq=stream.seq),
        grid=(m // tm, nj),
        in_specs=[
            pl.BlockSpec((BF16_ROWS, d), lambda i, j: (jnp.maximum(i * hb - 1, 0), 0)),
            pl.BlockSpec((tm, d), lambda i, j: (i, 0)),
            pl.BlockSpec((BF16_ROWS, d), lambda i, j: (jnp.minimum((i + 1) * hb, last_hb), 0)),
            pl.BlockSpec((1, 1, 3 * d), lambda i, j: (stream.mod_row(i * tm), 0, 0)),
            pl.BlockSpec((1, d), lambda i, j: (0, 0)),
            w_spec(0), w_spec(1), w_spec(2), w_spec(3),
            pl.BlockSpec((3, tn), lambda i, j: (0, j)),
            pl.BlockSpec((1, tn), lambda i, j: (0, j)),
            pl.BlockSpec((tn, d), lambda i, j: (j, 0)),
        ],
        out_specs=pl.BlockSpec((tm, d), lambda i, j: (i, 0)),
        out_shape=jax.ShapeDtypeStruct((m, d), F32),
        scratch_shapes=[pltpu.VMEM((tm + 2 * BF16_ROWS, d), BF16)],
        compiler_params=_params("parallel", "arbitrary"),
        name="conv_layer",
    )(x2, x2, x2, mod, norm_g.reshape(1, d), w_in, w_in, w_in, w_in, conv_w, conv_b.reshape(1, di), w_out)


def _head_norm_rope(acc, gn, cos, sin, group, out_scale):
    outs = []
    swap = group // 4
    for c in range(acc.shape[1] // LANES):
        t = acc[:, c * LANES:(c + 1) * LANES]
        lane = lax.broadcasted_iota(jnp.int32, t.shape, 1)
        sq = t * t
        if group == LANES:
            ms = jnp.mean(sq, axis=-1, keepdims=True)
        else:
            lo = lane < group
            s_lo = jnp.sum(jnp.where(lo, sq, 0.0), axis=-1, keepdims=True)
            s_hi = jnp.sum(jnp.where(lo, 0.0, sq), axis=-1, keepdims=True)
            ms = jnp.where(lo, s_lo, s_hi) * (1.0 / group)
        y = t * lax.rsqrt(ms + EPS) * gn
        if cos is not None:
            ahead = pltpu.roll(y, LANES - swap, 1)
            behind = pltpu.roll(y, swap, 1)
            y = y * cos + jnp.where((lane & swap) == 0, ahead, behind) * sin
        if out_scale != 1.0:
            y = y * out_scale
        outs.append(y)
    return outs[0] if len(outs) == 1 else jnp.concatenate(outs, axis=1)


def _proj_kernel(*refs, d, q_tiles, k_tiles, group, rope, q_scale):
    if rope:
        x_ref, mod_ref, g_ref, w_ref, qn_ref, kn_ref, cos_ref, sin_ref, o_ref, h_scr = refs
    else:
        x_ref, mod_ref, g_ref, w_ref, qn_ref, kn_ref, o_ref, h_scr = refs
    j = pl.program_id(1)

    @pl.when(j == 0)
    def _():
        mod = mod_ref[0]
        h_scr[...] = _norm_mod(x_ref[...], g_ref[...], mod[:, :d], mod[:, d:2 * d])

    tm, tn = o_ref.shape
    cw = min(tn, MXU_COLS)

    def units(rh, epilogue):
        for r in range(tm // rh):
            rows = slice(r * rh, (r + 1) * rh)
            h = h_scr[rows, :]
            for c in range(tn // cw):
                cols = slice(c * cw, (c + 1) * cw)
                o_ref[rows, cols] = epilogue(_dot(h, w_ref[:, cols]), rows).astype(BF16)

    def head_epilogue(gn_ref, scale):
        def epilogue(acc, rows):
            cos = cos_ref[rows, :] if rope else None
            sin = sin_ref[rows, :] if rope else None
            return _head_norm_rope(acc, gn_ref[...], cos, sin, group, scale)
        return epilogue

    @pl.when(j < q_tiles)
    def _():
        units(min(tm, EPILOGUE_ROWS), head_epilogue(qn_ref, q_scale))

    @pl.when((j >= q_tiles) & (j < q_tiles + k_tiles))
    def _():
        units(min(tm, EPILOGUE_ROWS), head_epilogue(kn_ref, 1.0))

    @pl.when(j >= q_tiles + k_tiles)
    def _():
        units(tm, lambda acc, rows: acc)


def _rope_tables(seq, group):
    rows = seq // GRID_W
    row = jnp.repeat(jnp.arange(rows), GRID_W).astype(F32)
    col = jnp.tile(jnp.arange(GRID_W), rows).astype(F32)
    n_freq = group // 4
    inv_freq = ROPE_BASE ** (-(jnp.arange(n_freq, dtype=F32) / n_freq))
    ar, ac = row[:, None] * inv_freq, col[:, None] * inv_freq
    cos = jnp.concatenate([jnp.cos(ar), jnp.cos(ar), jnp.cos(ac), jnp.cos(ac)], axis=-1)
    sin = jnp.concatenate([-jnp.sin(ar), jnp.sin(ar), -jnp.sin(ac), jnp.sin(ac)], axis=-1)
    reps = LANES // group
    return jnp.tile(cos, (1, reps)), jnp.tile(sin, (1, reps))


def _proj_layer(x2, stream, mod, norm_g, w_in, q_norm, k_norm, q_cols, k_cols, rope, q_scale):
    m, d = x2.shape
    n = w_in.shape[1]
    group = q_norm.shape[0]
    tm = min(stream.span, WIDE_ROW_TILE)
    tn = math.gcd(math.gcd(q_cols, k_cols), 1024)
    reps = LANES // group
    args = [x2, mod, norm_g.reshape(1, d), w_in,
            jnp.tile(q_norm, reps).reshape(1, LANES), jnp.tile(k_norm, reps).reshape(1, LANES)]
    in_specs = [
        pl.BlockSpec((tm, d), lambda i, j: (i, 0)),
        pl.BlockSpec((1, 1, 3 * d), lambda i, j: (stream.mod_row(i * tm), 0, 0)),
        pl.BlockSpec((1, d), lambda i, j: (0, 0)),
        pl.BlockSpec((d, tn), lambda i, j: (0, j)),
        pl.BlockSpec((1, LANES), lambda i, j: (0, 0)),
        pl.BlockSpec((1, LANES), lambda i, j: (0, 0)),
    ]
    if rope:
        tiles_per_seq = stream.seq // tm
        args += list(_rope_tables(stream.seq, group))
        in_specs += [pl.BlockSpec((tm, LANES), lambda i, j: (i % tiles_per_seq, 0))] * 2
    return pl.pallas_call(
        functools.partial(_proj_kernel, d=d, q_tiles=q_cols // tn, k_tiles=k_cols // tn, group=group,
                          rope=rope, q_scale=q_scale),
        grid=(m // tm, n // tn),
        in_specs=in_specs,
        out_specs=pl.BlockSpec((tm, tn), lambda i, j: (i, j)),
        out_shape=jax.ShapeDtypeStruct((m, n), BF16),
        scratch_shapes=[pltpu.VMEM((tm, d), BF16)],
        compiler_params=_params("parallel", "arbitrary"),
        name="proj_layer",
    )(*args)


def _out_kernel(a_ref, w_ref, x_ref, gate_ref, o_ref):
    o_ref[...] = x_ref[...] + gate_ref[0] * _dot(a_ref[...], w_ref[...])


def _out_layer(a2, x2, stream, mod, w_out):
    m, d = x2.shape
    di = a2.shape[1]
    tm = min(stream.span, WIDE_ROW_TILE)
    tn = _tile(d, 1024)
    gate_block0 = 2 * d // tn
    return pl.pallas_call(
        _out_kernel,
        grid=(m // tm, d // tn),
        in_specs=[
            pl.BlockSpec((tm, di), lambda i, j: (i, 0)),
            pl.BlockSpec((di, tn), lambda i, j: (0, j)),
            pl.BlockSpec((tm, tn), lambda i, j: (i, j)),
            pl.BlockSpec((1, 1, tn), lambda i, j: (stream.mod_row(i * tm), 0, gate_block0 + j)),
        ],
        out_specs=pl.BlockSpec((tm, tn), lambda i, j: (i, j)),
        out_shape=jax.ShapeDtypeStruct((m, d), F32),
        compiler_params=_params("parallel", "arbitrary"),
        name="out_layer",
    )(a2, w_out, x2, mod)


def _lane_chunks(t):
    return [t[:, c * LANES:(c + 1) * LANES] for c in range(t.shape[1] // LANES)]


def _diff_attn_kernel(*refs, tq, tk, n_chunks, has_ctx, tiles, steps_per_head, lam_init):
    pipelined = tiles > 1
    refs = iter(refs)
    lam_ref, q_ref = next(refs), next(refs)
    qn_ref = kn_ref = kcn_ref = kc_ref = vc_ref = vct_scr = None
    if pipelined:
        qn_ref, kn_ref = next(refs), next(refs)
        if has_ctx:
            kcn_ref = next(refs)
    k_ref, v_ref = next(refs), next(refs)
    if has_ctx:
        kc_ref, vc_ref = next(refs), next(refs)
    z_ref, sn_ref, o_ref, vt_scr = next(refs), next(refs), next(refs), next(refs)
    if has_ctx:
        vct_scr = next(refs)
    bufs = list(refs)
    per_set = 2 + has_ctx + pipelined
    sets = []
    for p in range(len(bufs) // per_set):
        group = bufs[p * per_set:(p + 1) * per_set]
        sets.append((group[0], group[1] if has_ctx else None, group[1 + has_ctx], group[-1] if pipelined else None))
    half = LANES // 2

    def tile_rows(t):
        return pl.ds(pl.multiple_of(t * tq, tq), tq)

    def transpose_values():
        for c in range(n_chunks):
            vt_scr[:, c * tk:(c + 1) * tk] = v_ref[c * tk:(c + 1) * tk, :].astype(F32).T.astype(BF16)
        if has_ctx:
            vct_scr[...] = vc_ref[...].astype(F32).T.astype(BF16)

    def sublane_groups(s):
        return s.reshape(s.shape[0] // SUBLANES, SUBLANES, s.shape[1])

    def scores(q, buf, t):
        s_buf, c_buf, m_buf, _ = buf
        lane = lax.broadcasted_iota(jnp.int32, q.shape, 1)
        zero = jnp.zeros_like(q)
        subs = (jnp.where(lane < half, q, zero), jnp.where(lane < half, zero, q))
        for h, qh in enumerate(subs):
            cols = slice(h * tq, (h + 1) * tq)
            mx = jnp.full((SUBLANES, tq), NEG_INF, F32)
            for c in range(n_chunks):
                s = _dot_nt(k_ref[c * tk:(c + 1) * tk, :], qh)
                s_buf[t, c, :, cols] = s
                mx = jnp.maximum(mx, jnp.max(sublane_groups(s), axis=0))
            if has_ctx:
                s = _dot_nt(kc_ref[...], qh)
                c_buf[t, :, cols] = s
                mx = jnp.maximum(mx, jnp.max(sublane_groups(s), axis=0))
            m_buf[t, :, cols] = jnp.broadcast_to(jnp.max(mx, axis=0, keepdims=True), (SUBLANES, tq))

    def weighted(buf, t):
        s_buf, c_buf, m_buf, _ = buf
        m = m_buf[t]
        lsum = jnp.zeros((SUBLANES, 2 * tq), F32)
        acc = jnp.zeros((LANES, 2 * tq), F32)
        blocks = [(s_buf[t, c], vt_scr[:, c * tk:(c + 1) * tk]) for c in range(n_chunks)]
        if has_ctx:
            blocks.append((c_buf[t], vct_scr[...]))
        for s, vt in blocks:
            p = jnp.exp2(sublane_groups(s) - m[None])
            lsum = lsum + jnp.sum(p, axis=0)
            acc = acc + _dot(vt, p.reshape(s.shape).astype(BF16))
        return acc / jnp.sum(lsum, axis=0, keepdims=True)

    lv = lam_ref[...]
    lam = (jnp.exp(jnp.sum(lv[0:1] * lv[1:2], axis=-1, keepdims=True))
           - jnp.exp(jnp.sum(lv[2:3] * lv[3:4], axis=-1, keepdims=True)) + lam_init)

    def finish(o_all, z):
        o = (o_all[:, :tq] - lam * o_all[:, tq:]).T
        ms = jnp.mean(o * o, axis=-1, keepdims=True)
        y = o * lax.rsqrt(ms + EPS) * sn_ref[...] * (1.0 - lam_init)
        return (y * jax.nn.silu(z.astype(F32))).astype(BF16)

    if not pipelined:
        transpose_values()
        scores(q_ref[...], sets[0], 0)
        o_ref[...] = finish(weighted(sets[0], 0), z_ref[...])
        return

    step_index = pl.program_id(0)

    @pl.when(step_index % steps_per_head == 0)
    def _():
        transpose_values()

    @pl.when(step_index == 0)
    def _():
        sets[0][3][...] = jnp.zeros_like(sets[0][3])

        @pl.loop(0, tiles)
        def _(t):
            scores(q_ref[tile_rows(t), :], sets[0], t)

    def both(q, cur, nxt, t):
        s_nxt, c_nxt, m_nxt, _ = nxt
        s_cur, c_cur, m_cur, _ = cur
        lane = lax.broadcasted_iota(jnp.int32, q.shape, 1)
        zero = jnp.zeros_like(q)
        subs = (jnp.where(lane < half, q, zero), jnp.where(lane < half, zero, q))
        mxs = [jnp.full((SUBLANES, tq), NEG_INF, F32) for _ in subs]
        m = m_cur[t]
        lsum = jnp.zeros((SUBLANES, 2 * tq), F32)
        acc = jnp.zeros((LANES, 2 * tq), F32)
        tie = None
        for c in range(n_chunks + has_ctx):
            if c < n_chunks:
                k, vt, s_old = kn_ref[c * tk:(c + 1) * tk, :], vt_scr[:, c * tk:(c + 1) * tk], s_cur[t, c]
            else:
                k, vt, s_old = kcn_ref[...], vct_scr[...], c_cur[t]
            if tie is not None:
                k = k + jnp.tile(tie, (k.shape[0] // SUBLANES, 1)).astype(BF16)
            for h, qh in enumerate(subs):
                cols = slice(h * tq, (h + 1) * tq)
                s = _dot_nt(k, qh)
                if c < n_chunks:
                    s_nxt[t, c, :, cols] = s
                else:
                    c_nxt[t, :, cols] = s
                mxs[h] = jnp.maximum(mxs[h], jnp.max(sublane_groups(s), axis=0))
            p = jnp.exp2(sublane_groups(s_old) - m[None])
            lsum = lsum + jnp.sum(p, axis=0)
            acc = acc + _dot(vt, p.reshape(s_old.shape).astype(BF16))
            bits = pltpu.bitcast(p[-1][:, :LANES], jnp.uint32)
            tie = pltpu.bitcast((bits >> 16) >> 16, F32)
        for h, mx in enumerate(mxs):
            m_nxt[t, :, h * tq:(h + 1) * tq] = jnp.broadcast_to(jnp.max(mx, axis=0, keepdims=True), (SUBLANES, tq))
        return acc / jnp.sum(lsum, axis=0, keepdims=True)

    def step(cur, nxt):
        for t in range(tiles):
            rows = slice(t * tq, (t + 1) * tq)
            o_ref[rows, :] = finish(cur[3][t], z_ref[rows, :])
            nxt[3][t] = both(qn_ref[rows, :], cur, nxt, t)

    @pl.when(step_index % 2 == 0)
    def _():
        step(sets[0], sets[1])

    @pl.when(step_index % 2 == 1)
    def _():
        step(sets[1], sets[0])


def _diff_attention(qkvz, seq, batch, ctx_qkvz, ctx_len, lam_vecs, sub_norm, lam_init):
    di = qkvz.shape[1] // 4
    heads = di // LANES
    tq = min(seq, 256)
    tk = min(seq, 512)
    tiles = 2 if seq >= 2 * tq else 1
    rows = tiles * tq
    nq = seq // rows
    has_ctx = ctx_qkvz is not None
    pipelined = tiles > 1
    n_steps = batch * heads * nq

    def at(offset, index):
        def index_map(g):
            g = jnp.clip(g + offset, 0, n_steps - 1)
            return index(g // (heads * nq), (g // nq) % heads, g % nq)
        return index_map

    q_idx = lambda b, h, i: (b * nq + i, h)
    k_idx = lambda b, h, i: (b, heads + h)
    v_idx = lambda b, h, i: (b, 2 * heads + h)
    z_idx = lambda b, h, i: (b * nq + i, 3 * heads + h)
    q_spec = lambda index_map: pl.BlockSpec((rows, LANES), index_map)
    kv_spec = lambda index_map: pl.BlockSpec((seq, LANES), index_map)
    ctx_spec = lambda index_map: pl.BlockSpec((ctx_len, LANES), index_map)

    args = [lam_vecs, qkvz]
    in_specs = [pl.BlockSpec(lam_vecs.shape, lambda g: (0, 0))]
    if pipelined:
        first = at(-n_steps, lambda b, h, i: (b, h, i))
        own_q, own_k, lag = (lambda g: q_idx(*first(g))), (lambda g: k_idx(*first(g))), 1
        in_specs += [q_spec(own_q), q_spec(at(1, q_idx)), kv_spec(at(1, k_idx))]
        args += [qkvz, qkvz]
        if has_ctx:
            args += [ctx_qkvz]
            in_specs += [ctx_spec(at(1, k_idx))]
    else:
        own_k, lag = at(0, k_idx), 0
        in_specs += [q_spec(at(0, q_idx))]
    args += [qkvz, qkvz]
    in_specs += [kv_spec(own_k), kv_spec(at(0, v_idx))]
    if has_ctx:
        args += [ctx_qkvz, ctx_qkvz]
        in_specs += [ctx_spec(own_k), ctx_spec(at(0, v_idx))]
    args += [qkvz, sub_norm.reshape(1, LANES)]
    in_specs += [q_spec(at(-lag, z_idx)), pl.BlockSpec((1, LANES), lambda g: (0, 0))]
    scratch = [pltpu.VMEM((LANES, seq), BF16)] + ([pltpu.VMEM((LANES, ctx_len), BF16)] if has_ctx else [])
    for _ in range(2 if pipelined else 1):
        scratch += [pltpu.VMEM((tiles, seq // tk, tk, 2 * tq), F32)]
        if has_ctx:
            scratch += [pltpu.VMEM((tiles, ctx_len, 2 * tq), F32)]
        scratch += [pltpu.VMEM((tiles, SUBLANES, 2 * tq), F32)]
        if pipelined:
            scratch += [pltpu.VMEM((tiles, LANES, 2 * tq), F32)]
    return pl.pallas_call(
        functools.partial(_diff_attn_kernel, tq=tq, tk=tk, n_chunks=seq // tk, has_ctx=has_ctx, tiles=tiles,
                          steps_per_head=nq, lam_init=lam_init),
        grid=(n_steps + lag,),
        in_specs=in_specs,
        out_specs=q_spec(at(-lag, q_idx)),
        out_shape=jax.ShapeDtypeStruct((batch * seq, di), BF16),
        scratch_shapes=scratch,
        compiler_params=_params("arbitrary"),
        name="diff_attention",
    )(*args)


def _win_attn_kernel(*refs, group, kv_heads, n_z):
    q_ref, kp_ref, kc_ref, kn_ref, vp_ref, vc_ref, vn_ref, kx_ref, vx_ref, sink_ref = refs[:10]
    z_refs = refs[10:10 + n_z]
    o_ref = refs[10 + n_z]
    i = pl.program_id(1)
    blk = ATTN_BLOCK
    rows = group * blk
    zw = z_refs[0].shape[1]
    qpos = lax.broadcasted_iota(jnp.int32, (rows, blk), 0) % blk
    kpos = lax.broadcasted_iota(jnp.int32, (rows, blk), 1)
    keep_prev = (kpos >= qpos) & (i > 0)
    keep_next = (kpos <= qpos) & (i < pl.num_programs(1) - 1)
    sink2 = sink_ref[...] * math.log2(math.e)
    for n in range(kv_heads):
        ks = slice(n * LANES, (n + 1) * LANES)
        q4 = jnp.concatenate([q_ref[:, (n * group + g) * LANES:(n * group + g + 1) * LANES] for g in range(group)],
                             axis=0)
        k_all = jnp.concatenate([kp_ref[:, ks], kc_ref[:, ks], kn_ref[:, ks], kx_ref[:, ks]], axis=0)
        v_all = jnp.concatenate([vp_ref[:, ks], vc_ref[:, ks], vn_ref[:, ks], vx_ref[:, ks]], axis=0)
        s = _lane_chunks(_dot_nt(q4, k_all))
        s[0] = jnp.where(keep_prev, s[0], NEG_INF)
        s[2] = jnp.where(keep_next, s[2], NEG_INF)
        sink_b = jnp.concatenate([jnp.broadcast_to(sink2[n, g:g + 1, :], (blk, LANES)) for g in range(group)], axis=0)
        mx = sink_b
        for t in s:
            mx = jnp.maximum(mx, t)
        m = jnp.broadcast_to(jnp.max(mx, axis=-1, keepdims=True), (rows, LANES))
        ps = [jnp.exp2(t - m) for t in s]
        lsum = ps[0]
        for t in ps[1:]:
            lsum = lsum + t
        l = jnp.sum(lsum, axis=-1, keepdims=True) + jnp.exp2(sink_b - m)[:, 0:1]
        o = _dot(jnp.concatenate(ps, axis=1).astype(BF16), v_all) / l
        for g in range(group):
            col = (n * group + g) * LANES
            z = z_refs[col // zw][:, col % zw:col % zw + LANES].astype(F32)
            o_ref[:, col:col + LANES] = (o[g * blk:(g + 1) * blk] * jax.nn.silu(z)).astype(BF16)


def _win_attention(qkvz, seq, batch, ctx_qkvz, ctx_len, di, kv_heads, sink):
    heads = di // LANES
    group = heads // kv_heads
    kvw = kv_heads * LANES
    blk = ATTN_BLOCK
    nb = seq // blk
    z_col0 = di + 2 * kvw
    zw = math.gcd(z_col0, di)
    n_z = di // zw
    assert di % kvw == 0
    sink_rep = jnp.broadcast_to(sink.astype(F32).reshape(kv_heads, group, 1), (kv_heads, group, LANES))
    band = lambda col_block, shift: pl.BlockSpec(
        (blk, kvw), lambda b, i: (b * nb + jnp.clip(i + shift, 0, nb - 1), col_block))
    k_blk, v_blk = di // kvw, di // kvw + 1
    z_spec = lambda t: pl.BlockSpec((blk, zw), lambda b, i: (b * nb + i, z_col0 // zw + t))
    return pl.pallas_call(
        functools.partial(_win_attn_kernel, group=group, kv_heads=kv_heads, n_z=n_z),
        grid=(batch, nb),
        in_specs=[
            pl.BlockSpec((blk, di), lambda b, i: (b * nb + i, 0)),
            band(k_blk, -1), band(k_blk, 0), band(k_blk, 1),
            band(v_blk, -1), band(v_blk, 0), band(v_blk, 1),
            pl.BlockSpec((ctx_len, kvw), lambda b, i: (b, k_blk)),
            pl.BlockSpec((ctx_len, kvw), lambda b, i: (b, v_blk)),
            pl.BlockSpec((kv_heads, group, LANES), lambda b, i: (0, 0, 0)),
        ] + [z_spec(t) for t in range(n_z)],
        out_specs=pl.BlockSpec((blk, di), lambda b, i: (b * nb + i, 0)),
        out_shape=jax.ShapeDtypeStruct((batch * seq, di), BF16),
        compiler_params=_params("parallel", "arbitrary"),
        name="win_attention",
    )(*([qkvz] * 7 + [ctx_qkvz, ctx_qkvz, sink_rep] + [qkvz] * n_z))


class _Stream(NamedTuple):
    seq: int
    span: int
    mod_row: Callable


def kernel(x, c, ctx, c_ctx, l0_norm, l0_w_mod, l0_b_mod, l0_w_in, l0_conv_w, l0_conv_b, l0_w_out, l1_norm, l1_w_mod, l1_b_mod, l1_w_in, l1_q_norm, l1_k_norm, l1_lam_q1, l1_lam_k1, l1_lam_q2, l1_lam_k2, l1_sub_norm, l1_w_out, l2_norm, l2_w_mod, l2_b_mod, l2_w_in, l2_q_norm, l2_k_norm, l2_sink, l2_w_out, l3_norm, l3_w_mod, l3_b_mod, l3_w_in, l3_conv_w, l3_conv_b, l3_w_out):
    batch, seq, d = x.shape
    ctx_len = ctx.shape[1]
    di = l0_w_out.shape[0]
    assert batch < MOD_ROWS
    xs = _Stream(seq, seq, lambda row: row // seq)
    cs = _Stream(ctx_len, batch * ctx_len, lambda row: batch)

    x2 = x.reshape(batch * seq, d)
    c2 = ctx.reshape(batch * ctx_len, d)
    cc = jnp.zeros((MOD_ROWS, d), F32).at[:batch].set(c).at[batch].set(c_ctx)
    mod0, mod1, mod2, mod3 = (_modulation(cc, w, b) for w, b in
                              ((l0_w_mod, l0_b_mod), (l1_w_mod, l1_b_mod), (l2_w_mod, l2_b_mod), (l3_w_mod, l3_b_mod)))
    bf = lambda w: w.astype(BF16)

    w_in, w_out = bf(l0_w_in), bf(l0_w_out)
    x2n = _conv_layer(x2, xs, mod0, l0_norm, w_in, l0_conv_w, l0_conv_b, w_out)
    c2 = _conv_layer(c2, cs, mod0, l0_norm, w_in, l0_conv_w, l0_conv_b, w_out)
    x2 = x2n

    w_in, w_out = bf(l1_w_in), bf(l1_w_out)
    lam_init = 0.8 - 0.6 * math.exp(-0.3 * 1)
    q_scale = l1_q_norm.shape[0] ** -0.5 * math.log2(math.e)
    proj = functools.partial(_proj_layer, norm_g=l1_norm, w_in=w_in, q_norm=l1_q_norm, k_norm=l1_k_norm,
                             q_cols=di, k_cols=di, q_scale=q_scale)
    qx = proj(x2, xs, mod1, rope=True)
    qc = proj(c2, cs, mod1, rope=False)
    lam_vecs = jnp.stack([l1_lam_q1, l1_lam_k1, l1_lam_q2, l1_lam_k2]).astype(F32)
    ax = _diff_attention(qx, seq, batch, qc, ctx_len, lam_vecs, l1_sub_norm, lam_init)
    ac = _diff_attention(qc, ctx_len, batch, None, 0, lam_vecs, l1_sub_norm, lam_init)
    x2 = _out_layer(ax, x2, xs, mod1, w_out)
    c2 = _out_layer(ac, c2, cs, mod1, w_out)

    w_in, w_out = bf(l2_w_in), bf(l2_w_out)
    kv_cols = (l2_w_in.shape[1] - 2 * di) // 2
    proj = functools.partial(_proj_layer, norm_g=l2_norm, w_in=w_in, q_norm=l2_q_norm, k_norm=l2_k_norm,
                             q_cols=di, k_cols=kv_cols, q_scale=l2_q_norm.shape[0] ** -0.5 * math.log2(math.e))
    qx = proj(x2, xs, mod2, rope=True)
    qc = proj(c2, cs, mod2, rope=False)
    ax = _win_attention(qx, seq, batch, qc, ctx_len, di, kv_cols // LANES, l2_sink)
    x2 = _out_layer(ax, x2, xs, mod2, w_out)

    x2 = _conv_layer(x2, xs, mod3, l3_norm, bf(l3_w_in), l3_conv_w, l3_conv_b, bf(l3_w_out))
    return x2.reshape(batch, seq, d)
```

```python
import functools
import math
from typing import Callable, NamedTuple

import jax
import jax.numpy as jnp
from jax import lax
from jax.experimental import pallas as pl
from jax.experimental.pallas import tpu as pltpu

LANES = 128
SUBLANES = 8
MXU_COLS = 256
BF16_ROWS = 16
GRID_W = 64
ROPE_BASE = 10000.0
EPS = 1e-6
NEG_INF = -1e30
ATTN_BLOCK = 128
EPILOGUE_ROWS = 256
ROW_TILE = 512
WIDE_ROW_TILE = 1024
MOD_ROWS = 8
VMEM_LIMIT = 56 * 1024 * 1024

F32 = jnp.float32
BF16 = jnp.bfloat16


def _params(*sem):
    return pltpu.CompilerParams(dimension_semantics=sem, vmem_limit_bytes=VMEM_LIMIT)


def _tile(n, target):
    if n <= target:
        return n
    t = target - target % LANES
    while n % t:
        t -= LANES
    return t


def _div_mod(x, n):
    if n & (n - 1) == 0:
        return lax.shift_right_logical(x, n.bit_length() - 1), x & (n - 1)
    return x // n, x % n


def _dot(a, b):
    return jnp.dot(a, b, preferred_element_type=F32)


def _dot_nt(a, b):
    return lax.dot_general(a, b, (((1,), (1,)), ((), ())), preferred_element_type=F32)


def _norm_mod(t, g, shift, scale):
    ms = jnp.mean(t * t, axis=-1, keepdims=True)
    y = t * lax.rsqrt(ms + EPS) * g
    return (y * (1.0 + scale) + shift).astype(BF16)


def _mod_kernel(c_ref, w_ref, b_ref, o_ref):
    a = jax.nn.silu(c_ref[...]).astype(BF16)
    o_ref[...] = _dot(a, w_ref[...].astype(BF16)) + b_ref[...]


def _modulation(cc, w_mod, b_mod):
    d, n = w_mod.shape
    tn = _tile(n, 768)
    out = pl.pallas_call(
        _mod_kernel,
        grid=(n // tn,),
        in_specs=[
            pl.BlockSpec((MOD_ROWS, d), lambda j: (0, 0)),
            pl.BlockSpec((d, tn), lambda j: (0, j)),
            pl.BlockSpec((1, tn), lambda j: (0, j)),
        ],
        out_specs=pl.BlockSpec((MOD_ROWS, tn), lambda j: (0, j)),
        out_shape=jax.ShapeDtypeStruct((MOD_ROWS, n), F32),
        compiler_params=_params("arbitrary"),
        name="modulation",
    )(cc, w_mod, b_mod.reshape(1, n))
    return out.reshape(MOD_ROWS, 1, n)


def _conv_kernel(xp_ref, x_ref, xn_ref, mod_ref, g_ref, wb_ref, wc_ref, wx_ref, wz_ref, cw_ref, cb_ref,
                 wo_ref, o_ref, h_scr, *, tm, d, seq):
    i = pl.program_id(0)
    j = pl.program_id(1)
    halo = BF16_ROWS
    mod = mod_ref[0]
    shift, scale, gate = mod[:, :d], mod[:, d:2 * d], mod[:, 2 * d:]

    @pl.when(j == 0)
    def _():
        g = g_ref[...]
        h_scr[0:halo, :] = _norm_mod(xp_ref[...], g, shift, scale)
        h_scr[halo:halo + tm, :] = _norm_mod(x_ref[...], g, shift, scale)
        h_scr[halo + tm:, :] = _norm_mod(xn_ref[...], g, shift, scale)
        o_ref[...] = jnp.zeros_like(o_ref)

    n = tm + 2 * halo
    h = h_scr[...]
    hm = h_scr[halo:halo + tm, :]
    u = _dot(h, wc_ref[...]) * _dot(h, wx_ref[...])
    pos = lax.rem(i * tm + lax.broadcasted_iota(jnp.int32, (tm, 1), 0), seq)
    u_prev = jnp.where(pos == 0, 0.0, pltpu.roll(u, 1, 0)[halo:halo + tm])
    u_next = jnp.where(pos == seq - 1, 0.0, pltpu.roll(u, n - 1, 0)[halo:halo + tm])
    cw = cw_ref[...]
    y = u_prev * cw[0:1] + u[halo:halo + tm] * cw[1:2] + u_next * cw[2:3] + cb_ref[...]
    gated = _dot(hm, wb_ref[...]) * y * jax.nn.silu(_dot(hm, wz_ref[...]))
    o_ref[...] += _dot(gated.astype(BF16), wo_ref[...])

    @pl.when(j == pl.num_programs(1) - 1)
    def _():
        o_ref[...] = x_ref[...] + gate * o_ref[...]


def _conv_layer(x2, stream, mod, norm_g, w_in, conv_w, conv_b, w_out):
    m, d = x2.shape
    di = w_out.shape[0]
    tm = min(stream.span, ROW_TILE)
    tn = _tile(di, 512)
    nj = di // tn
    hb = tm // BF16_ROWS
    last_hb = m // BF16_ROWS - 1
    w_spec = lambda c: pl.BlockSpec((d, tn), lambda i, j: (0, c * nj + j))
    return pl.pallas_call(
        functools.partial(_conv_kernel, tm=tm, d=d, seq=stream.seq),
        grid=(m // tm, nj),
        in_specs=[
            pl.BlockSpec((BF16_ROWS, d), lambda i, j: (jnp.maximum(i * hb - 1, 0), 0)),
            pl.BlockSpec((tm, d), lambda i, j: (i, 0)),
            pl.BlockSpec((BF16_ROWS, d), lambda i, j: (jnp.minimum((i + 1) * hb, last_hb), 0)),
            pl.BlockSpec((1, 1, 3 * d), lambda i, j: (stream.mod_row(i * tm), 0, 0)),
            pl.BlockSpec((1, d), lambda i, j: (0, 0)),
            w_spec(0), w_spec(1), w_spec(2), w_spec(3),
            pl.BlockSpec((3, tn), lambda i, j: (0, j)),
            pl.BlockSpec((1, tn), lambda i, j: (0, j)),
            pl.BlockSpec((tn, d), lambda i, j: (j, 0)),
        ],
        out_specs=pl.BlockSpec((tm, d), lambda i, j: (i, 0)),
        out_shape=jax.ShapeDtypeStruct((m, d), F32),
        scratch_shapes=[pltpu.VMEM((tm + 2 * BF16_ROWS, d), BF16)],
        compiler_params=_params("parallel", "arbitrary"),
        name="conv_layer",
    )(x2, x2, x2, mod, norm_g.reshape(1, d), w_in, w_in, w_in, w_in, conv_w, conv_b.reshape(1, di), w_out)


def _head_norm_rope(acc, gn, cos, sin, group, out_scale):
    outs = []
    swap = group // 4
    for c in range(acc.shape[1] // LANES):
        t = acc[:, c * LANES:(c + 1) * LANES]
        lane = lax.broadcasted_iota(jnp.int32, t.shape, 1)
        sq = t * t
        if group == LANES:
            ms = jnp.mean(sq, axis=-1, keepdims=True)
        else:
            lo = lane < group
            s_lo = jnp.sum(jnp.where(lo, sq, 0.0), axis=-1, keepdims=True)
            s_hi = jnp.sum(jnp.where(lo, 0.0, sq), axis=-1, keepdims=True)
            ms = jnp.where(lo, s_lo, s_hi) * (1.0 / group)
        y = t * lax.rsqrt(ms + EPS) * gn
        if cos is not None:
            ahead = pltpu.roll(y, LANES - swap, 1)
            behind = pltpu.roll(y, swap, 1)
            y = y * cos + jnp.where((lane & swap) == 0, ahead, behind) * sin
        if out_scale != 1.0:
            y = y * out_scale
        outs.append(y)
    return outs[0] if len(outs) == 1 else jnp.concatenate(outs, axis=1)


def _proj_kernel(*refs, d, q_tiles, k_tiles, group, rope, q_scale):
    if rope:
        x_ref, mod_ref, g_ref, w_ref, qn_ref, kn_ref, cos_ref, sin_ref, o_ref, h_scr = refs
    else:
        x_ref, mod_ref, g_ref, w_ref, qn_ref, kn_ref, o_ref, h_scr = refs
    j = pl.program_id(1)

    @pl.when(j == 0)
    def _():
        mod = mod_ref[0]
        h_scr[...] = _norm_mod(x_ref[...], g_ref[...], mod[:, :d], mod[:, d:2 * d])

    tm, tn = o_ref.shape
    cw = min(tn, MXU_COLS)

    def units(rh, epilogue):
        for r in range(tm // rh):
            rows = slice(r * rh, (r + 1) * rh)
            h = h_scr[rows, :]
            for c in range(tn // cw):
                cols = slice(c * cw, (c + 1) * cw)
                o_ref[rows, cols] = epilogue(_dot(h, w_ref[:, cols]), rows).astype(BF16)

    def head_epilogue(gn_ref, scale):
        def epilogue(acc, rows):
            cos = cos_ref[rows, :] if rope else None
            sin = sin_ref[rows, :] if rope else None
            return _head_norm_rope(acc, gn_ref[...], cos, sin, group, scale)
        return epilogue

    @pl.when(j < q_tiles)
    def _():
        units(min(tm, EPILOGUE_ROWS), head_epilogue(qn_ref, q_scale))

    @pl.when((j >= q_tiles) & (j < q_tiles + k_tiles))
    def _():
        units(min(tm, EPILOGUE_ROWS), head_epilogue(kn_ref, 1.0))

    @pl.when(j >= q_tiles + k_tiles)
    def _():
        units(tm, lambda acc, rows: acc)


def _rope_tables(seq, group):
    rows = seq // GRID_W
    row = jnp.repeat(jnp.arange(rows), GRID_W).astype(F32)
    col = jnp.tile(jnp.arange(GRID_W), rows).astype(F32)
    n_freq = group // 4
    inv_freq = ROPE_BASE ** (-(jnp.arange(n_freq, dtype=F32) / n_freq))
    ar, ac = row[:, None] * inv_freq, col[:, None] * inv_freq
    cos = jnp.concatenate([jnp.cos(ar), jnp.cos(ar), jnp.cos(ac), jnp.cos(ac)], axis=-1)
    sin = jnp.concatenate([-jnp.sin(ar), jnp.sin(ar), -jnp.sin(ac), jnp.sin(ac)], axis=-1)
    reps = LANES // group
    return jnp.tile(cos, (1, reps)), jnp.tile(sin, (1, reps))


def _proj_layer(x2, stream, mod, norm_g, w_in, q_norm, k_norm, q_cols, k_cols, rope, q_scale):
    m, d = x2.shape
    n = w_in.shape[1]
    group = q_norm.shape[0]
    tm = min(stream.span, WIDE_ROW_TILE)
    tn = math.gcd(math.gcd(q_cols, k_cols), 1024)
    reps = LANES // group
    args = [x2, mod, norm_g.reshape(1, d), w_in,
            jnp.tile(q_norm, reps).reshape(1, LANES), jnp.tile(k_norm, reps).reshape(1, LANES)]
    in_specs = [
        pl.BlockSpec((tm, d), lambda i, j: (i, 0)),
        pl.BlockSpec((1, 1, 3 * d), lambda i, j: (stream.mod_row(i * tm), 0, 0)),
        pl.BlockSpec((1, d), lambda i, j: (0, 0)),
        pl.BlockSpec((d, tn), lambda i, j: (0, j)),
        pl.BlockSpec((1, LANES), lambda i, j: (0, 0)),
        pl.BlockSpec((1, LANES), lambda i, j: (0, 0)),
    ]
    if rope:
        tiles_per_seq = stream.seq // tm
        args += list(_rope_tables(stream.seq, group))
        in_specs += [pl.BlockSpec((tm, LANES), lambda i, j: (i % tiles_per_seq, 0))] * 2
    return pl.pallas_call(
        functools.partial(_proj_kernel, d=d, q_tiles=q_cols // tn, k_tiles=k_cols // tn, group=group,
                          rope=rope, q_scale=q_scale),
        grid=(m // tm, n // tn),
        in_specs=in_specs,
        out_specs=pl.BlockSpec((tm, tn), lambda i, j: (i, j)),
        out_shape=jax.ShapeDtypeStruct((m, n), BF16),
        scratch_shapes=[pltpu.VMEM((tm, d), BF16)],
        compiler_params=_params("parallel", "arbitrary"),
        name="proj_layer",
    )(*args)


def _out_kernel(a_ref, w_ref, x_ref, gate_ref, o_ref):
    o_ref[...] = x_ref[...] + gate_ref[0] * _dot(a_ref[...], w_ref[...])


def _out_layer(a2, x2, stream, mod, w_out):
    m, d = x2.shape
    di = a2.shape[1]
    tm = min(stream.span, WIDE_ROW_TILE)
    tn = _tile(d, 1024)
    gate_block0 = 2 * d // tn
    return pl.pallas_call(
        _out_kernel,
        grid=(m // tm, d // tn),
        in_specs=[
            pl.BlockSpec((tm, di), lambda i, j: (i, 0)),
            pl.BlockSpec((di, tn), lambda i, j: (0, j)),
            pl.BlockSpec((tm, tn), lambda i, j: (i, j)),
            pl.BlockSpec((1, 1, tn), lambda i, j: (stream.mod_row(i * tm), 0, gate_block0 + j)),
        ],
        out_specs=pl.BlockSpec((tm, tn), lambda i, j: (i, j)),
        out_shape=jax.ShapeDtypeStruct((m, d), F32),
        compiler_params=_params("parallel", "arbitrary"),
        name="out_layer",
    )(a2, w_out, x2, mod)


def _lane_chunks(t):
    return [t[:, c * LANES:(c + 1) * LANES] for c in range(t.shape[1] // LANES)]


def _diff_attn_kernel(*refs, tq, tk, n_chunks, has_ctx, tiles, rounds, lam_init):
    pipelined = rounds > 1
    refs = iter(refs)
    lam_ref, q_ref = next(refs), next(refs)
    qn_ref = kn_ref = kcn_ref = kc_ref = vc_ref = vct_scr = None
    if pipelined:
        qn_ref, kn_ref = next(refs), next(refs)
        if has_ctx:
            kcn_ref = next(refs)
    k_ref, v_ref = next(refs), next(refs)
    if has_ctx:
        kc_ref, vc_ref = next(refs), next(refs)
    z_ref, sn_ref, o_ref, vt_scr = next(refs), next(refs), next(refs), next(refs)
    if has_ctx:
        vct_scr = next(refs)
    bufs = list(refs)
    per_set = 2 + has_ctx + pipelined
    sets = []
    for p in range(len(bufs) // per_set):
        group = bufs[p * per_set:(p + 1) * per_set]
        sets.append((group[0], group[1] if has_ctx else None, group[1 + has_ctx], group[-1] if pipelined else None))
    half = LANES // 2

    def transpose_values():
        for c in range(n_chunks):
            vt_scr[:, c * tk:(c + 1) * tk] = v_ref[c * tk:(c + 1) * tk, :].astype(F32).T.astype(BF16)
        if has_ctx:
            vct_scr[...] = vc_ref[...].astype(F32).T.astype(BF16)

    def sublane_groups(s):
        return s.reshape(s.shape[0] // SUBLANES, SUBLANES, s.shape[1])

    def scores(q, buf, t, k_ref=k_ref, kc_ref=kc_ref):
        s_buf, c_buf, m_buf, _ = buf
        lane = lax.broadcasted_iota(jnp.int32, q.shape, 1)
        zero = jnp.zeros_like(q)
        subs = (jnp.where(lane < half, q, zero), jnp.where(lane < half, zero, q))
        for h, qh in enumerate(subs):
            cols = slice(h * tq, (h + 1) * tq)
            mx = jnp.full((SUBLANES, tq), NEG_INF, F32)
            for c in range(n_chunks):
                s = _dot_nt(k_ref[c * tk:(c + 1) * tk, :], qh)
                s_buf[t, c, :, cols] = s
                mx = jnp.maximum(mx, jnp.max(sublane_groups(s), axis=0))
            if has_ctx:
                s = _dot_nt(kc_ref[...], qh)
                c_buf[t, :, cols] = s
                mx = jnp.maximum(mx, jnp.max(sublane_groups(s), axis=0))
            m_buf[t, :, cols] = jnp.broadcast_to(jnp.max(mx, axis=0, keepdims=True), (SUBLANES, tq))

    def weighted(buf, t):
        s_buf, c_buf, m_buf, _ = buf
        m = m_buf[t]
        lsum = jnp.zeros((SUBLANES, 2 * tq), F32)
        acc = jnp.zeros((LANES, 2 * tq), F32)
        blocks = [(s_buf[t, c], vt_scr[:, c * tk:(c + 1) * tk]) for c in range(n_chunks)]
        if has_ctx:
            blocks.append((c_buf[t], vct_scr[...]))
        for s, vt in blocks:
            p = jnp.exp2(sublane_groups(s) - m[None])
            lsum = lsum + jnp.sum(p, axis=0)
            acc = acc + _dot(vt, p.reshape(s.shape).astype(BF16))
        return acc / jnp.sum(lsum, axis=0, keepdims=True)

    lv = lam_ref[...]
    lam = (jnp.exp(jnp.sum(lv[0:1] * lv[1:2], axis=-1, keepdims=True))
           - jnp.exp(jnp.sum(lv[2:3] * lv[3:4], axis=-1, keepdims=True)) + lam_init)

    def finish(o_all, z):
        o = (o_all[:, :tq] - lam * o_all[:, tq:]).T
        ms = jnp.mean(o * o, axis=-1, keepdims=True)
        y = o * lax.rsqrt(ms + EPS) * sn_ref[...] * (1.0 - lam_init)
        return (y * jax.nn.silu(z.astype(F32))).astype(BF16)

    transpose_values()
    if not pipelined:
        scores(q_ref[...], sets[0], 0)
        o_ref[...] = finish(weighted(sets[0], 0), z_ref[...])
        return

    rows = tiles * tq

    @pl.when((pl.program_id(0) == 0) & (pl.program_id(1) == 0))
    def _():
        sets[0][3][...] = jnp.zeros_like(sets[0][3])
        for t in range(tiles):
            scores(q_ref[t * tq:(t + 1) * tq, :], sets[0], t)

    def both(q, k_ref, kc_ref, cur, nxt, t):
        s_nxt, c_nxt, m_nxt, _ = nxt
        s_cur, c_cur, m_cur, _ = cur
        lane = lax.broadcasted_iota(jnp.int32, q.shape, 1)
        zero = jnp.zeros_like(q)
        subs = (jnp.where(lane < half, q, zero), jnp.where(lane < half, zero, q))
        mxs = [jnp.full((SUBLANES, tq), NEG_INF, F32) for _ in subs]
        m = m_cur[t]
        lsum = jnp.zeros((SUBLANES, 2 * tq), F32)
        acc = jnp.zeros((LANES, 2 * tq), F32)
        tie = None
        for c in range(n_chunks + has_ctx):
            if c < n_chunks:
                k, vt, s_old = k_ref[c * tk:(c + 1) * tk, :], vt_scr[:, c * tk:(c + 1) * tk], s_cur[t, c]
            else:
                k, vt, s_old = kc_ref[...], vct_scr[...], c_cur[t]
            if tie is not None:
                k = k + jnp.tile(tie, (k.shape[0] // SUBLANES, 1)).astype(BF16)
            for h, qh in enumerate(subs):
                cols = slice(h * tq, (h + 1) * tq)
                s = _dot_nt(k, qh)
                if c < n_chunks:
                    s_nxt[t, c, :, cols] = s
                else:
                    c_nxt[t, :, cols] = s
                mxs[h] = jnp.maximum(mxs[h], jnp.max(sublane_groups(s), axis=0))
            p = jnp.exp2(sublane_groups(s_old) - m[None])
            lsum = lsum + jnp.sum(p, axis=0)
            acc = acc + _dot(vt, p.reshape(s_old.shape).astype(BF16))
            bits = pltpu.bitcast(p[-1][:, :LANES], jnp.uint32)
            tie = pltpu.bitcast((bits >> 16) >> 16, F32)
        for h, mx in enumerate(mxs):
            m_nxt[t, :, h * tq:(h + 1) * tq] = jnp.broadcast_to(jnp.max(mx, axis=0, keepdims=True), (SUBLANES, tq))
        return acc / jnp.sum(lsum, axis=0, keepdims=True)

    def round_(r, cur, nxt, next_q, k_ref, kc_ref):
        prev = jnp.maximum(r - 1, 0) * rows
        for t in range(tiles):
            out_rows = pl.ds(pl.multiple_of(prev + t * tq, tq), tq)
            o_ref[out_rows, :] = finish(cur[3][t], z_ref[out_rows, :])
            nxt[3][t] = both(next_q(t), k_ref, kc_ref, cur, nxt, t)

    @pl.loop(0, rounds - 1)
    def _(r):
        def next_q(t):
            return q_ref[pl.ds(pl.multiple_of((r + 1) * rows + t * tq, tq), tq), :]

        @pl.when((r & 1) == 0)
        def _():
            round_(r, sets[0], sets[1], next_q, k_ref, kc_ref)

        @pl.when((r & 1) == 1)
        def _():
            round_(r, sets[1], sets[0], next_q, k_ref, kc_ref)

    last = rounds - 1
    cur, nxt = sets[last % 2], sets[1 - last % 2]
    round_(last, cur, nxt, lambda t: qn_ref[t * tq:(t + 1) * tq, :], kn_ref, kcn_ref)
    for t in range(tiles):
        out_rows = slice(last * rows + t * tq, last * rows + (t + 1) * tq)
        o_ref[out_rows, :] = finish(nxt[3][t], z_ref[out_rows, :])


def _diff_attention(qkvz, seq, batch, ctx_qkvz, ctx_len, lam_vecs, sub_norm, lam_init):
    di = qkvz.shape[1] // 4
    heads = di // LANES
    tq = min(seq, 256)
    tk = min(seq, 512)
    tiles = 2 if seq >= 2 * tq else 1
    rounds = seq // (tiles * tq)
    assert rounds == 1 or rounds % 2 == 0
    has_ctx = ctx_qkvz is not None
    pipelined = rounds > 1

    def next_head(index):
        def index_map(b, h):
            wrap = h + 1 == heads
            return index(jnp.minimum(b + wrap, batch - 1), jnp.where(wrap, 0, h + 1))
        return index_map

    head_spec = lambda col0: pl.BlockSpec((seq, LANES), lambda b, h: (b, col0 + h))
    ctx_spec = lambda col0: pl.BlockSpec((ctx_len, LANES), lambda b, h: (b, col0 + h))
    args = [lam_vecs, qkvz]
    in_specs = [pl.BlockSpec(lam_vecs.shape, lambda b, h: (0, 0)), head_spec(0)]
    if pipelined:
        args += [qkvz, qkvz]
        in_specs += [pl.BlockSpec((tiles * tq, LANES), next_head(lambda b, h: (b * rounds, h))),
                     pl.BlockSpec((seq, LANES), next_head(lambda b, h: (b, heads + h)))]
        if has_ctx:
            args += [ctx_qkvz]
            in_specs += [pl.BlockSpec((ctx_len, LANES), next_head(lambda b, h: (b, heads + h)))]
    args += [qkvz, qkvz]
    in_specs += [head_spec(heads), head_spec(2 * heads)]
    if has_ctx:
        args += [ctx_qkvz, ctx_qkvz]
        in_specs += [ctx_spec(heads), ctx_spec(2 * heads)]
    args += [qkvz, sub_norm.reshape(1, LANES)]
    in_specs += [head_spec(3 * heads), pl.BlockSpec((1, LANES), lambda b, h: (0, 0))]
    scratch = [pltpu.VMEM((LANES, seq), BF16)] + ([pltpu.VMEM((LANES, ctx_len), BF16)] if has_ctx else [])
    for _ in range(2 if pipelined else 1):
        scratch += [pltpu.VMEM((tiles, seq // tk, tk, 2 * tq), F32)]
        if has_ctx:
            scratch += [pltpu.VMEM((tiles, ctx_len, 2 * tq), F32)]
        scratch += [pltpu.VMEM((tiles, SUBLANES, 2 * tq), F32)]
        if pipelined:
            scratch += [pltpu.VMEM((tiles, LANES, 2 * tq), F32)]
    return pl.pallas_call(
        functools.partial(_diff_attn_kernel, tq=tq, tk=tk, n_chunks=seq // tk, has_ctx=has_ctx, tiles=tiles,
                          rounds=rounds, lam_init=lam_init),
        grid=(batch, heads),
        in_specs=in_specs,
        out_specs=head_spec(0),
        out_shape=jax.ShapeDtypeStruct((batch * seq, di), BF16),
        scratch_shapes=scratch,
        compiler_params=_params("arbitrary", "arbitrary"),
        name="diff_attention",
    )(*args)


def _win_attn_kernel(*refs, group, kv_heads, n_z):
    q_ref, kp_ref, kc_ref, kn_ref, vp_ref, vc_ref, vn_ref, kx_ref, vx_ref, sink_ref = refs[:10]
    z_refs = refs[10:10 + n_z]
    o_ref = refs[10 + n_z]
    i = pl.program_id(1)
    blk = ATTN_BLOCK
    rows = group * blk
    zw = z_refs[0].shape[1]
    qpos = lax.broadcasted_iota(jnp.int32, (rows, blk), 0) % blk
    kpos = lax.broadcasted_iota(jnp.int32, (rows, blk), 1)
    keep_prev = (kpos >= qpos) & (i > 0)
    keep_next = (kpos <= qpos) & (i < pl.num_programs(1) - 1)
    sink2 = sink_ref[...] * math.log2(math.e)
    for n in range(kv_heads):
        ks = slice(n * LANES, (n + 1) * LANES)
        q4 = jnp.concatenate([q_ref[:, (n * group + g) * LANES:(n * group + g + 1) * LANES] for g in range(group)],
                             axis=0)
        k_all = jnp.concatenate([kp_ref[:, ks], kc_ref[:, ks], kn_ref[:, ks], kx_ref[:, ks]], axis=0)
        v_all = jnp.concatenate([vp_ref[:, ks], vc_ref[:, ks], vn_ref[:, ks], vx_ref[:, ks]], axis=0)
        s = _lane_chunks(_dot_nt(q4, k_all))
        s[0] = jnp.where(keep_prev, s[0], NEG_INF)
        s[2] = jnp.where(keep_next, s[2], NEG_INF)
        sink_b = jnp.concatenate([jnp.broadcast_to(sink2[n, g:g + 1, :], (blk, LANES)) for g in range(group)], axis=0)
        mx = sink_b
        for t in s:
            mx = jnp.maximum(mx, t)
        m = jnp.broadcast_to(jnp.max(mx, axis=-1, keepdims=True), (rows, LANES))
        ps = [jnp.exp2(t - m) for t in s]
        lsum = ps[0]
        for t in ps[1:]:
            lsum = lsum + t
        l = jnp.sum(lsum, axis=-1, keepdims=True) + jnp.exp2(sink_b - m)[:, 0:1]
        o = _dot(jnp.concatenate(ps, axis=1).astype(BF16), v_all) / l
        for g in range(group):
            col = (n * group + g) * LANES
            z = z_refs[col // zw][:, col % zw:col % zw + LANES].astype(F32)
            o_ref[:, col:col + LANES] = (o[g * blk:(g + 1) * blk] * jax.nn.silu(z)).astype(BF16)


def _win_attention(qkvz, seq, batch, ctx_qkvz, ctx_len, di, kv_heads, sink):
    heads = di // LANES
    group = heads // kv_heads
    kvw = kv_heads * LANES
    blk = ATTN_BLOCK
    nb = seq // blk
    z_col0 = di + 2 * kvw
    zw = math.gcd(z_col0, di)
    n_z = di // zw
    assert di % kvw == 0
    sink_rep = jnp.broadcast_to(sink.astype(F32).reshape(kv_heads, group, 1), (kv_heads, group, LANES))
    band = lambda col_block, shift: pl.BlockSpec(
        (blk, kvw), lambda b, i: (b * nb + jnp.clip(i + shift, 0, nb - 1), col_block))
    k_blk, v_blk = di // kvw, di // kvw + 1
    z_spec = lambda t: pl.BlockSpec((blk, zw), lambda b, i: (b * nb + i, z_col0 // zw + t))
    return pl.pallas_call(
        functools.partial(_win_attn_kernel, group=group, kv_heads=kv_heads, n_z=n_z),
        grid=(batch, nb),
        in_specs=[
            pl.BlockSpec((blk, di), lambda b, i: (b * nb + i, 0)),
            band(k_blk, -1), band(k_blk, 0), band(k_blk, 1),
            band(v_blk, -1), band(v_blk, 0), band(v_blk, 1),
            pl.BlockSpec((ctx_len, kvw), lambda b, i: (b, k_blk)),
            pl.BlockSpec((ctx_len, kvw), lambda b, i: (b, v_blk)),
            pl.BlockSpec((kv_heads, group, LANES), lambda b, i: (0, 0, 0)),
        ] + [z_spec(t) for t in range(n_z)],
        out_specs=pl.BlockSpec((blk, di), lambda b, i: (b * nb + i, 0)),
        out_shape=jax.ShapeDtypeStruct((batch * seq, di), BF16),
        compiler_params=_params("parallel", "arbitrary"),
        name="win_attention",
    )(*([qkvz] * 7 + [ctx_qkvz, ctx_qkvz, sink_rep] + [qkvz] * n_z))


class _Stream(NamedTuple):
    seq: int
    span: int
    mod_row: Callable


def kernel(x, c, ctx, c_ctx, l0_norm, l0_w_mod, l0_b_mod, l0_w_in, l0_conv_w, l0_conv_b, l0_w_out, l1_norm, l1_w_mod, l1_b_mod, l1_w_in, l1_q_norm, l1_k_norm, l1_lam_q1, l1_lam_k1, l1_lam_q2, l1_lam_k2, l1_sub_norm, l1_w_out, l2_norm, l2_w_mod, l2_b_mod, l2_w_in, l2_q_norm, l2_k_norm, l2_sink, l2_w_out, l3_norm, l3_w_mod, l3_b_mod, l3_w_in, l3_conv_w, l3_conv_b, l3_w_out):
    batch, seq, d = x.shape
    ctx_len = ctx.shape[1]
    di = l0_w_out.shape[0]
    assert batch < MOD_ROWS
    xs = _Stream(seq, seq, lambda row: row // seq)
    cs = _Stream(ctx_len, batch * ctx_len, lambda row: batch)

    x2 = x.reshape(batch * seq, d)
    c2 = ctx.reshape(batch * ctx_len, d)
    cc = jnp.zeros((MOD_ROWS, d), F32).at[:batch].set(c).at[batch].set(c_ctx)
    mod0, mod1, mod2, mod3 = (_modulation(cc, w, b) for w, b in
                              ((l0_w_mod, l0_b_mod), (l1_w_mod, l1_b_mod), (l2_w_mod, l2_b_mod), (l3_w_mod, l3_b_mod)))
    bf = lambda w: w.astype(BF16)

    w_in, w_out = bf(l0_w_in), bf(l0_w_out)
    x2n = _conv_layer(x2, xs, mod0, l0_norm, w_in, l0_conv_w, l0_conv_b, w_out)
    c2 = _conv_layer(c2, cs, mod0, l0_norm, w_in, l0_conv_w, l0_conv_b, w_out)
    x2 = x2n

    w_in, w_out = bf(l1_w_in), bf(l1_w_out)
    lam_init = 0.8 - 0.6 * math.exp(-0.3 * 1)
    q_scale = l1_q_norm.shape[0] ** -0.5 * math.log2(math.e)
    proj = functools.partial(_proj_layer, norm_g=l1_norm, w_in=w_in, q_norm=l1_q_norm, k_norm=l1_k_norm,
                             q_cols=di, k_cols=di, q_scale=q_scale)
    qx = proj(x2, xs, mod1, rope=True)
    qc = proj(c2, cs, mod1, rope=False)
    lam_vecs = jnp.stack([l1_lam_q1, l1_lam_k1, l1_lam_q2, l1_lam_k2]).astype(F32)
    ax = _diff_attention(qx, seq, batch, qc, ctx_len, lam_vecs, l1_sub_norm, lam_init)
    ac = _diff_attention(qc, ctx_len, batch, None, 0, lam_vecs, l1_sub_norm, lam_init)
    x2 = _out_layer(ax, x2, xs, mod1, w_out)
    c2 = _out_layer(ac, c2, cs, mod1, w_out)

    w_in, w_out = bf(l2_w_in), bf(l2_w_out)
    kv_cols = (l2_w_in.shape[1] - 2 * di) // 2
    proj = functools.partial(_proj_layer, norm_g=l2_norm, w_in=w_in, q_norm=l2_q_norm, k_norm=l2_k_norm,
                             q_cols=di, k_cols=kv_cols, q_scale=l2_q_norm.shape[0] ** -0.5 * math.log2(math.e))
    qx = proj(x2, xs, mod2, rope=True)
    qc = proj(c2, cs, mod2, rope=False)
    ax = _win_attention(qx, seq, batch, qc, ctx_len, di, kv_cols // LANES, l2_sink)
    x2 = _out_layer(ax, x2, xs, mod2, w_out)

    x2 = _conv_layer(x2, xs, mod3, l3_norm, bf(l3_w_in), l3_conv_w, l3_conv_b, bf(l3_w_out))
    return x2.reshape(batch, seq, d)
```

```python
import functools
import math
from typing import Callable, NamedTuple

import jax
import jax.numpy as jnp
from jax import lax
from jax.experimental import pallas as pl
from jax.experimental.pallas import tpu as pltpu

LANES = 128
SUBLANES = 8
MXU_COLS = 256
BF16_ROWS = 16
GRID_W = 64
ROPE_BASE = 10000.0
EPS = 1e-6
NEG_INF = -1e30
ATTN_BLOCK = 128
EPILOGUE_ROWS = 256
ROW_TILE = 512
WIDE_ROW_TILE = 1024
MOD_ROWS = 8
VMEM_LIMIT = 56 * 1024 * 1024

F32 = jnp.float32
BF16 = jnp.bfloat16


def _params(*sem):
    return pltpu.CompilerParams(dimension_semantics=sem, vmem_limit_bytes=VMEM_LIMIT)


def _tile(n, target):
    if n <= target:
        return n
    t = target - target % LANES
    while n % t:
        t -= LANES
    return t


def _div_mod(x, n):
    if n & (n - 1) == 0:
        return lax.shift_right_logical(x, n.bit_length() - 1), x & (n - 1)
    return x // n, x % n


def _dot(a, b):
    return jnp.dot(a, b, preferred_element_type=F32)


def _dot_nt(a, b):
    return lax.dot_general(a, b, (((1,), (1,)), ((), ())), preferred_element_type=F32)


def _norm_mod(t, g, shift, scale):
    ms = jnp.mean(t * t, axis=-1, keepdims=True)
    y = t * lax.rsqrt(ms + EPS) * g
    return (y * (1.0 + scale) + shift).astype(BF16)


def _mod_kernel(c_ref, w_ref, b_ref, o_ref):
    a = jax.nn.silu(c_ref[...]).astype(BF16)
    o_ref[...] = _dot(a, w_ref[...].astype(BF16)) + b_ref[...]


def _modulation(cc, w_mod, b_mod):
    d, n = w_mod.shape
    tn = _tile(n, 768)
    out = pl.pallas_call(
        _mod_kernel,
        grid=(n // tn,),
        in_specs=[
            pl.BlockSpec((MOD_ROWS, d), lambda j: (0, 0)),
            pl.BlockSpec((d, tn), lambda j: (0, j)),
            pl.BlockSpec((1, tn), lambda j: (0, j)),
        ],
        out_specs=pl.BlockSpec((MOD_ROWS, tn), lambda j: (0, j)),
        out_shape=jax.ShapeDtypeStruct((MOD_ROWS, n), F32),
        compiler_params=_params("arbitrary"),
        name="modulation",
    )(cc, w_mod, b_mod.reshape(1, n))
    return out.reshape(MOD_ROWS, 1, n)


def _conv_kernel(xp_ref, x_ref, xn_ref, mod_ref, g_ref, wb_ref, wc_ref, wx_ref, wz_ref, cw_ref, cb_ref,
                 wo_ref, o_ref, h_scr, *, tm, d, seq):
    i = pl.program_id(0)
    j = pl.program_id(1)
    halo = BF16_ROWS
    mod = mod_ref[0]
    shift, scale, gate = mod[:, :d], mod[:, d:2 * d], mod[:, 2 * d:]

    @pl.when(j == 0)
    def _():
        g = g_ref[...]
        h_scr[0:halo, :] = _norm_mod(xp_ref[...], g, shift, scale)
        h_scr[halo:halo + tm, :] = _norm_mod(x_ref[...], g, shift, scale)
        h_scr[halo + tm:, :] = _norm_mod(xn_ref[...], g, shift, scale)
        o_ref[...] = jnp.zeros_like(o_ref)

    n = tm + 2 * halo
    h = h_scr[...]
    hm = h_scr[halo:halo + tm, :]
    u = _dot(h, wc_ref[...]) * _dot(h, wx_ref[...])
    pos = lax.rem(i * tm + lax.broadcasted_iota(jnp.int32, (tm, 1), 0), seq)
    u_prev = jnp.where(pos == 0, 0.0, pltpu.roll(u, 1, 0)[halo:halo + tm])
    u_next = jnp.where(pos == seq - 1, 0.0, pltpu.roll(u, n - 1, 0)[halo:halo + tm])
    cw = cw_ref[...]
    y = u_prev * cw[0:1] + u[halo:halo + tm] * cw[1:2] + u_next * cw[2:3] + cb_ref[...]
    gated = _dot(hm, wb_ref[...]) * y * jax.nn.silu(_dot(hm, wz_ref[...]))
    o_ref[...] += _dot(gated.astype(BF16), wo_ref[...])

    @pl.when(j == pl.num_programs(1) - 1)
    def _():
        o_ref[...] = x_ref[...] + gate * o_ref[...]


def _conv_layer(x2, stream, mod, norm_g, w_in, conv_w, conv_b, w_out):
    m, d = x2.shape
    di = w_out.shape[0]
    tm = min(stream.span, ROW_TILE)
    tn = _tile(di, 512)
    nj = di // tn
    hb = tm // BF16_ROWS
    last_hb = m // BF16_ROWS - 1
    w_spec = lambda c: pl.BlockSpec((d, tn), lambda i, j: (0, c * nj + j))
    return pl.pallas_call(
        functools.partial(_conv_kernel, tm=tm, d=d, seq=stream.seq),
        grid=(m // tm, nj),
        in_specs=[
            pl.BlockSpec((BF16_ROWS, d), lambda i, j: (jnp.maximum(i * hb - 1, 0), 0)),
            pl.BlockSpec((tm, d), lambda i, j: (i, 0)),
            pl.BlockSpec((BF16_ROWS, d), lambda i, j: (jnp.minimum((i + 1) * hb, last_hb), 0)),
            pl.BlockSpec((1, 1, 3 * d), lambda i, j: (stream.mod_row(i * tm), 0, 0)),
            pl.BlockSpec((1, d), lambda i, j: (0, 0)),
            w_spec(0), w_spec(1), w_spec(2), w_spec(3),
            pl.BlockSpec((3, tn), lambda i, j: (0, j)),
            pl.BlockSpec((1, tn), lambda i, j: (0, j)),
            pl.BlockSpec((tn, d), lambda i, j: (j, 0)),
        ],
        out_specs=pl.BlockSpec((tm, d), lambda i, j: (i, 0)),
        out_shape=jax.ShapeDtypeStruct((m, d), F32),
        scratch_shapes=[pltpu.VMEM((tm + 2 * BF16_ROWS, d), BF16)],
        compiler_params=_params("parallel", "arbitrary"),
        name="conv_layer",
    )(x2, x2, x2, mod, norm_g.reshape(1, d), w_in, w_in, w_in, w_in, conv_w, conv_b.reshape(1, di), w_out)


def _head_norm_rope(acc, gn, cos, sin, group, out_scale):
    outs = []
    swap = group // 4
    for c in range(acc.shape[1] // LANES):
        t = acc[:, c * LANES:(c + 1) * LANES]
        lane = lax.broadcasted_iota(jnp.int32, t.shape, 1)
        sq = t * t
        if group == LANES:
            ms = jnp.mean(sq, axis=-1, keepdims=True)
        else:
            lo = lane < group
            s_lo = jnp.sum(jnp.where(lo, sq, 0.0), axis=-1, keepdims=True)
            s_hi = jnp.sum(jnp.where(lo, 0.0, sq), axis=-1, keepdims=True)
            ms = jnp.where(lo, s_lo, s_hi) * (1.0 / group)
        y = t * lax.rsqrt(ms + EPS) * gn
        if cos is not None:
            ahead = pltpu.roll(y, LANES - swap, 1)
            behind = pltpu.roll(y, swap, 1)
            y = y * cos + jnp.where((lane & swap) == 0, ahead, behind) * sin
        if out_scale != 1.0:
            y = y * out_scale
        outs.append(y)
    return outs[0] if len(outs) == 1 else jnp.concatenate(outs, axis=1)


def _proj_kernel(*refs, d, q_tiles, k_tiles, group, rope, q_scale):
    if rope:
        x_ref, mod_ref, g_ref, w_ref, qn_ref, kn_ref, cos_ref, sin_ref, o_ref, h_scr = refs
    else:
        x_ref, mod_ref, g_ref, w_ref, qn_ref, kn_ref, o_ref, h_scr = refs
    j = pl.program_id(1)

    @pl.when(j == 0)
    def _():
        mod = mod_ref[0]
        h_scr[...] = _norm_mod(x_ref[...], g_ref[...], mod[:, :d], mod[:, d:2 * d])

    tm, tn = o_ref.shape
    cw = min(tn, MXU_COLS)

    def units(rh, epilogue):
        for r in range(tm // rh):
            rows = slice(r * rh, (r + 1) * rh)
            h = h_scr[rows, :]
            for c in range(tn // cw):
                cols = slice(c * cw, (c + 1) * cw)
                o_ref[rows, cols] = epilogue(_dot(h, w_ref[:, cols]), rows).astype(BF16)

    def head_epilogue(gn_ref, scale):
        def epilogue(acc, rows):
            cos = cos_ref[rows, :] if rope else None
            sin = sin_ref[rows, :] if rope else None
            return _head_norm_rope(acc, gn_ref[...], cos, sin, group, scale)
        return epilogue

    @pl.when(j < q_tiles)
    def _():
        units(min(tm, EPILOGUE_ROWS), head_epilogue(qn_ref, q_scale))

    @pl.when((j >= q_tiles) & (j < q_tiles + k_tiles))
    def _():
        units(min(tm, EPILOGUE_ROWS), head_epilogue(kn_ref, 1.0))

    @pl.when(j >= q_tiles + k_tiles)
    def _():
        units(tm, lambda acc, rows: acc)


def _rope_tables(seq, group):
    rows = seq // GRID_W
    row = jnp.repeat(jnp.arange(rows), GRID_W).astype(F32)
    col = jnp.tile(jnp.arange(GRID_W), rows).astype(F32)
    n_freq = group // 4
    inv_freq = ROPE_BASE ** (-(jnp.arange(n_freq, dtype=F32) / n_freq))
    ar, ac = row[:, None] * inv_freq, col[:, None] * inv_freq
    cos = jnp.concatenate([jnp.cos(ar), jnp.cos(ar), jnp.cos(ac), jnp.cos(ac)], axis=-1)
    sin = jnp.concatenate([-jnp.sin(ar), jnp.sin(ar), -jnp.sin(ac), jnp.sin(ac)], axis=-1)
    reps = LANES // group
    return jnp.tile(cos, (1, reps)), jnp.tile(sin, (1, reps))


def _proj_layer(x2, stream, mod, norm_g, w_in, q_norm, k_norm, q_cols, k_cols, rope, q_scale):
    m, d = x2.shape
    n = w_in.shape[1]
    group = q_norm.shape[0]
    tm = min(stream.span, WIDE_ROW_TILE)
    tn = math.gcd(math.gcd(q_cols, k_cols), 1024)
    reps = LANES // group
    args = [x2, mod, norm_g.reshape(1, d), w_in,
            jnp.tile(q_norm, reps).reshape(1, LANES), jnp.tile(k_norm, reps).reshape(1, LANES)]
    in_specs = [
        pl.BlockSpec((tm, d), lambda i, j: (i, 0)),
        pl.BlockSpec((1, 1, 3 * d), lambda i, j: (stream.mod_row(i * tm), 0, 0)),
        pl.BlockSpec((1, d), lambda i, j: (0, 0)),
        pl.BlockSpec((d, tn), lambda i, j: (0, j)),
        pl.BlockSpec((1, LANES), lambda i, j: (0, 0)),
        pl.BlockSpec((1, LANES), lambda i, j: (0, 0)),
    ]
    if rope:
        tiles_per_seq = stream.seq // tm
        args += list(_rope_tables(stream.seq, group))
        in_specs += [pl.BlockSpec((tm, LANES), lambda i, j: (i % tiles_per_seq, 0))] * 2
    return pl.pallas_call(
        functools.partial(_proj_kernel, d=d, q_tiles=q_cols // tn, k_tiles=k_cols // tn, group=group,
                          rope=rope, q_scale=q_scale),
        grid=(m // tm, n // tn),
        in_specs=in_specs,
        out_specs=pl.BlockSpec((tm, tn), lambda i, j: (i, j)),
        out_shape=jax.ShapeDtypeStruct((m, n), BF16),
        scratch_shapes=[pltpu.VMEM((tm, d), BF16)],
        compiler_params=_params("parallel", "arbitrary"),
        name="proj_layer",
    )(*args)


def _out_kernel(a_ref, w_ref, x_ref, gate_ref, o_ref):
    o_ref[...] = x_ref[...] + gate_ref[0] * _dot(a_ref[...], w_ref[...])


def _out_layer(a2, x2, stream, mod, w_out):
    m, d = x2.shape
    di = a2.shape[1]
    tm = min(stream.span, WIDE_ROW_TILE)
    tn = _tile(d, 1024)
    gate_block0 = 2 * d // tn
    return pl.pallas_call(
        _out_kernel,
        grid=(m // tm, d // tn),
        in_specs=[
            pl.BlockSpec((tm, di), lambda i, j: (i, 0)),
            pl.BlockSpec((di, tn), lambda i, j: (0, j)),
            pl.BlockSpec((tm, tn), lambda i, j: (i, j)),
            pl.BlockSpec((1, 1, tn), lambda i, j: (stream.mod_row(i * tm), 0, gate_block0 + j)),
        ],
        out_specs=pl.BlockSpec((tm, tn), lambda i, j: (i, j)),
        out_shape=jax.ShapeDtypeStruct((m, d), F32),
        compiler_params=_params("parallel", "arbitrary"),
        name="out_layer",
    )(a2, w_out, x2, mod)


def _lane_chunks(t):
    return [t[:, c * LANES:(c + 1) * LANES] for c in range(t.shape[1] // LANES)]


def _diff_attn_kernel(*refs, tq, tk, n_chunks, has_ctx, tiles, rounds, lam_init):
    pipelined = rounds > 1
    refs = iter(refs)
    lam_ref, q_ref = next(refs), next(refs)
    qn_ref = kn_ref = kcn_ref = kc_ref = vc_ref = vct_scr = None
    if pipelined:
        qn_ref, kn_ref = next(refs), next(refs)
        if has_ctx:
            kcn_ref = next(refs)
    k_ref, v_ref = next(refs), next(refs)
    if has_ctx:
        kc_ref, vc_ref = next(refs), next(refs)
    z_ref, sn_ref, o_ref, vt_scr = next(refs), next(refs), next(refs), next(refs)
    if has_ctx:
        vct_scr = next(refs)
    bufs = list(refs)
    per_set = 2 + has_ctx + pipelined
    sets = []
    for p in range(len(bufs) // per_set):
        group = bufs[p * per_set:(p + 1) * per_set]
        sets.append((group[0], group[1] if has_ctx else None, group[1 + has_ctx], group[-1] if pipelined else None))
    half = LANES // 2

    def transpose_values():
        for c in range(n_chunks):
            vt_scr[:, c * tk:(c + 1) * tk] = v_ref[c * tk:(c + 1) * tk, :].astype(F32).T.astype(BF16)
        if has_ctx:
            vct_scr[...] = vc_ref[...].astype(F32).T.astype(BF16)

    def sublane_groups(s):
        return s.reshape(s.shape[0] // SUBLANES, SUBLANES, s.shape[1])

    def scores(q, buf, t, k_ref=k_ref, kc_ref=kc_ref):
        s_buf, c_buf, m_buf, _ = buf
        lane = lax.broadcasted_iota(jnp.int32, q.shape, 1)
        zero = jnp.zeros_like(q)
        subs = (jnp.where(lane < half, q, zero), jnp.where(lane < half, zero, q))
        for h, qh in enumerate(subs):
            cols = slice(h * tq, (h + 1) * tq)
            mx = jnp.full((SUBLANES, tq), NEG_INF, F32)
            for c in range(n_chunks):
                s = _dot_nt(k_ref[c * tk:(c + 1) * tk, :], qh)
                s_buf[t, c, :, cols] = s
                mx = jnp.maximum(mx, jnp.max(sublane_groups(s), axis=0))
            if has_ctx:
                s = _dot_nt(kc_ref[...], qh)
                c_buf[t, :, cols] = s
                mx = jnp.maximum(mx, jnp.max(sublane_groups(s), axis=0))
            m_buf[t, :, cols] = jnp.broadcast_to(jnp.max(mx, axis=0, keepdims=True), (SUBLANES, tq))

    def weighted(buf, t):
        s_buf, c_buf, m_buf, _ = buf
        m = m_buf[t]
        lsum = jnp.zeros((SUBLANES, 2 * tq), F32)
        acc = jnp.zeros((LANES, 2 * tq), F32)
        blocks = [(s_buf[t, c], vt_scr[:, c * tk:(c + 1) * tk]) for c in range(n_chunks)]
        if has_ctx:
            blocks.append((c_buf[t], vct_scr[...]))
        for s, vt in blocks:
            p = jnp.exp2(sublane_groups(s) - m[None])
            lsum = lsum + jnp.sum(p, axis=0)
            acc = acc + _dot(vt, p.reshape(s.shape).astype(BF16))
        return acc / jnp.sum(lsum, axis=0, keepdims=True)

    lv = lam_ref[...]
    lam = (jnp.exp(jnp.sum(lv[0:1] * lv[1:2], axis=-1, keepdims=True))
           - jnp.exp(jnp.sum(lv[2:3] * lv[3:4], axis=-1, keepdims=True)) + lam_init)

    def finish(o_all, z):
        o = (o_all[:, :tq] - lam * o_all[:, tq:]).T
        ms = jnp.mean(o * o, axis=-1, keepdims=True)
        y = o * lax.rsqrt(ms + EPS) * sn_ref[...] * (1.0 - lam_init)
        return (y * jax.nn.silu(z.astype(F32))).astype(BF16)

    transpose_values()
    if not pipelined:
        scores(q_ref[...], sets[0], 0)
        o_ref[...] = finish(weighted(sets[0], 0), z_ref[...])
        return

    rows = tiles * tq

    @pl.when((pl.program_id(0) == 0) & (pl.program_id(1) == 0))
    def _():
        sets[0][3][...] = jnp.zeros_like(sets[0][3])
        for t in range(tiles):
            scores(q_ref[t * tq:(t + 1) * tq, :], sets[0], t)

    def both(q, k_ref, kc_ref, cur, nxt, t):
        s_nxt, c_nxt, m_nxt, _ = nxt
        s_cur, c_cur, m_cur, _ = cur
        lane = lax.broadcasted_iota(jnp.int32, q.shape, 1)
        zero = jnp.zeros_like(q)
        subs = (jnp.where(lane < half, q, zero), jnp.where(lane < half, zero, q))
        mxs = [jnp.full((SUBLANES, tq), NEG_INF, F32) for _ in subs]
        m = m_cur[t]
        lsum = jnp.zeros((SUBLANES, 2 * tq), F32)
        acc = jnp.zeros((LANES, 2 * tq), F32)
        tie = None
        for c in range(n_chunks + has_ctx):
            if c < n_chunks:
                k, vt, s_old = k_ref[c * tk:(c + 1) * tk, :], vt_scr[:, c * tk:(c + 1) * tk], s_cur[t, c]
            else:
                k, vt, s_old = kc_ref[...], vct_scr[...], c_cur[t]
            if tie is not None:
                k = k + jnp.tile(tie, (k.shape[0] // SUBLANES, 1)).astype(BF16)
            for h, qh in enumerate(subs):
                cols = slice(h * tq, (h + 1) * tq)
                s = _dot_nt(k, qh)
                if c < n_chunks:
                    s_nxt[t, c, :, cols] = s
                else:
                    c_nxt[t, :, cols] = s
                mxs[h] = jnp.maximum(mxs[h], jnp.max(sublane_groups(s), axis=0))
            p = jnp.exp2(sublane_groups(s_old) - m[None])
            lsum = lsum + jnp.sum(p, axis=0)
            acc = acc + _dot(vt, p.reshape(s_old.shape).astype(BF16))
            bits = pltpu.bitcast(p[-1][:, :LANES], jnp.uint32)
            tie = pltpu.bitcast((bits >> 16) >> 16, F32)
        for h, mx in enumerate(mxs):
            m_nxt[t, :, h * tq:(h + 1) * tq] = jnp.broadcast_to(jnp.max(mx, axis=0, keepdims=True), (SUBLANES, tq))
        return acc / jnp.sum(lsum, axis=0, keepdims=True)

    def round_(r, cur, nxt, next_q, k_ref, kc_ref):
        prev = jnp.maximum(r - 1, 0) * rows
        for t in range(tiles):
            out_rows = pl.ds(pl.multiple_of(prev + t * tq, tq), tq)
            o_ref[out_rows, :] = finish(cur[3][t], z_ref[out_rows, :])
            nxt[3][t] = both(next_q(t), k_ref, kc_ref, cur, nxt, t)

    @pl.loop(0, rounds - 1)
    def _(r):
        def next_q(t):
            return q_ref[pl.ds(pl.multiple_of((r + 1) * rows + t * tq, tq), tq), :]

        @pl.when((r & 1) == 0)
        def _():
            round_(r, sets[0], sets[1], next_q, k_ref, kc_ref)

        @pl.when((r & 1) == 1)
        def _():
            round_(r, sets[1], sets[0], next_q, k_ref, kc_ref)

    last = rounds - 1
    cur, nxt = sets[last % 2], sets[1 - last % 2]
    round_(last, cur, nxt, lambda t: qn_ref[t * tq:(t + 1) * tq, :], kn_ref, kcn_ref)
    for t in range(tiles):
        out_rows = slice(last * rows + t * tq, last * rows + (t + 1) * tq)
        o_ref[out_rows, :] = finish(nxt[3][t], z_ref[out_rows, :])


def _diff_attention(qkvz, seq, batch, ctx_qkvz, ctx_len, lam_vecs, sub_norm, lam_init):
    di = qkvz.shape[1] // 4
    heads = di // LANES
    tq = min(seq, 256)
    tk = min(seq, 256)
    tiles = 2 if seq >= 2 * tq else 1
    rounds = seq // (tiles * tq)
    assert rounds == 1 or rounds % 2 == 0
    has_ctx = ctx_qkvz is not None
    pipelined = rounds > 1

    def next_head(index):
        def index_map(b, h):
            wrap = h + 1 == heads
            return index(jnp.minimum(b + wrap, batch - 1), jnp.where(wrap, 0, h + 1))
        return index_map

    head_spec = lambda col0: pl.BlockSpec((seq, LANES), lambda b, h: (b, col0 + h))
    ctx_spec = lambda col0: pl.BlockSpec((ctx_len, LANES), lambda b, h: (b, col0 + h))
    args = [lam_vecs, qkvz]
    in_specs = [pl.BlockSpec(lam_vecs.shape, lambda b, h: (0, 0)), head_spec(0)]
    if pipelined:
        args += [qkvz, qkvz]
        in_specs += [pl.BlockSpec((tiles * tq, LANES), next_head(lambda b, h: (b * rounds, h))),
                     pl.BlockSpec((seq, LANES), next_head(lambda b, h: (b, heads + h)))]
        if has_ctx:
            args += [ctx_qkvz]
            in_specs += [pl.BlockSpec((ctx_len, LANES), next_head(lambda b, h: (b, heads + h)))]
    args += [qkvz, qkvz]
    in_specs += [head_spec(heads), head_spec(2 * heads)]
    if has_ctx:
        args += [ctx_qkvz, ctx_qkvz]
        in_specs += [ctx_spec(heads), ctx_spec(2 * heads)]
    args += [qkvz, sub_norm.reshape(1, LANES)]
    in_specs += [head_spec(3 * heads), pl.BlockSpec((1, LANES), lambda b, h: (0, 0))]
    scratch = [pltpu.VMEM((LANES, seq), BF16)] + ([pltpu.VMEM((LANES, ctx_len), BF16)] if has_ctx else [])
    for _ in range(2 if pipelined else 1):
        scratch += [pltpu.VMEM((tiles, seq // tk, tk, 2 * tq), F32)]
        if has_ctx:
            scratch += [pltpu.VMEM((tiles, ctx_len, 2 * tq), F32)]
        scratch += [pltpu.VMEM((tiles, SUBLANES, 2 * tq), F32)]
        if pipelined:
            scratch += [pltpu.VMEM((tiles, LANES, 2 * tq), F32)]
    return pl.pallas_call(
        functools.partial(_diff_attn_kernel, tq=tq, tk=tk, n_chunks=seq // tk, has_ctx=has_ctx, tiles=tiles,
                          rounds=rounds, lam_init=lam_init),
        grid=(batch, heads),
        in_specs=in_specs,
        out_specs=head_spec(0),
        out_shape=jax.ShapeDtypeStruct((batch * seq, di), BF16),
        scratch_shapes=scratch,
        compiler_params=_params("arbitrary", "arbitrary"),
        name="diff_attention",
    )(*args)


def _win_attn_kernel(*refs, group, kv_heads, n_z):
    q_ref, kp_ref, kc_ref, kn_ref, vp_ref, vc_ref, vn_ref, kx_ref, vx_ref, sink_ref = refs[:10]
    z_refs = refs[10:10 + n_z]
    o_ref = refs[10 + n_z]
    i = pl.program_id(1)
    blk = ATTN_BLOCK
    rows = group * blk
    zw = z_refs[0].shape[1]
    qpos = lax.broadcasted_iota(jnp.int32, (rows, blk), 0) % blk
    kpos = lax.broadcasted_iota(jnp.int32, (rows, blk), 1)
    keep_prev = (kpos >= qpos) & (i > 0)
    keep_next = (kpos <= qpos) & (i < pl.num_programs(1) - 1)
    sink2 = sink_ref[...] * math.log2(math.e)
    for n in range(kv_heads):
        ks = slice(n * LANES, (n + 1) * LANES)
        q4 = jnp.concatenate([q_ref[:, (n * group + g) * LANES:(n * group + g + 1) * LANES] for g in range(group)],
                             axis=0)
        k_all = jnp.concatenate([kp_ref[:, ks], kc_ref[:, ks], kn_ref[:, ks], kx_ref[:, ks]], axis=0)
        v_all = jnp.concatenate([vp_ref[:, ks], vc_ref[:, ks], vn_ref[:, ks], vx_ref[:, ks]], axis=0)
        s = _lane_chunks(_dot_nt(q4, k_all))
        s[0] = jnp.where(keep_prev, s[0], NEG_INF)
        s[2] = jnp.where(keep_next, s[2], NEG_INF)
        sink_b = jnp.concatenate([jnp.broadcast_to(sink2[n, g:g + 1, :], (blk, LANES)) for g in range(group)], axis=0)
        mx = sink_b
        for t in s:
            mx = jnp.maximum(mx, t)
        m = jnp.broadcast_to(jnp.max(mx, axis=-1, keepdims=True), (rows, LANES))
        ps = [jnp.exp2(t - m) for t in s]
        lsum = ps[0]
        for t in ps[1:]:
            lsum = lsum + t
        l = jnp.sum(lsum, axis=-1, keepdims=True) + jnp.exp2(sink_b - m)[:, 0:1]
        o = _dot(jnp.concatenate(ps, axis=1).astype(BF16), v_all) / l
        for g in range(group):
            col = (n * group + g) * LANES
            z = z_refs[col // zw][:, col % zw:col % zw + LANES].astype(F32)
            o_ref[:, col:col + LANES] = (o[g * blk:(g + 1) * blk] * jax.nn.silu(z)).astype(BF16)


def _win_attention(qkvz, seq, batch, ctx_qkvz, ctx_len, di, kv_heads, sink):
    heads = di // LANES
    group = heads // kv_heads
    kvw = kv_heads * LANES
    blk = ATTN_BLOCK
    nb = seq // blk
    z_col0 = di + 2 * kvw
    zw = math.gcd(z_col0, di)
    n_z = di // zw
    assert di % kvw == 0
    sink_rep = jnp.broadcast_to(sink.astype(F32).reshape(kv_heads, group, 1), (kv_heads, group, LANES))
    band = lambda col_block, shift: pl.BlockSpec(
        (blk, kvw), lambda b, i: (b * nb + jnp.clip(i + shift, 0, nb - 1), col_block))
    k_blk, v_blk = di // kvw, di // kvw + 1
    z_spec = lambda t: pl.BlockSpec((blk, zw), lambda b, i: (b * nb + i, z_col0 // zw + t))
    return pl.pallas_call(
        functools.partial(_win_attn_kernel, group=group, kv_heads=kv_heads, n_z=n_z),
        grid=(batch, nb),
        in_specs=[
            pl.BlockSpec((blk, di), lambda b, i: (b * nb + i, 0)),
            band(k_blk, -1), band(k_blk, 0), band(k_blk, 1),
            band(v_blk, -1), band(v_blk, 0), band(v_blk, 1),
            pl.BlockSpec((ctx_len, kvw), lambda b, i: (b, k_blk)),
            pl.BlockSpec((ctx_len, kvw), lambda b, i: (b, v_blk)),
            pl.BlockSpec((kv_heads, group, LANES), lambda b, i: (0, 0, 0)),
        ] + [z_spec(t) for t in range(n_z)],
        out_specs=pl.BlockSpec((blk, di), lambda b, i: (b * nb + i, 0)),
        out_shape=jax.ShapeDtypeStruct((batch * seq, di), BF16),
        compiler_params=_params("parallel", "arbitrary"),
        name="win_attention",
    )(*([qkvz] * 7 + [ctx_qkvz, ctx_qkvz, sink_rep] + [qkvz] * n_z))


class _Stream(NamedTuple):
    seq: int
    span: int
    mod_row: Callable


def kernel(x, c, ctx, c_ctx, l0_norm, l0_w_mod, l0_b_mod, l0_w_in, l0_conv_w, l0_conv_b, l0_w_out, l1_norm, l1_w_mod, l1_b_mod, l1_w_in, l1_q_norm, l1_k_norm, l1_lam_q1, l1_lam_k1, l1_lam_q2, l1_lam_k2, l1_sub_norm, l1_w_out, l2_norm, l2_w_mod, l2_b_mod, l2_w_in, l2_q_norm, l2_k_norm, l2_sink, l2_w_out, l3_norm, l3_w_mod, l3_b_mod, l3_w_in, l3_conv_w, l3_conv_b, l3_w_out):
    batch, seq, d = x.shape
    ctx_len = ctx.shape[1]
    di = l0_w_out.shape[0]
    assert batch < MOD_ROWS
    xs = _Stream(seq, seq, lambda row: row // seq)
    cs = _Stream(ctx_len, batch * ctx_len, lambda row: batch)

    x2 = x.reshape(batch * seq, d)
    c2 = ctx.reshape(batch * ctx_len, d)
    cc = jnp.zeros((MOD_ROWS, d), F32).at[:batch].set(c).at[batch].set(c_ctx)
    mod0, mod1, mod2, mod3 = (_modulation(cc, w, b) for w, b in
                              ((l0_w_mod, l0_b_mod), (l1_w_mod, l1_b_mod), (l2_w_mod, l2_b_mod), (l3_w_mod, l3_b_mod)))
    bf = lambda w: w.astype(BF16)

    w_in, w_out = bf(l0_w_in), bf(l0_w_out)
    x2n = _conv_layer(x2, xs, mod0, l0_norm, w_in, l0_conv_w, l0_conv_b, w_out)
    c2 = _conv_layer(c2, cs, mod0, l0_norm, w_in, l0_conv_w, l0_conv_b, w_out)
    x2 = x2n

    w_in, w_out = bf(l1_w_in), bf(l1_w_out)
    lam_init = 0.8 - 0.6 * math.exp(-0.3 * 1)
    q_scale = l1_q_norm.shape[0] ** -0.5 * math.log2(math.e)
    proj = functools.partial(_proj_layer, norm_g=l1_norm, w_in=w_in, q_norm=l1_q_norm, k_norm=l1_k_norm,
                             q_cols=di, k_cols=di, q_scale=q_scale)
    qx = proj(x2, xs, mod1, rope=True)
    qc = proj(c2, cs, mod1, rope=False)
    lam_vecs = jnp.stack([l1_lam_q1, l1_lam_k1, l1_lam_q2, l1_lam_k2]).astype(F32)
    ax = _diff_attention(qx, seq, batch, qc, ctx_len, lam_vecs, l1_sub_norm, lam_init)
    ac = _diff_attention(qc, ctx_len, batch, None, 0, lam_vecs, l1_sub_norm, lam_init)
    x2 = _out_layer(ax, x2, xs, mod1, w_out)
    c2 = _out_layer(ac, c2, cs, mod1, w_out)

    w_in, w_out = bf(l2_w_in), bf(l2_w_out)
    kv_cols = (l2_w_in.shape[1] - 2 * di) // 2
    proj = functools.partial(_proj_layer, norm_g=l2_norm, w_in=w_in, q_norm=l2_q_norm, k_norm=l2_k_norm,
                             q_cols=di, k_cols=kv_cols, q_scale=l2_q_norm.shape[0] ** -0.5 * math.log2(math.e))
    qx = proj(x2, xs, mod2, rope=True)
    qc = proj(c2, cs, mod2, rope=False)
    ax = _win_attention(qx, seq, batch, qc, ctx_len, di, kv_cols // LANES, l2_sink)
    x2 = _out_layer(ax, x2, xs, mod2, w_out)

    x2 = _conv_layer(x2, xs, mod3, l3_norm, bf(l3_w_in), l3_conv_w, l3_conv_b, bf(l3_w_out))
    return x2.reshape(batch, seq, d)
```

```python
import functools
import math
from typing import Callable, NamedTuple

import jax
import jax.numpy as jnp
from jax import lax
from jax.experimental import pallas as pl
from jax.experimental.pallas import tpu as pltpu

LANES = 128
SUBLANES = 8
MXU_COLS = 256
BF16_ROWS = 16
GRID_W = 64
ROPE_BASE = 10000.0
EPS = 1e-6
NEG_INF = -1e30
ATTN_BLOCK = 128
EPILOGUE_ROWS = 256
ROW_TILE = 512
WIDE_ROW_TILE = 1024
MOD_ROWS = 8
VMEM_LIMIT = 56 * 1024 * 1024

F32 = jnp.float32
BF16 = jnp.bfloat16


def _params(*sem):
    return pltpu.CompilerParams(dimension_semantics=sem, vmem_limit_bytes=VMEM_LIMIT)


def _tile(n, target):
    if n <= target:
        return n
    t = target - target % LANES
    while n % t:
        t -= LANES
    return t


def _div_mod(x, n):
    if n & (n - 1) == 0:
        return lax.shift_right_logical(x, n.bit_length() - 1), x & (n - 1)
    return x // n, x % n


def _dot(a, b):
    return jnp.dot(a, b, preferred_element_type=F32)


def _dot_nt(a, b):
    return lax.dot_general(a, b, (((1,), (1,)), ((), ())), preferred_element_type=F32)


def _norm_mod(t, g, shift, scale):
    ms = jnp.mean(t * t, axis=-1, keepdims=True)
    y = t * lax.rsqrt(ms + EPS) * g
    return (y * (1.0 + scale) + shift).astype(BF16)


def _mod_kernel(c_ref, w_ref, b_ref, o_ref):
    a = jax.nn.silu(c_ref[...]).astype(BF16)
    o_ref[...] = _dot(a, w_ref[...].astype(BF16)) + b_ref[...]


def _modulation(cc, w_mod, b_mod):
    d, n = w_mod.shape
    tn = _tile(n, 768)
    out = pl.pallas_call(
        _mod_kernel,
        grid=(n // tn,),
        in_specs=[
            pl.BlockSpec((MOD_ROWS, d), lambda j: (0, 0)),
            pl.BlockSpec((d, tn), lambda j: (0, j)),
            pl.BlockSpec((1, tn), lambda j: (0, j)),
        ],
        out_specs=pl.BlockSpec((MOD_ROWS, tn), lambda j: (0, j)),
        out_shape=jax.ShapeDtypeStruct((MOD_ROWS, n), F32),
        compiler_params=_params("arbitrary"),
        name="modulation",
    )(cc, w_mod, b_mod.reshape(1, n))
    return out.reshape(MOD_ROWS, 1, n)


def _conv_kernel(xp_ref, x_ref, xn_ref, mod_ref, g_ref, wb_ref, wc_ref, wx_ref, wz_ref, cw_ref, cb_ref,
                 wo_ref, o_ref, h_scr, *, tm, d, seq):
    i = pl.program_id(0)
    j = pl.program_id(1)
    halo = BF16_ROWS
    mod = mod_ref[0]
    shift, scale, gate = mod[:, :d], mod[:, d:2 * d], mod[:, 2 * d:]

    @pl.when(j == 0)
    def _():
        g = g_ref[...]
        h_scr[0:halo, :] = _norm_mod(xp_ref[...], g, shift, scale)
        h_scr[halo:halo + tm, :] = _norm_mod(x_ref[...], g, shift, scale)
        h_scr[halo + tm:, :] = _norm_mod(xn_ref[...], g, shift, scale)
        o_ref[...] = jnp.zeros_like(o_ref)

    n = tm + 2 * halo
    h = h_scr[...]
    hm = h_scr[halo:halo + tm, :]
    u = _dot(h, wc_ref[...]) * _dot(h, wx_ref[...])
    pos = lax.rem(i * tm + lax.broadcasted_iota(jnp.int32, (tm, 1), 0), seq)
    u_prev = jnp.where(pos == 0, 0.0, pltpu.roll(u, 1, 0)[halo:halo + tm])
    u_next = jnp.where(pos == seq - 1, 0.0, pltpu.roll(u, n - 1, 0)[halo:halo + tm])
    cw = cw_ref[...]
    y = u_prev * cw[0:1] + u[halo:halo + tm] * cw[1:2] + u_next * cw[2:3] + cb_ref[...]
    gated = _dot(hm, wb_ref[...]) * y * jax.nn.silu(_dot(hm, wz_ref[...]))
    o_ref[...] += _dot(gated.astype(BF16), wo_ref[...])

    @pl.when(j == pl.num_programs(1) - 1)
    def _():
        o_ref[...] = x_ref[...] + gate * o_ref[...]


def _conv_layer(x2, stream, mod, norm_g, w_in, conv_w, conv_b, w_out):
    m, d = x2.shape
    di = w_out.shape[0]
    tm = min(stream.span, ROW_TILE)
    tn = _tile(di, 512)
    nj = di // tn
    hb = tm // BF16_ROWS
    last_hb = m // BF16_ROWS - 1
    w_spec = lambda c: pl.BlockSpec((d, tn), lambda i, j: (0, c * nj + j))
    return pl.pallas_call(
        functools.partial(_conv_kernel, tm=tm, d=d, seq=stream.seq),
        grid=(m // tm, nj),
        in_specs=[
            pl.BlockSpec((BF16_ROWS, d), lambda i, j: (jnp.maximum(i * hb - 1, 0), 0)),
            pl.BlockSpec((tm, d), lambda i, j: (i, 0)),
            pl.BlockSpec((BF16_ROWS, d), lambda i, j: (jnp.minimum((i + 1) * hb, last_hb), 0)),
            pl.BlockSpec((1, 1, 3 * d), lambda i, j: (stream.mod_row(i * tm), 0, 0)),
            pl.BlockSpec((1, d), lambda i, j: (0, 0)),
            w_spec(0), w_spec(1), w_spec(2), w_spec(3),
            pl.BlockSpec((3, tn), lambda i, j: (0, j)),
            pl.BlockSpec((1, tn), lambda i, j: (0, j)),
            pl.BlockSpec((tn, d), lambda i, j: (j, 0)),
        ],
        out_specs=pl.BlockSpec((tm, d), lambda i, j: (i, 0)),
        out_shape=jax.ShapeDtypeStruct((m, d), F32),
        scratch_shapes=[pltpu.VMEM((tm + 2 * BF16_ROWS, d), BF16)],
        compiler_params=_params("parallel", "arbitrary"),
        name="conv_layer",
    )(x2, x2, x2, mod, norm_g.reshape(1, d), w_in, w_in, w_in, w_in, conv_w, conv_b.reshape(1, di), w_out)


def _head_norm_rope(acc, gn, cos, sin, group, out_scale):
    outs = []
    swap = group // 4
    for c in range(acc.shape[1] // LANES):
        t = acc[:, c * LANES:(c + 1) * LANES]
        lane = lax.broadcasted_iota(jnp.int32, t.shape, 1)
        sq = t * t
        if group == LANES:
            ms = jnp.mean(sq, axis=-1, keepdims=True)
        else:
            lo = lane < group
            s_lo = jnp.sum(jnp.where(lo, sq, 0.0), axis=-1, keepdims=True)
            s_hi = jnp.sum(jnp.where(lo, 0.0, sq), axis=-1, keepdims=True)
            ms = jnp.where(lo, s_lo, s_hi) * (1.0 / group)
        y = t * lax.rsqrt(ms + EPS) * gn
        if cos is not None:
            partner = jnp.take_along_axis(y, lane ^ swap, axis=1)
            y = y * cos + partner * sin
        if out_scale != 1.0:
            y = y * out_scale
        outs.append(y)
    return outs[0] if len(outs) == 1 else jnp.concatenate(outs, axis=1)


def _proj_kernel(*refs, d, q_tiles, k_tiles, group, rope, q_scale):
    if rope:
        x_ref, mod_ref, g_ref, w_ref, qn_ref, kn_ref, cos_ref, sin_ref, o_ref, h_scr = refs
    else:
        x_ref, mod_ref, g_ref, w_ref, qn_ref, kn_ref, o_ref, h_scr = refs
    j = pl.program_id(1)

    @pl.when(j == 0)
    def _():
        mod = mod_ref[0]
        h_scr[...] = _norm_mod(x_ref[...], g_ref[...], mod[:, :d], mod[:, d:2 * d])

    tm, tn = o_ref.shape
    cw = min(tn, MXU_COLS)

    def units(rh, epilogue):
        for r in range(tm // rh):
            rows = slice(r * rh, (r + 1) * rh)
            h = h_scr[rows, :]
            for c in range(tn // cw):
                cols = slice(c * cw, (c + 1) * cw)
                o_ref[rows, cols] = epilogue(_dot(h, w_ref[:, cols]), rows).astype(BF16)

    def head_epilogue(gn_ref, scale):
        def epilogue(acc, rows):
            cos = cos_ref[rows, :] if rope else None
            sin = sin_ref[rows, :] if rope else None
            return _head_norm_rope(acc, gn_ref[...], cos, sin, group, scale)
        return epilogue

    @pl.when(j < q_tiles)
    def _():
        units(min(tm, EPILOGUE_ROWS), head_epilogue(qn_ref, q_scale))

    @pl.when((j >= q_tiles) & (j < q_tiles + k_tiles))
    def _():
        units(min(tm, EPILOGUE_ROWS), head_epilogue(kn_ref, 1.0))

    @pl.when(j >= q_tiles + k_tiles)
    def _():
        units(tm, lambda acc, rows: acc)


def _rope_tables(seq, group):
    rows = seq // GRID_W
    row = jnp.repeat(jnp.arange(rows), GRID_W).astype(F32)
    col = jnp.tile(jnp.arange(GRID_W), rows).astype(F32)
    n_freq = group // 4
    inv_freq = ROPE_BASE ** (-(jnp.arange(n_freq, dtype=F32) / n_freq))
    ar, ac = row[:, None] * inv_freq, col[:, None] * inv_freq
    cos = jnp.concatenate([jnp.cos(ar), jnp.cos(ar), jnp.cos(ac), jnp.cos(ac)], axis=-1)
    sin = jnp.concatenate([-jnp.sin(ar), jnp.sin(ar), -jnp.sin(ac), jnp.sin(ac)], axis=-1)
    reps = LANES // group
    return jnp.tile(cos, (1, reps)), jnp.tile(sin, (1, reps))


def _proj_layer(x2, stream, mod, norm_g, w_in, q_norm, k_norm, q_cols, k_cols, rope, q_scale):
    m, d = x2.shape
    n = w_in.shape[1]
    group = q_norm.shape[0]
    tm = min(stream.span, WIDE_ROW_TILE)
    tn = math.gcd(math.gcd(q_cols, k_cols), 1024)
    reps = LANES // group
    args = [x2, mod, norm_g.reshape(1, d), w_in,
            jnp.tile(q_norm, reps).reshape(1, LANES), jnp.tile(k_norm, reps).reshape(1, LANES)]
    in_specs = [
        pl.BlockSpec((tm, d), lambda i, j: (i, 0)),
        pl.BlockSpec((1, 1, 3 * d), lambda i, j: (stream.mod_row(i * tm), 0, 0)),
        pl.BlockSpec((1, d), lambda i, j: (0, 0)),
        pl.BlockSpec((d, tn), lambda i, j: (0, j)),
        pl.BlockSpec((1, LANES), lambda i, j: (0, 0)),
        pl.BlockSpec((1, LANES), lambda i, j: (0, 0)),
    ]
    if rope:
        tiles_per_seq = stream.seq // tm
        args += list(_rope_tables(stream.seq, group))
        in_specs += [pl.BlockSpec((tm, LANES), lambda i, j: (i % tiles_per_seq, 0))] * 2
    return pl.pallas_call(
        functools.partial(_proj_kernel, d=d, q_tiles=q_cols // tn, k_tiles=k_cols // tn, group=group,
                          rope=rope, q_scale=q_scale),
        grid=(m // tm, n // tn),
        in_specs=in_specs,
        out_specs=pl.BlockSpec((tm, tn), lambda i, j: (i, j)),
        out_shape=jax.ShapeDtypeStruct((m, n), BF16),
        scratch_shapes=[pltpu.VMEM((tm, d), BF16)],
        compiler_params=_params("parallel", "arbitrary"),
        name="proj_layer",
    )(*args)


def _out_kernel(a_ref, w_ref, x_ref, gate_ref, o_ref):
    o_ref[...] = x_ref[...] + gate_ref[0] * _dot(a_ref[...], w_ref[...])


def _out_layer(a2, x2, stream, mod, w_out):
    m, d = x2.shape
    di = a2.shape[1]
    tm = min(stream.span, WIDE_ROW_TILE)
    tn = _tile(d, 1024)
    gate_block0 = 2 * d // tn
    return pl.pallas_call(
        _out_kernel,
        grid=(m // tm, d // tn),
        in_specs=[
            pl.BlockSpec((tm, di), lambda i, j: (i, 0)),
            pl.BlockSpec((di, tn), lambda i, j: (0, j)),
            pl.BlockSpec((tm, tn), lambda i, j: (i, j)),
            pl.BlockSpec((1, 1, tn), lambda i, j: (stream.mod_row(i * tm), 0, gate_block0 + j)),
        ],
        out_specs=pl.BlockSpec((tm, tn), lambda i, j: (i, j)),
        out_shape=jax.ShapeDtypeStruct((m, d), F32),
        compiler_params=_params("parallel", "arbitrary"),
        name="out_layer",
    )(a2, w_out, x2, mod)


def _lane_chunks(t):
    return [t[:, c * LANES:(c + 1) * LANES] for c in range(t.shape[1] // LANES)]


def _diff_attn_kernel(*refs, tq, tk, n_chunks, has_ctx, tiles, rounds, lam_init):
    pipelined = rounds > 1
    refs = iter(refs)
    lam_ref, q_ref = next(refs), next(refs)
    qn_ref = kn_ref = kcn_ref = kc_ref = vc_ref = vct_scr = None
    if pipelined:
        qn_ref, kn_ref = next(refs), next(refs)
        if has_ctx:
            kcn_ref = next(refs)
    k_ref, v_ref = next(refs), next(refs)
    if has_ctx:
        kc_ref, vc_ref = next(refs), next(refs)
    z_ref, sn_ref, o_ref, vt_scr = next(refs), next(refs), next(refs), next(refs)
    if has_ctx:
        vct_scr = next(refs)
    bufs = list(refs)
    per_set = 2 + has_ctx + pipelined
    sets = []
    for p in range(len(bufs) // per_set):
        group = bufs[p * per_set:(p + 1) * per_set]
        sets.append((group[0], group[1] if has_ctx else None, group[1 + has_ctx], group[-1] if pipelined else None))
    half = LANES // 2

    def transpose_values():
        for c in range(n_chunks):
            vt_scr[:, c * tk:(c + 1) * tk] = v_ref[c * tk:(c + 1) * tk, :].astype(F32).T.astype(BF16)
        if has_ctx:
            vct_scr[...] = vc_ref[...].astype(F32).T.astype(BF16)

    def sublane_groups(s):
        return s.reshape(s.shape[0] // SUBLANES, SUBLANES, s.shape[1])

    def scores(q, buf, t, k_ref=k_ref, kc_ref=kc_ref):
        s_buf, c_buf, m_buf, _ = buf
        lane = lax.broadcasted_iota(jnp.int32, q.shape, 1)
        zero = jnp.zeros_like(q)
        subs = (jnp.where(lane < half, q, zero), jnp.where(lane < half, zero, q))
        for h, qh in enumerate(subs):
            cols = slice(h * tq, (h + 1) * tq)
            mx = jnp.full((SUBLANES, tq), NEG_INF, F32)
            for c in range(n_chunks):
                s = _dot_nt(k_ref[c * tk:(c + 1) * tk, :], qh)
                s_buf[t, c, :, cols] = s
                mx = jnp.maximum(mx, jnp.max(sublane_groups(s), axis=0))
            if has_ctx:
                s = _dot_nt(kc_ref[...], qh)
                c_buf[t, :, cols] = s
                mx = jnp.maximum(mx, jnp.max(sublane_groups(s), axis=0))
            m_buf[t, :, cols] = jnp.broadcast_to(jnp.max(mx, axis=0, keepdims=True), (SUBLANES, tq))

    def weighted(buf, t):
        s_buf, c_buf, m_buf, _ = buf
        m = m_buf[t]
        lsum = jnp.zeros((SUBLANES, 2 * tq), F32)
        acc = jnp.zeros((LANES, 2 * tq), F32)
        blocks = [(s_buf[t, c], vt_scr[:, c * tk:(c + 1) * tk]) for c in range(n_chunks)]
        if has_ctx:
            blocks.append((c_buf[t], vct_scr[...]))
        for s, vt in blocks:
            p = jnp.exp2(sublane_groups(s) - m[None])
            lsum = lsum + jnp.sum(p, axis=0)
            acc = acc + _dot(vt, p.reshape(s.shape).astype(BF16))
        return acc / jnp.sum(lsum, axis=0, keepdims=True)

    lv = lam_ref[...]
    lam = (jnp.exp(jnp.sum(lv[0:1] * lv[1:2], axis=-1, keepdims=True))
           - jnp.exp(jnp.sum(lv[2:3] * lv[3:4], axis=-1, keepdims=True)) + lam_init)

    def finish(o_all, z):
        o = (o_all[:, :tq] - lam * o_all[:, tq:]).T
        ms = jnp.mean(o * o, axis=-1, keepdims=True)
        y = o * lax.rsqrt(ms + EPS) * sn_ref[...] * (1.0 - lam_init)
        return (y * jax.nn.silu(z.astype(F32))).astype(BF16)

    transpose_values()
    if not pipelined:
        scores(q_ref[...], sets[0], 0)
        o_ref[...] = finish(weighted(sets[0], 0), z_ref[...])
        return

    rows = tiles * tq

    @pl.when((pl.program_id(0) == 0) & (pl.program_id(1) == 0))
    def _():
        sets[0][3][...] = jnp.zeros_like(sets[0][3])
        for t in range(tiles):
            scores(q_ref[t * tq:(t + 1) * tq, :], sets[0], t)

    def both(q, k_ref, kc_ref, cur, nxt, t):
        s_nxt, c_nxt, m_nxt, _ = nxt
        s_cur, c_cur, m_cur, _ = cur
        lane = lax.broadcasted_iota(jnp.int32, q.shape, 1)
        zero = jnp.zeros_like(q)
        subs = (jnp.where(lane < half, q, zero), jnp.where(lane < half, zero, q))
        mxs = [jnp.full((SUBLANES, tq), NEG_INF, F32) for _ in subs]
        m = m_cur[t]
        lsum = jnp.zeros((SUBLANES, 2 * tq), F32)
        acc = jnp.zeros((LANES, 2 * tq), F32)
        tie = None
        for c in range(n_chunks + has_ctx):
            if c < n_chunks:
                k, vt, s_old = k_ref[c * tk:(c + 1) * tk, :], vt_scr[:, c * tk:(c + 1) * tk], s_cur[t, c]
            else:
                k, vt, s_old = kc_ref[...], vct_scr[...], c_cur[t]
            if tie is not None:
                k = k + jnp.tile(tie, (k.shape[0] // SUBLANES, 1)).astype(BF16)
            for h, qh in enumerate(subs):
                cols = slice(h * tq, (h + 1) * tq)
                s = _dot_nt(k, qh)
                if c < n_chunks:
                    s_nxt[t, c, :, cols] = s
                else:
                    c_nxt[t, :, cols] = s
                mxs[h] = jnp.maximum(mxs[h], jnp.max(sublane_groups(s), axis=0))
            p = jnp.exp2(sublane_groups(s_old) - m[None])
            lsum = lsum + jnp.sum(p, axis=0)
            acc = acc + _dot(vt, p.reshape(s_old.shape).astype(BF16))
            bits = pltpu.bitcast(p[-1][:, :LANES], jnp.uint32)
            tie = pltpu.bitcast((bits >> 16) >> 16, F32)
        for h, mx in enumerate(mxs):
            m_nxt[t, :, h * tq:(h + 1) * tq] = jnp.broadcast_to(jnp.max(mx, axis=0, keepdims=True), (SUBLANES, tq))
        return acc / jnp.sum(lsum, axis=0, keepdims=True)

    def round_(r, cur, nxt, next_q, k_ref, kc_ref):
        prev = jnp.maximum(r - 1, 0) * rows
        for t in range(tiles):
            out_rows = pl.ds(pl.multiple_of(prev + t * tq, tq), tq)
            o_ref[out_rows, :] = finish(cur[3][t], z_ref[out_rows, :])
            nxt[3][t] = both(next_q(t), k_ref, kc_ref, cur, nxt, t)

    @pl.loop(0, rounds - 1)
    def _(r):
        def next_q(t):
            return q_ref[pl.ds(pl.multiple_of((r + 1) * rows + t * tq, tq), tq), :]

        @pl.when((r & 1) == 0)
        def _():
            round_(r, sets[0], sets[1], next_q, k_ref, kc_ref)

        @pl.when((r & 1) == 1)
        def _():
            round_(r, sets[1], sets[0], next_q, k_ref, kc_ref)

    last = rounds - 1
    cur, nxt = sets[last % 2], sets[1 - last % 2]
    round_(last, cur, nxt, lambda t: qn_ref[t * tq:(t + 1) * tq, :], kn_ref, kcn_ref)
    for t in range(tiles):
        out_rows = slice(last * rows + t * tq, last * rows + (t + 1) * tq)
        o_ref[out_rows, :] = finish(nxt[3][t], z_ref[out_rows, :])


def _diff_attention(qkvz, seq, batch, ctx_qkvz, ctx_len, lam_vecs, sub_norm, lam_init):
    di = qkvz.shape[1] // 4
    heads = di // LANES
    tq = min(seq, 256)
    tk = min(seq, 256)
    tiles = 2 if seq >= 2 * tq else 1
    rounds = seq // (tiles * tq)
    assert rounds == 1 or rounds % 2 == 0
    has_ctx = ctx_qkvz is not None
    pipelined = rounds > 1

    def next_head(index):
        def index_map(b, h):
            wrap = h + 1 == heads
            return index(jnp.minimum(b + wrap, batch - 1), jnp.where(wrap, 0, h + 1))
        return index_map

    head_spec = lambda col0: pl.BlockSpec((seq, LANES), lambda b, h: (b, col0 + h))
    ctx_spec = lambda col0: pl.BlockSpec((ctx_len, LANES), lambda b, h: (b, col0 + h))
    args = [lam_vecs, qkvz]
    in_specs = [pl.BlockSpec(lam_vecs.shape, lambda b, h: (0, 0)), head_spec(0)]
    if pipelined:
        args += [qkvz, qkvz]
        in_specs += [pl.BlockSpec((tiles * tq, LANES), next_head(lambda b, h: (b * rounds, h))),
                     pl.BlockSpec((seq, LANES), next_head(lambda b, h: (b, heads + h)))]
        if has_ctx:
            args += [ctx_qkvz]
            in_specs += [pl.BlockSpec((ctx_len, LANES), next_head(lambda b, h: (b, heads + h)))]
    args += [qkvz, qkvz]
    in_specs += [head_spec(heads), head_spec(2 * heads)]
    if has_ctx:
        args += [ctx_qkvz, ctx_qkvz]
        in_specs += [ctx_spec(heads), ctx_spec(2 * heads)]
    args += [qkvz, sub_norm.reshape(1, LANES)]
    in_specs += [head_spec(3 * heads), pl.BlockSpec((1, LANES), lambda b, h: (0, 0))]
    scratch = [pltpu.VMEM((LANES, seq), BF16)] + ([pltpu.VMEM((LANES, ctx_len), BF16)] if has_ctx else [])
    for _ in range(2 if pipelined else 1):
        scratch += [pltpu.VMEM((tiles, seq // tk, tk, 2 * tq), F32)]
        if has_ctx:
            scratch += [pltpu.VMEM((tiles, ctx_len, 2 * tq), F32)]
        scratch += [pltpu.VMEM((tiles, SUBLANES, 2 * tq), F32)]
        if pipelined:
            scratch += [pltpu.VMEM((tiles, LANES, 2 * tq), F32)]
    return pl.pallas_call(
        functools.partial(_diff_attn_kernel, tq=tq, tk=tk, n_chunks=seq // tk, has_ctx=has_ctx, tiles=tiles,
                          rounds=rounds, lam_init=lam_init),
        grid=(batch, heads),
        in_specs=in_specs,
        out_specs=head_spec(0),
        out_shape=jax.ShapeDtypeStruct((batch * seq, di), BF16),
        scratch_shapes=scratch,
        compiler_params=_params("arbitrary", "arbitrary"),
        name="diff_attention",
    )(*args)


def _win_attn_kernel(*refs, group, kv_heads, n_z):
    q_ref, kp_ref, kc_ref, kn_ref, vp_ref, vc_ref, vn_ref, kx_ref, vx_ref, sink_ref = refs[:10]
    z_refs = refs[10:10 + n_z]
    o_ref = refs[10 + n_z]
    i = pl.program_id(1)
    blk = ATTN_BLOCK
    rows = group * blk
    zw = z_refs[0].shape[1]
    qpos = lax.broadcasted_iota(jnp.int32, (rows, blk), 0) % blk
    kpos = lax.broadcasted_iota(jnp.int32, (rows, blk), 1)
    keep_prev = (kpos >= qpos) & (i > 0)
    keep_next = (kpos <= qpos) & (i < pl.num_programs(1) - 1)
    sink2 = sink_ref[...] * math.log2(math.e)
    for n in range(kv_heads):
        ks = slice(n * LANES, (n + 1) * LANES)
        q4 = jnp.concatenate([q_ref[:, (n * group + g) * LANES:(n * group + g + 1) * LANES] for g in range(group)],
                             axis=0)
        k_all = jnp.concatenate([kp_ref[:, ks], kc_ref[:, ks], kn_ref[:, ks], kx_ref[:, ks]], axis=0)
        v_all = jnp.concatenate([vp_ref[:, ks], vc_ref[:, ks], vn_ref[:, ks], vx_ref[:, ks]], axis=0)
        s = _lane_chunks(_dot_nt(q4, k_all))
        s[0] = jnp.where(keep_prev, s[0], NEG_INF)
        s[2] = jnp.where(keep_next, s[2], NEG_INF)
        sink_b = jnp.concatenate([jnp.broadcast_to(sink2[n, g:g + 1, :], (blk, LANES)) for g in range(group)], axis=0)
        mx = sink_b
        for t in s:
            mx = jnp.maximum(mx, t)
        m = jnp.broadcast_to(jnp.max(mx, axis=-1, keepdims=True), (rows, LANES))
        ps = [jnp.exp2(t - m) for t in s]
        lsum = ps[0]
        for t in ps[1:]:
            lsum = lsum + t
        l = jnp.sum(lsum, axis=-1, keepdims=True) + jnp.exp2(sink_b - m)[:, 0:1]
        o = _dot(jnp.concatenate(ps, axis=1).astype(BF16), v_all) / l
        for g in range(group):
            col = (n * group + g) * LANES
            z = z_refs[col // zw][:, col % zw:col % zw + LANES].astype(F32)
            o_ref[:, col:col + LANES] = (o[g * blk:(g + 1) * blk] * jax.nn.silu(z)).astype(BF16)


def _win_attention(qkvz, seq, batch, ctx_qkvz, ctx_len, di, kv_heads, sink):
    heads = di // LANES
    group = heads // kv_heads
    kvw = kv_heads * LANES
    blk = ATTN_BLOCK
    nb = seq // blk
    z_col0 = di + 2 * kvw
    zw = math.gcd(z_col0, di)
    n_z = di // zw
    assert di % kvw == 0
    sink_rep = jnp.broadcast_to(sink.astype(F32).reshape(kv_heads, group, 1), (kv_heads, group, LANES))
    band = lambda col_block, shift: pl.BlockSpec(
        (blk, kvw), lambda b, i: (b * nb + jnp.clip(i + shift, 0, nb - 1), col_block))
    k_blk, v_blk = di // kvw, di // kvw + 1
    z_spec = lambda t: pl.BlockSpec((blk, zw), lambda b, i: (b * nb + i, z_col0 // zw + t))
    return pl.pallas_call(
        functools.partial(_win_attn_kernel, group=group, kv_heads=kv_heads, n_z=n_z),
        grid=(batch, nb),
        in_specs=[
            pl.BlockSpec((blk, di), lambda b, i: (b * nb + i, 0)),
            band(k_blk, -1), band(k_blk, 0), band(k_blk, 1),
            band(v_blk, -1), band(v_blk, 0), band(v_blk, 1),
            pl.BlockSpec((ctx_len, kvw), lambda b, i: (b, k_blk)),
            pl.BlockSpec((ctx_len, kvw), lambda b, i: (b, v_blk)),
            pl.BlockSpec((kv_heads, group, LANES), lambda b, i: (0, 0, 0)),
        ] + [z_spec(t) for t in range(n_z)],
        out_specs=pl.BlockSpec((blk, di), lambda b, i: (b * nb + i, 0)),
        out_shape=jax.ShapeDtypeStruct((batch * seq, di), BF16),
        compiler_params=_params("parallel", "arbitrary"),
        name="win_attention",
    )(*([qkvz] * 7 + [ctx_qkvz, ctx_qkvz, sink_rep] + [qkvz] * n_z))


class _Stream(NamedTuple):
    seq: int
    span: int
    mod_row: Callable


def kernel(x, c, ctx, c_ctx, l0_norm, l0_w_mod, l0_b_mod, l0_w_in, l0_conv_w, l0_conv_b, l0_w_out, l1_norm, l1_w_mod, l1_b_mod, l1_w_in, l1_q_norm, l1_k_norm, l1_lam_q1, l1_lam_k1, l1_lam_q2, l1_lam_k2, l1_sub_norm, l1_w_out, l2_norm, l2_w_mod, l2_b_mod, l2_w_in, l2_q_norm, l2_k_norm, l2_sink, l2_w_out, l3_norm, l3_w_mod, l3_b_mod, l3_w_in, l3_conv_w, l3_conv_b, l3_w_out):
    batch, seq, d = x.shape
    ctx_len = ctx.shape[1]
    di = l0_w_out.shape[0]
    assert batch < MOD_ROWS
    xs = _Stream(seq, seq, lambda row: row // seq)
    cs = _Stream(ctx_len, batch * ctx_len, lambda row: batch)

    x2 = x.reshape(batch * seq, d)
    c2 = ctx.reshape(batch * ctx_len, d)
    cc = jnp.zeros((MOD_ROWS, d), F32).at[:batch].set(c).at[batch].set(c_ctx)
    mod0, mod1, mod2, mod3 = (_modulation(cc, w, b) for w, b in
                              ((l0_w_mod, l0_b_mod), (l1_w_mod, l1_b_mod), (l2_w_mod, l2_b_mod), (l3_w_mod, l3_b_mod)))
    bf = lambda w: w.astype(BF16)

    w_in, w_out = bf(l0_w_in), bf(l0_w_out)
    x2n = _conv_layer(x2, xs, mod0, l0_norm, w_in, l0_conv_w, l0_conv_b, w_out)
    c2 = _conv_layer(c2, cs, mod0, l0_norm, w_in, l0_conv_w, l0_conv_b, w_out)
    x2 = x2n

    w_in, w_out = bf(l1_w_in), bf(l1_w_out)
    lam_init = 0.8 - 0.6 * math.exp(-0.3 * 1)
    q_scale = l1_q_norm.shape[0] ** -0.5 * math.log2(math.e)
    proj = functools.partial(_proj_layer, norm_g=l1_norm, w_in=w_in, q_norm=l1_q_norm, k_norm=l1_k_norm,
                             q_cols=di, k_cols=di, q_scale=q_scale)
    qx = proj(x2, xs, mod1, rope=True)
    qc = proj(c2, cs, mod1, rope=False)
    lam_vecs = jnp.stack([l1_lam_q1, l1_lam_k1, l1_lam_q2, l1_lam_k2]).astype(F32)
    ax = _diff_attention(qx, seq, batch, qc, ctx_len, lam_vecs, l1_sub_norm, lam_init)
    ac = _diff_attention(qc, ctx_len, batch, None, 0, lam_vecs, l1_sub_norm, lam_init)
    x2 = _out_layer(ax, x2, xs, mod1, w_out)
    c2 = _out_layer(ac, c2, cs, mod1, w_out)

    w_in, w_out = bf(l2_w_in), bf(l2_w_out)
    kv_cols = (l2_w_in.shape[1] - 2 * di) // 2
    proj = functools.partial(_proj_layer, norm_g=l2_norm, w_in=w_in, q_norm=l2_q_norm, k_norm=l2_k_norm,
                             q_cols=di, k_cols=kv_cols, q_scale=l2_q_norm.shape[0] ** -0.5 * math.log2(math.e))
    qx = proj(x2, xs, mod2, rope=True)
    qc = proj(c2, cs, mod2, rope=False)
    ax = _win_attention(qx, seq, batch, qc, ctx_len, di, kv_cols // LANES, l2_sink)
    x2 = _out_layer(ax, x2, xs, mod2, w_out)

    x2 = _conv_layer(x2, xs, mod3, l3_norm, bf(l3_w_in), l3_conv_w, l3_conv_b, bf(l3_w_out))
    return x2.reshape(batch, seq, d)
```

```python
import functools
import math
from typing import Callable, NamedTuple

import jax
import jax.numpy as jnp
from jax import lax
from jax.experimental import pallas as pl
from jax.experimental.pallas import tpu as pltpu

LANES = 128
SUBLANES = 8
MXU_COLS = 256
BF16_ROWS = 16
GRID_W = 64
ROPE_BASE = 10000.0
EPS = 1e-6
NEG_INF = -1e30
ATTN_BLOCK = 128
EPILOGUE_ROWS = 256
ROW_TILE = 512
WIDE_ROW_TILE = 1024
MOD_ROWS = 8
VMEM_LIMIT = 56 * 1024 * 1024

F32 = jnp.float32
BF16 = jnp.bfloat16


def _params(*sem):
    return pltpu.CompilerParams(dimension_semantics=sem, vmem_limit_bytes=VMEM_LIMIT)


def _tile(n, target):
    if n <= target:
        return n
    t = target - target % LANES
    while n % t:
        t -= LANES
    return t


def _div_mod(x, n):
    if n & (n - 1) == 0:
        return lax.shift_right_logical(x, n.bit_length() - 1), x & (n - 1)
    return x // n, x % n


def _dot(a, b):
    return jnp.dot(a, b, preferred_element_type=F32)


def _dot_nt(a, b):
    return lax.dot_general(a, b, (((1,), (1,)), ((), ())), preferred_element_type=F32)


def _norm_mod(t, g, shift, scale):
    ms = jnp.mean(t * t, axis=-1, keepdims=True)
    y = t * lax.rsqrt(ms + EPS) * g
    return (y * (1.0 + scale) + shift).astype(BF16)


def _mod_kernel(c_ref, w_ref, b_ref, o_ref):
    a = jax.nn.silu(c_ref[...]).astype(BF16)
    o_ref[...] = _dot(a, w_ref[...].astype(BF16)) + b_ref[...]


def _modulation(cc, w_mod, b_mod):
    d, n = w_mod.shape
    tn = _tile(n, 768)
    out = pl.pallas_call(
        _mod_kernel,
        grid=(n // tn,),
        in_specs=[
            pl.BlockSpec((MOD_ROWS, d), lambda j: (0, 0)),
            pl.BlockSpec((d, tn), lambda j: (0, j)),
            pl.BlockSpec((1, tn), lambda j: (0, j)),
        ],
        out_specs=pl.BlockSpec((MOD_ROWS, tn), lambda j: (0, j)),
        out_shape=jax.ShapeDtypeStruct((MOD_ROWS, n), F32),
        compiler_params=_params("arbitrary"),
        name="modulation",
    )(cc, w_mod, b_mod.reshape(1, n))
    return out.reshape(MOD_ROWS, 1, n)


def _conv_kernel(xp_ref, x_ref, xn_ref, mod_ref, g_ref, wb_ref, wc_ref, wx_ref, wz_ref, cw_ref, cb_ref,
                 wo_ref, o_ref, h_scr, *, tm, d, seq):
    i = pl.program_id(0)
    j = pl.program_id(1)
    halo = BF16_ROWS
    mod = mod_ref[0]
    shift, scale, gate = mod[:, :d], mod[:, d:2 * d], mod[:, 2 * d:]

    @pl.when(j == 0)
    def _():
        g = g_ref[...]
        h_scr[0:halo, :] = _norm_mod(xp_ref[...], g, shift, scale)
        h_scr[halo:halo + tm, :] = _norm_mod(x_ref[...], g, shift, scale)
        h_scr[halo + tm:, :] = _norm_mod(xn_ref[...], g, shift, scale)
        o_ref[...] = jnp.zeros_like(o_ref)

    n = tm + 2 * halo
    h = h_scr[...]
    hm = h_scr[halo:halo + tm, :]
    u = _dot(h, wc_ref[...]) * _dot(h, wx_ref[...])
    pos = lax.rem(i * tm + lax.broadcasted_iota(jnp.int32, (tm, 1), 0), seq)
    u_prev = jnp.where(pos == 0, 0.0, pltpu.roll(u, 1, 0)[halo:halo + tm])
    u_next = jnp.where(pos == seq - 1, 0.0, pltpu.roll(u, n - 1, 0)[halo:halo + tm])
    cw = cw_ref[...]
    y = u_prev * cw[0:1] + u[halo:halo + tm] * cw[1:2] + u_next * cw[2:3] + cb_ref[...]
    gated = _dot(hm, wb_ref[...]) * y * jax.nn.silu(_dot(hm, wz_ref[...]))
    o_ref[...] += _dot(gated.astype(BF16), wo_ref[...])

    @pl.when(j == pl.num_programs(1) - 1)
    def _():
        o_ref[...] = x_ref[...] + gate * o_ref[...]


def _conv_layer(x2, stream, mod, norm_g, w_in, conv_w, conv_b, w_out):
    m, d = x2.shape
    di = w_out.shape[0]
    tm = min(stream.span, ROW_TILE)
    tn = _tile(di, 512)
    nj = di // tn
    hb = tm // BF16_ROWS
    last_hb = m // BF16_ROWS - 1
    w_spec = lambda c: pl.BlockSpec((d, tn), lambda i, j: (0, c * nj + j))
    return pl.pallas_call(
        functools.partial(_conv_kernel, tm=tm, d=d, seq=stream.seq),
        grid=(m // tm, nj),
        in_specs=[
            pl.BlockSpec((BF16_ROWS, d), lambda i, j: (jnp.maximum(i * hb - 1, 0), 0)),
            pl.BlockSpec((tm, d), lambda i, j: (i, 0)),
            pl.BlockSpec((BF16_ROWS, d), lambda i, j: (jnp.minimum((i + 1) * hb, last_hb), 0)),
            pl.BlockSpec((1, 1, 3 * d), lambda i, j: (stream.mod_row(i * tm), 0, 0)),
            pl.BlockSpec((1, d), lambda i, j: (0, 0)),
            w_spec(0), w_spec(1), w_spec(2), w_spec(3),
            pl.BlockSpec((3, tn), lambda i, j: (0, j)),
            pl.BlockSpec((1, tn), lambda i, j: (0, j)),
            pl.BlockSpec((tn, d), lambda i, j: (j, 0)),
        ],
        out_specs=pl.BlockSpec((tm, d), lambda i, j: (i, 0)),
        out_shape=jax.ShapeDtypeStruct((m, d), F32),
        scratch_shapes=[pltpu.VMEM((tm + 2 * BF16_ROWS, d), BF16)],
        compiler_params=_params("parallel", "arbitrary"),
        name="conv_layer",
    )(x2, x2, x2, mod, norm_g.reshape(1, d), w_in, w_in, w_in, w_in, conv_w, conv_b.reshape(1, di), w_out)


def _head_norm_rope(acc, gn, cos, sin, group, out_scale):
    outs = []
    swap = group // 4
    for c in range(acc.shape[1] // LANES):
        t = acc[:, c * LANES:(c + 1) * LANES]
        lane = lax.broadcasted_iota(jnp.int32, t.shape, 1)
        sq = t * t
        if group == LANES:
            ms = jnp.mean(sq, axis=-1, keepdims=True)
        else:
            lo = lane < group
            s_lo = jnp.sum(jnp.where(lo, sq, 0.0), axis=-1, keepdims=True)
            s_hi = jnp.sum(jnp.where(lo, 0.0, sq), axis=-1, keepdims=True)
            ms = jnp.where(lo, s_lo, s_hi) * (1.0 / group)
        y = t * lax.rsqrt(ms + EPS) * gn
        if cos is not None:
            partner = jnp.take_along_axis(y, lane ^ swap, axis=1)
            y = y * cos + partner * sin
        if out_scale != 1.0:
            y = y * out_scale
        outs.append(y)
    return outs[0] if len(outs) == 1 else jnp.concatenate(outs, axis=1)


def _proj_kernel(*refs, d, q_tiles, k_tiles, group, rope, q_scale):
    if rope:
        x_ref, mod_ref, g_ref, w_ref, qn_ref, kn_ref, cos_ref, sin_ref, o_ref, h_scr = refs
    else:
        x_ref, mod_ref, g_ref, w_ref, qn_ref, kn_ref, o_ref, h_scr = refs
    j = pl.program_id(1)

    @pl.when(j == 0)
    def _():
        mod = mod_ref[0]
        h_scr[...] = _norm_mod(x_ref[...], g_ref[...], mod[:, :d], mod[:, d:2 * d])

    tm, tn = o_ref.shape
    cw = min(tn, MXU_COLS)

    def units(rh, epilogue):
        for r in range(tm // rh):
            rows = slice(r * rh, (r + 1) * rh)
            h = h_scr[rows, :]
            for c in range(tn // cw):
                cols = slice(c * cw, (c + 1) * cw)
                o_ref[rows, cols] = epilogue(_dot(h, w_ref[:, cols]), rows).astype(BF16)

    def head_epilogue(gn_ref, scale):
        def epilogue(acc, rows):
            cos = cos_ref[rows, :] if rope else None
            sin = sin_ref[rows, :] if rope else None
            return _head_norm_rope(acc, gn_ref[...], cos, sin, group, scale)
        return epilogue

    @pl.when(j < q_tiles)
    def _():
        units(min(tm, EPILOGUE_ROWS), head_epilogue(qn_ref, q_scale))

    @pl.when((j >= q_tiles) & (j < q_tiles + k_tiles))
    def _():
        units(min(tm, EPILOGUE_ROWS), head_epilogue(kn_ref, 1.0))

    @pl.when(j >= q_tiles + k_tiles)
    def _():
        units(tm, lambda acc, rows: acc)


def _rope_tables(seq, group):
    rows = seq // GRID_W
    row = jnp.repeat(jnp.arange(rows), GRID_W).astype(F32)
    col = jnp.tile(jnp.arange(GRID_W), rows).astype(F32)
    n_freq = group // 4
    inv_freq = ROPE_BASE ** (-(jnp.arange(n_freq, dtype=F32) / n_freq))
    ar, ac = row[:, None] * inv_freq, col[:, None] * inv_freq
    cos = jnp.concatenate([jnp.cos(ar), jnp.cos(ar), jnp.cos(ac), jnp.cos(ac)], axis=-1)
    sin = jnp.concatenate([-jnp.sin(ar), jnp.sin(ar), -jnp.sin(ac), jnp.sin(ac)], axis=-1)
    reps = LANES // group
    return jnp.tile(cos, (1, reps)), jnp.tile(sin, (1, reps))


def _proj_layer(x2, stream, mod, norm_g, w_in, q_norm, k_norm, q_cols, k_cols, rope, q_scale, first_col=0, n=None):
    m, d = x2.shape
    n = w_in.shape[1] - first_col if n is None else n
    group = q_norm.shape[0]
    tm = min(stream.span, WIDE_ROW_TILE)
    tn = math.gcd(math.gcd(math.gcd(q_cols, k_cols), first_col), 1024)
    reps = LANES // group
    args = [x2, mod, norm_g.reshape(1, d), w_in,
            jnp.tile(q_norm, reps).reshape(1, LANES), jnp.tile(k_norm, reps).reshape(1, LANES)]
    in_specs = [
        pl.BlockSpec((tm, d), lambda i, j: (i, 0)),
        pl.BlockSpec((1, 1, 3 * d), lambda i, j: (stream.mod_row(i * tm), 0, 0)),
        pl.BlockSpec((1, d), lambda i, j: (0, 0)),
        pl.BlockSpec((d, tn), lambda i, j: (0, first_col // tn + j)),
        pl.BlockSpec((1, LANES), lambda i, j: (0, 0)),
        pl.BlockSpec((1, LANES), lambda i, j: (0, 0)),
    ]
    if rope:
        tiles_per_seq = stream.seq // tm
        args += list(_rope_tables(stream.seq, group))
        in_specs += [pl.BlockSpec((tm, LANES), lambda i, j: (i % tiles_per_seq, 0))] * 2
    return pl.pallas_call(
        functools.partial(_proj_kernel, d=d, q_tiles=q_cols // tn, k_tiles=k_cols // tn, group=group,
                          rope=rope, q_scale=q_scale),
        grid=(m // tm, n // tn),
        in_specs=in_specs,
        out_specs=pl.BlockSpec((tm, tn), lambda i, j: (i, j)),
        out_shape=jax.ShapeDtypeStruct((m, n), BF16),
        scratch_shapes=[pltpu.VMEM((tm, d), BF16)],
        compiler_params=_params("parallel", "arbitrary"),
        name="proj_layer",
    )(*args)


def _out_kernel(a_ref, w_ref, x_ref, gate_ref, o_ref):
    o_ref[...] = x_ref[...] + gate_ref[0] * _dot(a_ref[...], w_ref[...])


def _out_layer(a2, x2, stream, mod, w_out):
    m, d = x2.shape
    di = a2.shape[1]
    tm = min(stream.span, WIDE_ROW_TILE)
    tn = _tile(d, 1024)
    gate_block0 = 2 * d // tn
    return pl.pallas_call(
        _out_kernel,
        grid=(m // tm, d // tn),
        in_specs=[
            pl.BlockSpec((tm, di), lambda i, j: (i, 0)),
            pl.BlockSpec((di, tn), lambda i, j: (0, j)),
            pl.BlockSpec((tm, tn), lambda i, j: (i, j)),
            pl.BlockSpec((1, 1, tn), lambda i, j: (stream.mod_row(i * tm), 0, gate_block0 + j)),
        ],
        out_specs=pl.BlockSpec((tm, tn), lambda i, j: (i, j)),
        out_shape=jax.ShapeDtypeStruct((m, d), F32),
        compiler_params=_params("parallel", "arbitrary"),
        name="out_layer",
    )(a2, w_out, x2, mod)


def _lane_chunks(t):
    return [t[:, c * LANES:(c + 1) * LANES] for c in range(t.shape[1] // LANES)]


def _diff_attn_kernel(*refs, tq, tk, n_chunks, has_ctx, tiles, rounds, lam_init):
    pipelined = rounds > 1
    refs = iter(refs)
    lam_ref, q_ref = next(refs), next(refs)
    qn_ref = kn_ref = kcn_ref = kc_ref = vc_ref = vct_scr = None
    if pipelined:
        qn_ref, kn_ref = next(refs), next(refs)
        if has_ctx:
            kcn_ref = next(refs)
    k_ref, v_ref = next(refs), next(refs)
    if has_ctx:
        kc_ref, vc_ref = next(refs), next(refs)
    z_ref, sn_ref, o_ref, vt_scr = next(refs), next(refs), next(refs), next(refs)
    if has_ctx:
        vct_scr = next(refs)
    bufs = list(refs)
    per_set = 2 + has_ctx + pipelined
    sets = []
    for p in range(len(bufs) // per_set):
        group = bufs[p * per_set:(p + 1) * per_set]
        sets.append((group[0], group[1] if has_ctx else None, group[1 + has_ctx], group[-1] if pipelined else None))
    half = LANES // 2

    def transpose_values():
        for c in range(n_chunks):
            vt_scr[:, c * tk:(c + 1) * tk] = v_ref[c * tk:(c + 1) * tk, :].astype(F32).T.astype(BF16)
        if has_ctx:
            vct_scr[...] = vc_ref[...].astype(F32).T.astype(BF16)

    def sublane_groups(s):
        return s.reshape(s.shape[0] // SUBLANES, SUBLANES, s.shape[1])

    def scores(q, buf, t, k_ref=k_ref, kc_ref=kc_ref):
        s_buf, c_buf, m_buf, _ = buf
        lane = lax.broadcasted_iota(jnp.int32, q.shape, 1)
        zero = jnp.zeros_like(q)
        subs = (jnp.where(lane < half, q, zero), jnp.where(lane < half, zero, q))
        for h, qh in enumerate(subs):
            cols = slice(h * tq, (h + 1) * tq)
            mx = jnp.full((SUBLANES, tq), NEG_INF, F32)
            for c in range(n_chunks):
                s = _dot_nt(k_ref[c * tk:(c + 1) * tk, :], qh)
                s_buf[t, c, :, cols] = s
                mx = jnp.maximum(mx, jnp.max(sublane_groups(s), axis=0))
            if has_ctx:
                s = _dot_nt(kc_ref[...], qh)
                c_buf[t, :, cols] = s
                mx = jnp.maximum(mx, jnp.max(sublane_groups(s), axis=0))
            m_buf[t, :, cols] = jnp.broadcast_to(jnp.max(mx, axis=0, keepdims=True), (SUBLANES, tq))

    def weighted(buf, t):
        s_buf, c_buf, m_buf, _ = buf
        m = m_buf[t]
        lsum = jnp.zeros((SUBLANES, 2 * tq), F32)
        acc = jnp.zeros((LANES, 2 * tq), F32)
        blocks = [(s_buf[t, c], vt_scr[:, c * tk:(c + 1) * tk]) for c in range(n_chunks)]
        if has_ctx:
            blocks.append((c_buf[t], vct_scr[...]))
        for s, vt in blocks:
            p = jnp.exp2(sublane_groups(s) - m[None])
            lsum = lsum + jnp.sum(p, axis=0)
            acc = acc + _dot(vt, p.reshape(s.shape).astype(BF16))
        return acc / jnp.sum(lsum, axis=0, keepdims=True)

    lv = lam_ref[...]
    lam = (jnp.exp(jnp.sum(lv[0:1] * lv[1:2], axis=-1, keepdims=True))
           - jnp.exp(jnp.sum(lv[2:3] * lv[3:4], axis=-1, keepdims=True)) + lam_init)

    def finish(o_all, z):
        o = (o_all[:, :tq] - lam * o_all[:, tq:]).T
        ms = jnp.mean(o * o, axis=-1, keepdims=True)
        y = o * lax.rsqrt(ms + EPS) * sn_ref[...] * (1.0 - lam_init)
        return (y * jax.nn.silu(z.astype(F32))).astype(BF16)

    transpose_values()
    if not pipelined:
        scores(q_ref[...], sets[0], 0)
        o_ref[...] = finish(weighted(sets[0], 0), z_ref[...])
        return

    rows = tiles * tq

    @pl.when((pl.program_id(0) == 0) & (pl.program_id(1) == 0))
    def _():
        sets[0][3][...] = jnp.zeros_like(sets[0][3])
        for t in range(tiles):
            scores(q_ref[t * tq:(t + 1) * tq, :], sets[0], t)

    def both(q, k_ref, kc_ref, cur, nxt, t):
        s_nxt, c_nxt, m_nxt, _ = nxt
        s_cur, c_cur, m_cur, _ = cur
        lane = lax.broadcasted_iota(jnp.int32, q.shape, 1)
        zero = jnp.zeros_like(q)
        subs = (jnp.where(lane < half, q, zero), jnp.where(lane < half, zero, q))
        mxs = [jnp.full((SUBLANES, tq), NEG_INF, F32) for _ in subs]
        m = m_cur[t]
        lsum = jnp.zeros((SUBLANES, 2 * tq), F32)
        acc = jnp.zeros((LANES, 2 * tq), F32)
        tie = None
        for c in range(n_chunks + has_ctx):
            if c < n_chunks:
                k, vt, s_old = k_ref[c * tk:(c + 1) * tk, :], vt_scr[:, c * tk:(c + 1) * tk], s_cur[t, c]
            else:
                k, vt, s_old = kc_ref[...], vct_scr[...], c_cur[t]
            if tie is not None:
                k = k + jnp.tile(tie, (k.shape[0] // SUBLANES, 1)).astype(BF16)
            for h, qh in enumerate(subs):
                cols = slice(h * tq, (h + 1) * tq)
                s = _dot_nt(k, qh)
                if c < n_chunks:
                    s_nxt[t, c, :, cols] = s
                else:
                    c_nxt[t, :, cols] = s
                mxs[h] = jnp.maximum(mxs[h], jnp.max(sublane_groups(s), axis=0))
            p = jnp.exp2(sublane_groups(s_old) - m[None])
            lsum = lsum + jnp.sum(p, axis=0)
            acc = acc + _dot(vt, p.reshape(s_old.shape).astype(BF16))
            bits = pltpu.bitcast(p[-1][:, :LANES], jnp.uint32)
            tie = pltpu.bitcast((bits >> 16) >> 16, F32)
        for h, mx in enumerate(mxs):
            m_nxt[t, :, h * tq:(h + 1) * tq] = jnp.broadcast_to(jnp.max(mx, axis=0, keepdims=True), (SUBLANES, tq))
        return acc / jnp.sum(lsum, axis=0, keepdims=True)

    def round_(r, cur, nxt, next_q, k_ref, kc_ref):
        prev = jnp.maximum(r - 1, 0) * rows
        for t in range(tiles):
            out_rows = pl.ds(pl.multiple_of(prev + t * tq, tq), tq)
            o_ref[out_rows, :] = finish(cur[3][t], z_ref[out_rows, :])
            nxt[3][t] = both(next_q(t), k_ref, kc_ref, cur, nxt, t)

    @pl.loop(0, rounds - 1)
    def _(r):
        def next_q(t):
            return q_ref[pl.ds(pl.multiple_of((r + 1) * rows + t * tq, tq), tq), :]

        @pl.when((r & 1) == 0)
        def _():
            round_(r, sets[0], sets[1], next_q, k_ref, kc_ref)

        @pl.when((r & 1) == 1)
        def _():
            round_(r, sets[1], sets[0], next_q, k_ref, kc_ref)

    last = rounds - 1
    cur, nxt = sets[last % 2], sets[1 - last % 2]
    round_(last, cur, nxt, lambda t: qn_ref[t * tq:(t + 1) * tq, :], kn_ref, kcn_ref)
    for t in range(tiles):
        out_rows = slice(last * rows + t * tq, last * rows + (t + 1) * tq)
        o_ref[out_rows, :] = finish(nxt[3][t], z_ref[out_rows, :])


def _diff_attention(qkvz, seq, batch, ctx_qkvz, ctx_len, lam_vecs, sub_norm, lam_init):
    di = qkvz.shape[1] // 4
    heads = di // LANES
    tq = min(seq, 256)
    tk = min(seq, 256)
    tiles = 2 if seq >= 2 * tq else 1
    rounds = seq // (tiles * tq)
    assert rounds == 1 or rounds % 2 == 0
    has_ctx = ctx_qkvz is not None
    pipelined = rounds > 1

    def next_head(index):
        def index_map(b, h):
            wrap = h + 1 == heads
            return index(jnp.minimum(b + wrap, batch - 1), jnp.where(wrap, 0, h + 1))
        return index_map

    head_spec = lambda col0: pl.BlockSpec((seq, LANES), lambda b, h: (b, col0 + h))
    ctx_spec = lambda col0: pl.BlockSpec((ctx_len, LANES), lambda b, h: (b, col0 + h))
    args = [lam_vecs, qkvz]
    in_specs = [pl.BlockSpec(lam_vecs.shape, lambda b, h: (0, 0)), head_spec(0)]
    if pipelined:
        args += [qkvz, qkvz]
        in_specs += [pl.BlockSpec((tiles * tq, LANES), next_head(lambda b, h: (b * rounds, h))),
                     pl.BlockSpec((seq, LANES), next_head(lambda b, h: (b, heads + h)))]
        if has_ctx:
            args += [ctx_qkvz]
            in_specs += [pl.BlockSpec((ctx_len, LANES), next_head(lambda b, h: (b, heads + h)))]
    args += [qkvz, qkvz]
    in_specs += [head_spec(heads), head_spec(2 * heads)]
    if has_ctx:
        args += [ctx_qkvz, ctx_qkvz]
        in_specs += [ctx_spec(heads), ctx_spec(2 * heads)]
    args += [qkvz, sub_norm.reshape(1, LANES)]
    in_specs += [head_spec(3 * heads), pl.BlockSpec((1, LANES), lambda b, h: (0, 0))]
    scratch = [pltpu.VMEM((LANES, seq), BF16)] + ([pltpu.VMEM((LANES, ctx_len), BF16)] if has_ctx else [])
    for _ in range(2 if pipelined else 1):
        scratch += [pltpu.VMEM((tiles, seq // tk, tk, 2 * tq), F32)]
        if has_ctx:
            scratch += [pltpu.VMEM((tiles, ctx_len, 2 * tq), F32)]
        scratch += [pltpu.VMEM((tiles, SUBLANES, 2 * tq), F32)]
        if pipelined:
            scratch += [pltpu.VMEM((tiles, LANES, 2 * tq), F32)]
    return pl.pallas_call(
        functools.partial(_diff_attn_kernel, tq=tq, tk=tk, n_chunks=seq // tk, has_ctx=has_ctx, tiles=tiles,
                          rounds=rounds, lam_init=lam_init),
        grid=(batch, heads),
        in_specs=in_specs,
        out_specs=head_spec(0),
        out_shape=jax.ShapeDtypeStruct((batch * seq, di), BF16),
        scratch_shapes=scratch,
        compiler_params=_params("arbitrary", "arbitrary"),
        name="diff_attention",
    )(*args)


def _diff_self_attn_kernel(lam_ref, q_ref, k_ref, v_ref, z_ref, sn_ref, o_ref, *, heads_per_step, lam_init):
    half = LANES // 2
    lv = lam_ref[...]
    lam = (jnp.exp(jnp.sum(lv[0:1] * lv[1:2], axis=-1, keepdims=True))
           - jnp.exp(jnp.sum(lv[2:3] * lv[3:4], axis=-1, keepdims=True)) + lam_init)
    for j in range(heads_per_step):
        cs = slice(j * LANES, (j + 1) * LANES)
        q, k = q_ref[:, cs], k_ref[:, cs]
        vt = v_ref[:, cs].astype(F32).T.astype(BF16)
        lane = lax.broadcasted_iota(jnp.int32, q.shape, 1)
        zero = jnp.zeros_like(q)
        outs = []
        for qh in (jnp.where(lane < half, q, zero), jnp.where(lane < half, zero, q)):
            s = _dot_nt(k, qh)
            p = jnp.exp2(s - jnp.max(s, axis=0, keepdims=True))
            outs.append(_dot(vt, p.astype(BF16)) / jnp.sum(p, axis=0, keepdims=True))
        o = (outs[0] - lam * outs[1]).T
        ms = jnp.mean(o * o, axis=-1, keepdims=True)
        y = o * lax.rsqrt(ms + EPS) * sn_ref[...] * (1.0 - lam_init)
        o_ref[:, cs] = (y * jax.nn.silu(z_ref[:, cs].astype(F32))).astype(BF16)


def _diff_self_attention(qkvz, seq, batch, lam_vecs, sub_norm, lam_init):
    di = qkvz.shape[1] // 4
    heads = di // LANES
    hs = math.gcd(heads, 8)
    groups = heads // hs
    spec = lambda section: pl.BlockSpec((seq, hs * LANES), lambda b, g: (b, section * groups + g))
    return pl.pallas_call(
        functools.partial(_diff_self_attn_kernel, heads_per_step=hs, lam_init=lam_init),
        grid=(batch, groups),
        in_specs=[pl.BlockSpec(lam_vecs.shape, lambda b, g: (0, 0)), spec(0), spec(1), spec(2), spec(3),
                  pl.BlockSpec((1, LANES), lambda b, g: (0, 0))],
        out_specs=spec(0),
        out_shape=jax.ShapeDtypeStruct((batch * seq, di), BF16),
        compiler_params=_params("parallel", "parallel"),
        name="diff_self_attention",
    )(lam_vecs, qkvz, qkvz, qkvz, qkvz, sub_norm.reshape(1, LANES))


def _win_attn_kernel(*refs, group, kv_heads, n_z):
    q_ref, kp_ref, kc_ref, kn_ref, vp_ref, vc_ref, vn_ref, kx_ref, vx_ref, sink_ref = refs[:10]
    z_refs = refs[10:10 + n_z]
    o_ref = refs[10 + n_z]
    i = pl.program_id(1)
    blk = ATTN_BLOCK
    rows = group * blk
    zw = z_refs[0].shape[1]
    qpos = lax.broadcasted_iota(jnp.int32, (rows, blk), 0) % blk
    kpos = lax.broadcasted_iota(jnp.int32, (rows, blk), 1)
    keep_prev = (kpos >= qpos) & (i > 0)
    keep_next = (kpos <= qpos) & (i < pl.num_programs(1) - 1)
    sink2 = sink_ref[...] * math.log2(math.e)
    for n in range(kv_heads):
        ks = slice(n * LANES, (n + 1) * LANES)
        q4 = jnp.concatenate([q_ref[:, (n * group + g) * LANES:(n * group + g + 1) * LANES] for g in range(group)],
                             axis=0)
        k_all = jnp.concatenate([kp_ref[:, ks], kc_ref[:, ks], kn_ref[:, ks], kx_ref[:, ks]], axis=0)
        v_all = jnp.concatenate([vp_ref[:, ks], vc_ref[:, ks], vn_ref[:, ks], vx_ref[:, ks]], axis=0)
        s = _lane_chunks(_dot_nt(q4, k_all))
        s[0] = jnp.where(keep_prev, s[0], NEG_INF)
        s[2] = jnp.where(keep_next, s[2], NEG_INF)
        sink_b = jnp.concatenate([jnp.broadcast_to(sink2[n, g:g + 1, :], (blk, LANES)) for g in range(group)], axis=0)
        mx = sink_b
        for t in s:
            mx = jnp.maximum(mx, t)
        m = jnp.broadcast_to(jnp.max(mx, axis=-1, keepdims=True), (rows, LANES))
        ps = [jnp.exp2(t - m) for t in s]
        lsum = ps[0]
        for t in ps[1:]:
            lsum = lsum + t
        l = jnp.sum(lsum, axis=-1, keepdims=True) + jnp.exp2(sink_b - m)[:, 0:1]
        o = _dot(jnp.concatenate(ps, axis=1).astype(BF16), v_all) / l
        for g in range(group):
            col = (n * group + g) * LANES
            z = z_refs[col // zw][:, col % zw:col % zw + LANES].astype(F32)
            o_ref[:, col:col + LANES] = (o[g * blk:(g + 1) * blk] * jax.nn.silu(z)).astype(BF16)


def _win_attention(qkvz, seq, batch, ctx_kv, ctx_len, di, kv_heads, sink):
    heads = di // LANES
    group = heads // kv_heads
    kvw = kv_heads * LANES
    blk = ATTN_BLOCK
    nb = seq // blk
    z_col0 = di + 2 * kvw
    zw = math.gcd(z_col0, di)
    n_z = di // zw
    assert di % kvw == 0
    sink_rep = jnp.broadcast_to(sink.astype(F32).reshape(kv_heads, group, 1), (kv_heads, group, LANES))
    band = lambda col_block, shift: pl.BlockSpec(
        (blk, kvw), lambda b, i: (b * nb + jnp.clip(i + shift, 0, nb - 1), col_block))
    k_blk, v_blk = di // kvw, di // kvw + 1
    z_spec = lambda t: pl.BlockSpec((blk, zw), lambda b, i: (b * nb + i, z_col0 // zw + t))
    return pl.pallas_call(
        functools.partial(_win_attn_kernel, group=group, kv_heads=kv_heads, n_z=n_z),
        grid=(batch, nb),
        in_specs=[
            pl.BlockSpec((blk, di), lambda b, i: (b * nb + i, 0)),
            band(k_blk, -1), band(k_blk, 0), band(k_blk, 1),
            band(v_blk, -1), band(v_blk, 0), band(v_blk, 1),
            pl.BlockSpec((ctx_len, kvw), lambda b, i: (b, 0)),
            pl.BlockSpec((ctx_len, kvw), lambda b, i: (b, 1)),
            pl.BlockSpec((kv_heads, group, LANES), lambda b, i: (0, 0, 0)),
        ] + [z_spec(t) for t in range(n_z)],
        out_specs=pl.BlockSpec((blk, di), lambda b, i: (b * nb + i, 0)),
        out_shape=jax.ShapeDtypeStruct((batch * seq, di), BF16),
        compiler_params=_params("parallel", "arbitrary"),
        name="win_attention",
    )(*([qkvz] * 7 + [ctx_kv, ctx_kv, sink_rep] + [qkvz] * n_z))


class _Stream(NamedTuple):
    seq: int
    span: int
    mod_row: Callable


def kernel(x, c, ctx, c_ctx, l0_norm, l0_w_mod, l0_b_mod, l0_w_in, l0_conv_w, l0_conv_b, l0_w_out, l1_norm, l1_w_mod, l1_b_mod, l1_w_in, l1_q_norm, l1_k_norm, l1_lam_q1, l1_lam_k1, l1_lam_q2, l1_lam_k2, l1_sub_norm, l1_w_out, l2_norm, l2_w_mod, l2_b_mod, l2_w_in, l2_q_norm, l2_k_norm, l2_sink, l2_w_out, l3_norm, l3_w_mod, l3_b_mod, l3_w_in, l3_conv_w, l3_conv_b, l3_w_out):
    batch, seq, d = x.shape
    ctx_len = ctx.shape[1]
    di = l0_w_out.shape[0]
    assert batch < MOD_ROWS
    xs = _Stream(seq, seq, lambda row: row // seq)
    cs = _Stream(ctx_len, batch * ctx_len, lambda row: batch)

    x2 = x.reshape(batch * seq, d)
    c2 = ctx.reshape(batch * ctx_len, d)
    cc = jnp.zeros((MOD_ROWS, d), F32).at[:batch].set(c).at[batch].set(c_ctx)
    mod0, mod1, mod2, mod3 = (_modulation(cc, w, b) for w, b in
                              ((l0_w_mod, l0_b_mod), (l1_w_mod, l1_b_mod), (l2_w_mod, l2_b_mod), (l3_w_mod, l3_b_mod)))
    bf = lambda w: w.astype(BF16)

    w_in, w_out = bf(l0_w_in), bf(l0_w_out)
    x2n = _conv_layer(x2, xs, mod0, l0_norm, w_in, l0_conv_w, l0_conv_b, w_out)
    c2 = _conv_layer(c2, cs, mod0, l0_norm, w_in, l0_conv_w, l0_conv_b, w_out)
    x2 = x2n

    w_in, w_out = bf(l1_w_in), bf(l1_w_out)
    lam_init = 0.8 - 0.6 * math.exp(-0.3 * 1)
    q_scale = l1_q_norm.shape[0] ** -0.5 * math.log2(math.e)
    proj = functools.partial(_proj_layer, norm_g=l1_norm, w_in=w_in, q_norm=l1_q_norm, k_norm=l1_k_norm,
                             q_cols=di, k_cols=di, q_scale=q_scale)
    qx = proj(x2, xs, mod1, rope=True)
    qc = proj(c2, cs, mod1, rope=False)
    lam_vecs = jnp.stack([l1_lam_q1, l1_lam_k1, l1_lam_q2, l1_lam_k2]).astype(F32)
    ax = _diff_attention(qx, seq, batch, qc, ctx_len, lam_vecs, l1_sub_norm, lam_init)
    ac = _diff_self_attention(qc, ctx_len, batch, lam_vecs, l1_sub_norm, lam_init)
    x2 = _out_layer(ax, x2, xs, mod1, w_out)
    c2 = _out_layer(ac, c2, cs, mod1, w_out)

    w_in, w_out = bf(l2_w_in), bf(l2_w_out)
    kv_cols = (l2_w_in.shape[1] - 2 * di) // 2
    proj = functools.partial(_proj_layer, norm_g=l2_norm, w_in=w_in, q_norm=l2_q_norm, k_norm=l2_k_norm,
                             q_cols=di, k_cols=kv_cols, q_scale=l2_q_norm.shape[0] ** -0.5 * math.log2(math.e))
    qx = proj(x2, xs, mod2, rope=True)
    kvc = proj(c2, cs, mod2, rope=False, q_cols=0, first_col=di, n=2 * kv_cols)
    ax = _win_attention(qx, seq, batch, kvc, ctx_len, di, kv_cols // LANES, l2_sink)
    x2 = _out_layer(ax, x2, xs, mod2, w_out)

    x2 = _conv_layer(x2, xs, mod3, l3_norm, bf(l3_w_in), l3_conv_w, l3_conv_b, bf(l3_w_out))
    return x2.reshape(batch, seq, d)
```

```python
import functools
import math
from typing import Callable, NamedTuple

import jax
import jax.numpy as jnp
from jax import lax
from jax.experimental import pallas as pl
from jax.experimental.pallas import tpu as pltpu

LANES = 128
SUBLANES = 8
MXU_COLS = 256
BF16_ROWS = 16
GRID_W = 64
ROPE_BASE = 10000.0
EPS = 1e-6
NEG_INF = -1e30
ATTN_BLOCK = 128
EPILOGUE_ROWS = 256
ROW_TILE = 512
WIDE_ROW_TILE = 1024
MOD_ROWS = 8
VMEM_LIMIT = 56 * 1024 * 1024

F32 = jnp.float32
BF16 = jnp.bfloat16


def _params(*sem):
    return pltpu.CompilerParams(dimension_semantics=sem, vmem_limit_bytes=VMEM_LIMIT)


def _tile(n, target):
    if n <= target:
        return n
    t = target - target % LANES
    while n % t:
        t -= LANES
    return t


def _div_mod(x, n):
    if n & (n - 1) == 0:
        return lax.shift_right_logical(x, n.bit_length() - 1), x & (n - 1)
    return x // n, x % n


def _dot(a, b):
    return jnp.dot(a, b, preferred_element_type=F32)


def _dot_nt(a, b):
    return lax.dot_general(a, b, (((1,), (1,)), ((), ())), preferred_element_type=F32)


def _norm_mod(t, g, shift, scale):
    ms = jnp.mean(t * t, axis=-1, keepdims=True)
    y = t * lax.rsqrt(ms + EPS) * g
    return (y * (1.0 + scale) + shift).astype(BF16)


def _mod_kernel(c_ref, w_ref, b_ref, o_ref):
    a = jax.nn.silu(c_ref[...]).astype(BF16)
    o_ref[...] = _dot(a, w_ref[...].astype(BF16)) + b_ref[...]


def _modulation(cc, w_mod, b_mod):
    d, n = w_mod.shape
    tn = _tile(n, 768)
    out = pl.pallas_call(
        _mod_kernel,
        grid=(n // tn,),
        in_specs=[
            pl.BlockSpec((MOD_ROWS, d), lambda j: (0, 0)),
            pl.BlockSpec((d, tn), lambda j: (0, j)),
            pl.BlockSpec((1, tn), lambda j: (0, j)),
        ],
        out_specs=pl.BlockSpec((MOD_ROWS, tn), lambda j: (0, j)),
        out_shape=jax.ShapeDtypeStruct((MOD_ROWS, n), F32),
        compiler_params=_params("arbitrary"),
        name="modulation",
    )(cc, w_mod, b_mod.reshape(1, n))
    return out.reshape(MOD_ROWS, 1, n)


def _conv_kernel(xp_ref, x_ref, xn_ref, mod_ref, g_ref, wb_ref, wc_ref, wx_ref, wz_ref, cw_ref, cb_ref,
                 wo_ref, o_ref, h_scr, *, tm, d, seq):
    i = pl.program_id(0)
    j = pl.program_id(1)
    halo = BF16_ROWS
    mod = mod_ref[0]
    shift, scale, gate = mod[:, :d], mod[:, d:2 * d], mod[:, 2 * d:]

    @pl.when(j == 0)
    def _():
        g = g_ref[...]
        h_scr[0:halo, :] = _norm_mod(xp_ref[...], g, shift, scale)
        h_scr[halo:halo + tm, :] = _norm_mod(x_ref[...], g, shift, scale)
        h_scr[halo + tm:, :] = _norm_mod(xn_ref[...], g, shift, scale)
        o_ref[...] = jnp.zeros_like(o_ref)

    n = tm + 2 * halo
    h = h_scr[...]
    hm = h_scr[halo:halo + tm, :]
    u = _dot(h, wc_ref[...]) * _dot(h, wx_ref[...])
    pos = lax.rem(i * tm + lax.broadcasted_iota(jnp.int32, (tm, 1), 0), seq)
    u_prev = jnp.where(pos == 0, 0.0, pltpu.roll(u, 1, 0)[halo:halo + tm])
    u_next = jnp.where(pos == seq - 1, 0.0, pltpu.roll(u, n - 1, 0)[halo:halo + tm])
    cw = cw_ref[...]
    y = u_prev * cw[0:1] + u[halo:halo + tm] * cw[1:2] + u_next * cw[2:3] + cb_ref[...]
    gated = _dot(hm, wb_ref[...]) * y * jax.nn.silu(_dot(hm, wz_ref[...]))
    o_ref[...] += _dot(gated.astype(BF16), wo_ref[...])

    @pl.when(j == pl.num_programs(1) - 1)
    def _():
        o_ref[...] = x_ref[...] + gate * o_ref[...]


def _conv_layer(x2, stream, mod, norm_g, w_in, conv_w, conv_b, w_out):
    m, d = x2.shape
    di = w_out.shape[0]
    tm = min(stream.span, ROW_TILE)
    tn = _tile(di, 512)
    nj = di // tn
    hb = tm // BF16_ROWS
    last_hb = m // BF16_ROWS - 1
    w_spec = lambda c: pl.BlockSpec((d, tn), lambda i, j: (0, c * nj + j))
    return pl.pallas_call(
        functools.partial(_conv_kernel, tm=tm, d=d, seq=stream.seq),
        grid=(m // tm, nj),
        in_specs=[
            pl.BlockSpec((BF16_ROWS, d), lambda i, j: (jnp.maximum(i * hb - 1, 0), 0)),
            pl.BlockSpec((tm, d), lambda i, j: (i, 0)),
            pl.BlockSpec((BF16_ROWS, d), lambda i, j: (jnp.minimum((i + 1) * hb, last_hb), 0)),
            pl.BlockSpec((1, 1, 3 * d), lambda i, j: (stream.mod_row(i * tm), 0, 0)),
            pl.BlockSpec((1, d), lambda i, j: (0, 0)),
            w_spec(0), w_spec(1), w_spec(2), w_spec(3),
            pl.BlockSpec((3, tn), lambda i, j: (0, j)),
            pl.BlockSpec((1, tn), lambda i, j: (0, j)),
            pl.BlockSpec((tn, d), lambda i, j: (j, 0)),
        ],
        out_specs=pl.BlockSpec((tm, d), lambda i, j: (i, 0)),
        out_shape=jax.ShapeDtypeStruct((m, d), F32),
        scratch_shapes=[pltpu.VMEM((tm + 2 * BF16_ROWS, d), BF16)],
        compiler_params=_params("parallel", "arbitrary"),
        name="conv_layer",
    )(x2, x2, x2, mod, norm_g.reshape(1, d), w_in, w_in, w_in, w_in, conv_w, conv_b.reshape(1, di), w_out)


def _head_norm_rope(acc, gn, cos, sin, group, out_scale):
    outs = []
    swap = group // 4
    for c in range(acc.shape[1] // LANES):
        t = acc[:, c * LANES:(c + 1) * LANES]
        lane = lax.broadcasted_iota(jnp.int32, t.shape, 1)
        sq = t * t
        if group == LANES:
            ms = jnp.mean(sq, axis=-1, keepdims=True)
        else:
            lo = lane < group
            s_lo = jnp.sum(jnp.where(lo, sq, 0.0), axis=-1, keepdims=True)
            s_hi = jnp.sum(jnp.where(lo, 0.0, sq), axis=-1, keepdims=True)
            ms = jnp.where(lo, s_lo, s_hi) * (1.0 / group)
        y = t * lax.rsqrt(ms + EPS) * gn
        if cos is not None:
            partner = jnp.take_along_axis(y, lane ^ swap, axis=1)
            y = y * cos + partner * sin
        if out_scale != 1.0:
            y = y * out_scale
        outs.append(y)
    return outs[0] if len(outs) == 1 else jnp.concatenate(outs, axis=1)


def _proj_kernel(*refs, d, q_tiles, k_tiles, group, rope, q_scale):
    if rope:
        x_ref, mod_ref, g_ref, w_ref, qn_ref, kn_ref, cos_ref, sin_ref, o_ref, h_scr = refs
    else:
        x_ref, mod_ref, g_ref, w_ref, qn_ref, kn_ref, o_ref, h_scr = refs
    j = pl.program_id(1)

    @pl.when(j == 0)
    def _():
        mod = mod_ref[0]
        h_scr[...] = _norm_mod(x_ref[...], g_ref[...], mod[:, :d], mod[:, d:2 * d])

    tm, tn = o_ref.shape
    cw = min(tn, MXU_COLS)

    def units(rh, epilogue):
        for r in range(tm // rh):
            rows = slice(r * rh, (r + 1) * rh)
            h = h_scr[rows, :]
            for c in range(tn // cw):
                cols = slice(c * cw, (c + 1) * cw)
                o_ref[rows, cols] = epilogue(_dot(h, w_ref[:, cols]), rows).astype(BF16)

    def head_epilogue(gn_ref, scale):
        def epilogue(acc, rows):
            cos = cos_ref[rows, :] if rope else None
            sin = sin_ref[rows, :] if rope else None
            return _head_norm_rope(acc, gn_ref[...], cos, sin, group, scale)
        return epilogue

    @pl.when(j < q_tiles)
    def _():
        units(min(tm, EPILOGUE_ROWS), head_epilogue(qn_ref, q_scale))

    @pl.when((j >= q_tiles) & (j < q_tiles + k_tiles))
    def _():
        units(min(tm, EPILOGUE_ROWS), head_epilogue(kn_ref, 1.0))

    @pl.when(j >= q_tiles + k_tiles)
    def _():
        units(tm, lambda acc, rows: acc)


def _rope_tables(seq, group):
    rows = seq // GRID_W
    row = jnp.repeat(jnp.arange(rows), GRID_W).astype(F32)
    col = jnp.tile(jnp.arange(GRID_W), rows).astype(F32)
    n_freq = group // 4
    inv_freq = ROPE_BASE ** (-(jnp.arange(n_freq, dtype=F32) / n_freq))
    ar, ac = row[:, None] * inv_freq, col[:, None] * inv_freq
    cos = jnp.concatenate([jnp.cos(ar), jnp.cos(ar), jnp.cos(ac), jnp.cos(ac)], axis=-1)
    sin = jnp.concatenate([-jnp.sin(ar), jnp.sin(ar), -jnp.sin(ac), jnp.sin(ac)], axis=-1)
    reps = LANES // group
    return jnp.tile(cos, (1, reps)), jnp.tile(sin, (1, reps))


def _proj_layer(x2, stream, mod, norm_g, w_in, q_norm, k_norm, q_cols, k_cols, rope, q_scale, first_col=0, n=None):
    m, d = x2.shape
    n = w_in.shape[1] - first_col if n is None else n
    group = q_norm.shape[0]
    tm = min(stream.span, WIDE_ROW_TILE)
    tn = math.gcd(math.gcd(math.gcd(q_cols, k_cols), first_col), 2048)
    reps = LANES // group
    args = [x2, mod, norm_g.reshape(1, d), w_in,
            jnp.tile(q_norm, reps).reshape(1, LANES), jnp.tile(k_norm, reps).reshape(1, LANES)]
    in_specs = [
        pl.BlockSpec((tm, d), lambda i, j: (i, 0)),
        pl.BlockSpec((1, 1, 3 * d), lambda i, j: (stream.mod_row(i * tm), 0, 0)),
        pl.BlockSpec((1, d), lambda i, j: (0, 0)),
        pl.BlockSpec((d, tn), lambda i, j: (0, first_col // tn + j)),
        pl.BlockSpec((1, LANES), lambda i, j: (0, 0)),
        pl.BlockSpec((1, LANES), lambda i, j: (0, 0)),
    ]
    if rope:
        tiles_per_seq = stream.seq // tm
        args += list(_rope_tables(stream.seq, group))
        in_specs += [pl.BlockSpec((tm, LANES), lambda i, j: (i % tiles_per_seq, 0))] * 2
    return pl.pallas_call(
        functools.partial(_proj_kernel, d=d, q_tiles=q_cols // tn, k_tiles=k_cols // tn, group=group,
                          rope=rope, q_scale=q_scale),
        grid=(m // tm, n // tn),
        in_specs=in_specs,
        out_specs=pl.BlockSpec((tm, tn), lambda i, j: (i, j)),
        out_shape=jax.ShapeDtypeStruct((m, n), BF16),
        scratch_shapes=[pltpu.VMEM((tm, d), BF16)],
        compiler_params=_params("parallel", "arbitrary"),
        name="proj_layer",
    )(*args)


def _out_kernel(a_ref, w_ref, x_ref, gate_ref, o_ref):
    o_ref[...] = x_ref[...] + gate_ref[0] * _dot(a_ref[...], w_ref[...])


def _out_layer(a2, x2, stream, mod, w_out):
    m, d = x2.shape
    di = a2.shape[1]
    tm = min(stream.span, WIDE_ROW_TILE)
    tn = _tile(d, 1024)
    gate_block0 = 2 * d // tn
    return pl.pallas_call(
        _out_kernel,
        grid=(m // tm, d // tn),
        in_specs=[
            pl.BlockSpec((tm, di), lambda i, j: (i, 0)),
            pl.BlockSpec((di, tn), lambda i, j: (0, j)),
            pl.BlockSpec((tm, tn), lambda i, j: (i, j)),
            pl.BlockSpec((1, 1, tn), lambda i, j: (stream.mod_row(i * tm), 0, gate_block0 + j)),
        ],
        out_specs=pl.BlockSpec((tm, tn), lambda i, j: (i, j)),
        out_shape=jax.ShapeDtypeStruct((m, d), F32),
        compiler_params=_params("parallel", "arbitrary"),
        name="out_layer",
    )(a2, w_out, x2, mod)


def _lane_chunks(t):
    return [t[:, c * LANES:(c + 1) * LANES] for c in range(t.shape[1] // LANES)]


def _diff_attn_kernel(*refs, tq, tk, n_chunks, has_ctx, tiles, rounds, lam_init):
    pipelined = rounds > 1
    refs = iter(refs)
    lam_ref, q_ref = next(refs), next(refs)
    qn_ref = kn_ref = kcn_ref = kc_ref = vc_ref = vct_scr = None
    if pipelined:
        qn_ref, kn_ref = next(refs), next(refs)
        if has_ctx:
            kcn_ref = next(refs)
    k_ref, v_ref = next(refs), next(refs)
    if has_ctx:
        kc_ref, vc_ref = next(refs), next(refs)
    z_ref, sn_ref, o_ref, vt_scr = next(refs), next(refs), next(refs), next(refs)
    if has_ctx:
        vct_scr = next(refs)
    bufs = list(refs)
    per_set = 2 + has_ctx + pipelined
    sets = []
    for p in range(len(bufs) // per_set):
        group = bufs[p * per_set:(p + 1) * per_set]
        sets.append((group[0], group[1] if has_ctx else None, group[1 + has_ctx], group[-1] if pipelined else None))
    half = LANES // 2

    def transpose_values():
        for c in range(n_chunks):
            vt_scr[:, c * tk:(c + 1) * tk] = v_ref[c * tk:(c + 1) * tk, :].astype(F32).T.astype(BF16)
        if has_ctx:
            vct_scr[...] = vc_ref[...].astype(F32).T.astype(BF16)

    def sublane_groups(s):
        return s.reshape(s.shape[0] // SUBLANES, SUBLANES, s.shape[1])

    def scores(q, buf, t, k_ref=k_ref, kc_ref=kc_ref):
        s_buf, c_buf, m_buf, _ = buf
        lane = lax.broadcasted_iota(jnp.int32, q.shape, 1)
        zero = jnp.zeros_like(q)
        subs = (jnp.where(lane < half, q, zero), jnp.where(lane < half, zero, q))
        for h, qh in enumerate(subs):
            cols = slice(h * tq, (h + 1) * tq)
            mx = jnp.full((SUBLANES, tq), NEG_INF, F32)
            for c in range(n_chunks):
                s = _dot_nt(k_ref[c * tk:(c + 1) * tk, :], qh)
                s_buf[t, c, :, cols] = s
                mx = jnp.maximum(mx, jnp.max(sublane_groups(s), axis=0))
            if has_ctx:
                s = _dot_nt(kc_ref[...], qh)
                c_buf[t, :, cols] = s
                mx = jnp.maximum(mx, jnp.max(sublane_groups(s), axis=0))
            m_buf[t, :, cols] = jnp.broadcast_to(jnp.max(mx, axis=0, keepdims=True), (SUBLANES, tq))

    def weighted(buf, t):
        s_buf, c_buf, m_buf, _ = buf
        m = m_buf[t]
        lsum = jnp.zeros((SUBLANES, 2 * tq), F32)
        acc = jnp.zeros((LANES, 2 * tq), F32)
        blocks = [(s_buf[t, c], vt_scr[:, c * tk:(c + 1) * tk]) for c in range(n_chunks)]
        if has_ctx:
            blocks.append((c_buf[t], vct_scr[...]))
        for s, vt in blocks:
            p = jnp.exp2(sublane_groups(s) - m[None])
            lsum = lsum + jnp.sum(p, axis=0)
            acc = acc + _dot(vt, p.reshape(s.shape).astype(BF16))
        return acc / jnp.sum(lsum, axis=0, keepdims=True)

    lv = lam_ref[...]
    lam = (jnp.exp(jnp.sum(lv[0:1] * lv[1:2], axis=-1, keepdims=True))
           - jnp.exp(jnp.sum(lv[2:3] * lv[3:4], axis=-1, keepdims=True)) + lam_init)

    def finish(o_all, z):
        o = (o_all[:, :tq] - lam * o_all[:, tq:]).T
        ms = jnp.mean(o * o, axis=-1, keepdims=True)
        y = o * lax.rsqrt(ms + EPS) * sn_ref[...] * (1.0 - lam_init)
        return (y * jax.nn.silu(z.astype(F32))).astype(BF16)

    transpose_values()
    if not pipelined:
        scores(q_ref[...], sets[0], 0)
        o_ref[...] = finish(weighted(sets[0], 0), z_ref[...])
        return

    rows = tiles * tq

    @pl.when((pl.program_id(0) == 0) & (pl.program_id(1) == 0))
    def _():
        sets[0][3][...] = jnp.zeros_like(sets[0][3])
        for t in range(tiles):
            scores(q_ref[t * tq:(t + 1) * tq, :], sets[0], t)

    def both(q, k_ref, kc_ref, cur, nxt, t):
        s_nxt, c_nxt, m_nxt, _ = nxt
        s_cur, c_cur, m_cur, _ = cur
        lane = lax.broadcasted_iota(jnp.int32, q.shape, 1)
        zero = jnp.zeros_like(q)
        subs = (jnp.where(lane < half, q, zero), jnp.where(lane < half, zero, q))
        mxs = [jnp.full((SUBLANES, tq), NEG_INF, F32) for _ in subs]
        m = m_cur[t]
        lsum = jnp.zeros((SUBLANES, 2 * tq), F32)
        acc = jnp.zeros((LANES, 2 * tq), F32)
        tie = None
        for c in range(n_chunks + has_ctx):
            if c < n_chunks:
                k, vt, s_old = k_ref[c * tk:(c + 1) * tk, :], vt_scr[:, c * tk:(c + 1) * tk], s_cur[t, c]
            else:
                k, vt, s_old = kc_ref[...], vct_scr[...], c_cur[t]
            if tie is not None:
                k = k + jnp.tile(tie, (k.shape[0] // SUBLANES, 1)).astype(BF16)
            for h, qh in enumerate(subs):
                cols = slice(h * tq, (h + 1) * tq)
                s = _dot_nt(k, qh)
                if c < n_chunks:
                    s_nxt[t, c, :, cols] = s
                else:
                    c_nxt[t, :, cols] = s
                mxs[h] = jnp.maximum(mxs[h], jnp.max(sublane_groups(s), axis=0))
            p = jnp.exp2(sublane_groups(s_old) - m[None])
            lsum = lsum + jnp.sum(p, axis=0)
            acc = acc + _dot(vt, p.reshape(s_old.shape).astype(BF16))
            bits = pltpu.bitcast(p[-1][:, :LANES], jnp.uint32)
            tie = pltpu.bitcast((bits >> 16) >> 16, F32)
        for h, mx in enumerate(mxs):
            m_nxt[t, :, h * tq:(h + 1) * tq] = jnp.broadcast_to(jnp.max(mx, axis=0, keepdims=True), (SUBLANES, tq))
        return acc / jnp.sum(lsum, axis=0, keepdims=True)

    def round_(r, cur, nxt, next_q, k_ref, kc_ref):
        prev = jnp.maximum(r - 1, 0) * rows
        for t in range(tiles):
            out_rows = pl.ds(pl.multiple_of(prev + t * tq, tq), tq)
            o_ref[out_rows, :] = finish(cur[3][t], z_ref[out_rows, :])
            nxt[3][t] = both(next_q(t), k_ref, kc_ref, cur, nxt, t)

    @pl.loop(0, rounds - 1)
    def _(r):
        def next_q(t):
            return q_ref[pl.ds(pl.multiple_of((r + 1) * rows + t * tq, tq), tq), :]

        @pl.when((r & 1) == 0)
        def _():
            round_(r, sets[0], sets[1], next_q, k_ref, kc_ref)

        @pl.when((r & 1) == 1)
        def _():
            round_(r, sets[1], sets[0], next_q, k_ref, kc_ref)

    last = rounds - 1
    cur, nxt = sets[last % 2], sets[1 - last % 2]
    round_(last, cur, nxt, lambda t: qn_ref[t * tq:(t + 1) * tq, :], kn_ref, kcn_ref)
    for t in range(tiles):
        out_rows = slice(last * rows + t * tq, last * rows + (t + 1) * tq)
        o_ref[out_rows, :] = finish(nxt[3][t], z_ref[out_rows, :])


def _diff_attention(qkvz, seq, batch, ctx_qkvz, ctx_len, lam_vecs, sub_norm, lam_init):
    di = qkvz.shape[1] // 4
    heads = di // LANES
    tq = min(seq, 256)
    tk = min(seq, 256)
    tiles = 2 if seq >= 2 * tq else 1
    rounds = seq // (tiles * tq)
    assert rounds == 1 or rounds % 2 == 0
    has_ctx = ctx_qkvz is not None
    pipelined = rounds > 1

    def next_head(index):
        def index_map(b, h):
            wrap = h + 1 == heads
            return index(jnp.minimum(b + wrap, batch - 1), jnp.where(wrap, 0, h + 1))
        return index_map

    head_spec = lambda col0: pl.BlockSpec((seq, LANES), lambda b, h: (b, col0 + h))
    ctx_spec = lambda col0: pl.BlockSpec((ctx_len, LANES), lambda b, h: (b, col0 + h))
    args = [lam_vecs, qkvz]
    in_specs = [pl.BlockSpec(lam_vecs.shape, lambda b, h: (0, 0)), head_spec(0)]
    if pipelined:
        args += [qkvz, qkvz]
        in_specs += [pl.BlockSpec((tiles * tq, LANES), next_head(lambda b, h: (b * rounds, h))),
                     pl.BlockSpec((seq, LANES), next_head(lambda b, h: (b, heads + h)))]
        if has_ctx:
            args += [ctx_qkvz]
            in_specs += [pl.BlockSpec((ctx_len, LANES), next_head(lambda b, h: (b, heads + h)))]
    args += [qkvz, qkvz]
    in_specs += [head_spec(heads), head_spec(2 * heads)]
    if has_ctx:
        args += [ctx_qkvz, ctx_qkvz]
        in_specs += [ctx_spec(heads), ctx_spec(2 * heads)]
    args += [qkvz, sub_norm.reshape(1, LANES)]
    in_specs += [head_spec(3 * heads), pl.BlockSpec((1, LANES), lambda b, h: (0, 0))]
    scratch = [pltpu.VMEM((LANES, seq), BF16)] + ([pltpu.VMEM((LANES, ctx_len), BF16)] if has_ctx else [])
    for _ in range(2 if pipelined else 1):
        scratch += [pltpu.VMEM((tiles, seq // tk, tk, 2 * tq), F32)]
        if has_ctx:
            scratch += [pltpu.VMEM((tiles, ctx_len, 2 * tq), F32)]
        scratch += [pltpu.VMEM((tiles, SUBLANES, 2 * tq), F32)]
        if pipelined:
            scratch += [pltpu.VMEM((tiles, LANES, 2 * tq), F32)]
    return pl.pallas_call(
        functools.partial(_diff_attn_kernel, tq=tq, tk=tk, n_chunks=seq // tk, has_ctx=has_ctx, tiles=tiles,
                          rounds=rounds, lam_init=lam_init),
        grid=(batch, heads),
        in_specs=in_specs,
        out_specs=head_spec(0),
        out_shape=jax.ShapeDtypeStruct((batch * seq, di), BF16),
        scratch_shapes=scratch,
        compiler_params=_params("arbitrary", "arbitrary"),
        name="diff_attention",
    )(*args)


def _diff_self_attn_kernel(lam_ref, q_ref, k_ref, v_ref, z_ref, sn_ref, o_ref, *, heads_per_step, lam_init):
    half = LANES // 2
    lv = lam_ref[...]
    lam = (jnp.exp(jnp.sum(lv[0:1] * lv[1:2], axis=-1, keepdims=True))
           - jnp.exp(jnp.sum(lv[2:3] * lv[3:4], axis=-1, keepdims=True)) + lam_init)
    for j in range(heads_per_step):
        cs = slice(j * LANES, (j + 1) * LANES)
        q, k = q_ref[:, cs], k_ref[:, cs]
        vt = v_ref[:, cs].astype(F32).T.astype(BF16)
        lane = lax.broadcasted_iota(jnp.int32, q.shape, 1)
        zero = jnp.zeros_like(q)
        outs = []
        for qh in (jnp.where(lane < half, q, zero), jnp.where(lane < half, zero, q)):
            s = _dot_nt(k, qh)
            p = jnp.exp2(s - jnp.max(s, axis=0, keepdims=True))
            outs.append(_dot(vt, p.astype(BF16)) / jnp.sum(p, axis=0, keepdims=True))
        o = (outs[0] - lam * outs[1]).T
        ms = jnp.mean(o * o, axis=-1, keepdims=True)
        y = o * lax.rsqrt(ms + EPS) * sn_ref[...] * (1.0 - lam_init)
        o_ref[:, cs] = (y * jax.nn.silu(z_ref[:, cs].astype(F32))).astype(BF16)


def _diff_self_attention(qkvz, seq, batch, lam_vecs, sub_norm, lam_init):
    di = qkvz.shape[1] // 4
    heads = di // LANES
    hs = math.gcd(heads, 8)
    groups = heads // hs
    spec = lambda section: pl.BlockSpec((seq, hs * LANES), lambda b, g: (b, section * groups + g))
    return pl.pallas_call(
        functools.partial(_diff_self_attn_kernel, heads_per_step=hs, lam_init=lam_init),
        grid=(batch, groups),
        in_specs=[pl.BlockSpec(lam_vecs.shape, lambda b, g: (0, 0)), spec(0), spec(1), spec(2), spec(3),
                  pl.BlockSpec((1, LANES), lambda b, g: (0, 0))],
        out_specs=spec(0),
        out_shape=jax.ShapeDtypeStruct((batch * seq, di), BF16),
        compiler_params=_params("parallel", "parallel"),
        name="diff_self_attention",
    )(lam_vecs, qkvz, qkvz, qkvz, qkvz, sub_norm.reshape(1, LANES))


def _win_attn_kernel(*refs, group, kv_heads, n_z):
    q_ref, kp_ref, kc_ref, kn_ref, vp_ref, vc_ref, vn_ref, kx_ref, vx_ref, sink_ref = refs[:10]
    z_refs = refs[10:10 + n_z]
    o_ref = refs[10 + n_z]
    i = pl.program_id(1)
    blk = ATTN_BLOCK
    rows = group * blk
    zw = z_refs[0].shape[1]
    qpos = lax.broadcasted_iota(jnp.int32, (rows, blk), 0) % blk
    kpos = lax.broadcasted_iota(jnp.int32, (rows, blk), 1)
    keep_prev = (kpos >= qpos) & (i > 0)
    keep_next = (kpos <= qpos) & (i < pl.num_programs(1) - 1)
    sink2 = sink_ref[...] * math.log2(math.e)
    for n in range(kv_heads):
        ks = slice(n * LANES, (n + 1) * LANES)
        q4 = jnp.concatenate([q_ref[:, (n * group + g) * LANES:(n * group + g + 1) * LANES] for g in range(group)],
                             axis=0)
        k_all = jnp.concatenate([kp_ref[:, ks], kc_ref[:, ks], kn_ref[:, ks], kx_ref[:, ks]], axis=0)
        v_all = jnp.concatenate([vp_ref[:, ks], vc_ref[:, ks], vn_ref[:, ks], vx_ref[:, ks]], axis=0)
        s = _lane_chunks(_dot_nt(q4, k_all))
        s[0] = jnp.where(keep_prev, s[0], NEG_INF)
        s[2] = jnp.where(keep_next, s[2], NEG_INF)
        sink_b = jnp.concatenate([jnp.broadcast_to(sink2[n, g:g + 1, :], (blk, LANES)) for g in range(group)], axis=0)
        mx = sink_b
        for t in s:
            mx = jnp.maximum(mx, t)
        m = jnp.broadcast_to(jnp.max(mx, axis=-1, keepdims=True), (rows, LANES))
        ps = [jnp.exp2(t - m) for t in s]
        lsum = ps[0]
        for t in ps[1:]:
            lsum = lsum + t
        l = jnp.sum(lsum, axis=-1, keepdims=True) + jnp.exp2(sink_b - m)[:, 0:1]
        o = _dot(jnp.concatenate(ps, axis=1).astype(BF16), v_all) / l
        for g in range(group):
            col = (n * group + g) * LANES
            z = z_refs[col // zw][:, col % zw:col % zw + LANES].astype(F32)
            o_ref[:, col:col + LANES] = (o[g * blk:(g + 1) * blk] * jax.nn.silu(z)).astype(BF16)


def _win_attention(qkvz, seq, batch, ctx_kv, ctx_len, di, kv_heads, sink):
    heads = di // LANES
    group = heads // kv_heads
    kvw = kv_heads * LANES
    blk = ATTN_BLOCK
    nb = seq // blk
    z_col0 = di + 2 * kvw
    zw = math.gcd(z_col0, di)
    n_z = di // zw
    assert di % kvw == 0
    sink_rep = jnp.broadcast_to(sink.astype(F32).reshape(kv_heads, group, 1), (kv_heads, group, LANES))
    band = lambda col_block, shift: pl.BlockSpec(
        (blk, kvw), lambda b, i: (b * nb + jnp.clip(i + shift, 0, nb - 1), col_block))
    k_blk, v_blk = di // kvw, di // kvw + 1
    z_spec = lambda t: pl.BlockSpec((blk, zw), lambda b, i: (b * nb + i, z_col0 // zw + t))
    return pl.pallas_call(
        functools.partial(_win_attn_kernel, group=group, kv_heads=kv_heads, n_z=n_z),
        grid=(batch, nb),
        in_specs=[
            pl.BlockSpec((blk, di), lambda b, i: (b * nb + i, 0)),
            band(k_blk, -1), band(k_blk, 0), band(k_blk, 1),
            band(v_blk, -1), band(v_blk, 0), band(v_blk, 1),
            pl.BlockSpec((ctx_len, kvw), lambda b, i: (b, 0)),
            pl.BlockSpec((ctx_len, kvw), lambda b, i: (b, 1)),
            pl.BlockSpec((kv_heads, group, LANES), lambda b, i: (0, 0, 0)),
        ] + [z_spec(t) for t in range(n_z)],
        out_specs=pl.BlockSpec((blk, di), lambda b, i: (b * nb + i, 0)),
        out_shape=jax.ShapeDtypeStruct((batch * seq, di), BF16),
        compiler_params=_params("parallel", "arbitrary"),
        name="win_attention",
    )(*([qkvz] * 7 + [ctx_kv, ctx_kv, sink_rep] + [qkvz] * n_z))


class _Stream(NamedTuple):
    seq: int
    span: int
    mod_row: Callable


def kernel(x, c, ctx, c_ctx, l0_norm, l0_w_mod, l0_b_mod, l0_w_in, l0_conv_w, l0_conv_b, l0_w_out, l1_norm, l1_w_mod, l1_b_mod, l1_w_in, l1_q_norm, l1_k_norm, l1_lam_q1, l1_lam_k1, l1_lam_q2, l1_lam_k2, l1_sub_norm, l1_w_out, l2_norm, l2_w_mod, l2_b_mod, l2_w_in, l2_q_norm, l2_k_norm, l2_sink, l2_w_out, l3_norm, l3_w_mod, l3_b_mod, l3_w_in, l3_conv_w, l3_conv_b, l3_w_out):
    batch, seq, d = x.shape
    ctx_len = ctx.shape[1]
    di = l0_w_out.shape[0]
    assert batch < MOD_ROWS
    xs = _Stream(seq, seq, lambda row: row // seq)
    cs = _Stream(ctx_len, batch * ctx_len, lambda row: batch)

    x2 = x.reshape(batch * seq, d)
    c2 = ctx.reshape(batch * ctx_len, d)
    cc = jnp.zeros((MOD_ROWS, d), F32).at[:batch].set(c).at[batch].set(c_ctx)
    mod0, mod1, mod2, mod3 = (_modulation(cc, w, b) for w, b in
                              ((l0_w_mod, l0_b_mod), (l1_w_mod, l1_b_mod), (l2_w_mod, l2_b_mod), (l3_w_mod, l3_b_mod)))
    bf = lambda w: w.astype(BF16)

    w_in, w_out = bf(l0_w_in), bf(l0_w_out)
    x2n = _conv_layer(x2, xs, mod0, l0_norm, w_in, l0_conv_w, l0_conv_b, w_out)
    c2 = _conv_layer(c2, cs, mod0, l0_norm, w_in, l0_conv_w, l0_conv_b, w_out)
    x2 = x2n

    w_in, w_out = bf(l1_w_in), bf(l1_w_out)
    lam_init = 0.8 - 0.6 * math.exp(-0.3 * 1)
    q_scale = l1_q_norm.shape[0] ** -0.5 * math.log2(math.e)
    proj = functools.partial(_proj_layer, norm_g=l1_norm, w_in=w_in, q_norm=l1_q_norm, k_norm=l1_k_norm,
                             q_cols=di, k_cols=di, q_scale=q_scale)
    qx = proj(x2, xs, mod1, rope=True)
    qc = proj(c2, cs, mod1, rope=False)
    lam_vecs = jnp.stack([l1_lam_q1, l1_lam_k1, l1_lam_q2, l1_lam_k2]).astype(F32)
    ax = _diff_attention(qx, seq, batch, qc, ctx_len, lam_vecs, l1_sub_norm, lam_init)
    ac = _diff_self_attention(qc, ctx_len, batch, lam_vecs, l1_sub_norm, lam_init)
    x2 = _out_layer(ax, x2, xs, mod1, w_out)
    c2 = _out_layer(ac, c2, cs, mod1, w_out)

    w_in, w_out = bf(l2_w_in), bf(l2_w_out)
    kv_cols = (l2_w_in.shape[1] - 2 * di) // 2
    proj = functools.partial(_proj_layer, norm_g=l2_norm, w_in=w_in, q_norm=l2_q_norm, k_norm=l2_k_norm,
                             q_cols=di, k_cols=kv_cols, q_scale=l2_q_norm.shape[0] ** -0.5 * math.log2(math.e))
    qx = proj(x2, xs, mod2, rope=True)
    kvc = proj(c2, cs, mod2, rope=False, q_cols=0, first_col=di, n=2 * kv_cols)
    ax = _win_attention(qx, seq, batch, kvc, ctx_len, di, kv_cols // LANES, l2_sink)
    x2 = _out_layer(ax, x2, xs, mod2, w_out)

    x2 = _conv_layer(x2, xs, mod3, l3_norm, bf(l3_w_in), l3_conv_w, l3_conv_b, bf(l3_w_out))
    return x2.reshape(batch, seq, d)
```

```python
import functools
import math
from typing import Callable, NamedTuple

import jax
import jax.numpy as jnp
from jax import lax
from jax.experimental import pallas as pl
from jax.experimental.pallas import tpu as pltpu

LANES = 128
SUBLANES = 8
MXU_COLS = 256
BF16_ROWS = 16
GRID_W = 64
ROPE_BASE = 10000.0
EPS = 1e-6
NEG_INF = -1e30
ATTN_BLOCK = 128
EPILOGUE_ROWS = 256
ROW_TILE = 512
WIDE_ROW_TILE = 1024
MOD_ROWS = 8
VMEM_LIMIT = 56 * 1024 * 1024

F32 = jnp.float32
BF16 = jnp.bfloat16


def _params(*sem):
    return pltpu.CompilerParams(dimension_semantics=sem, vmem_limit_bytes=VMEM_LIMIT)


def _tile(n, target):
    if n <= target:
        return n
    t = target - target % LANES
    while n % t:
        t -= LANES
    return t


def _div_mod(x, n):
    if n & (n - 1) == 0:
        return lax.shift_right_logical(x, n.bit_length() - 1), x & (n - 1)
    return x // n, x % n


def _dot(a, b):
    return jnp.dot(a, b, preferred_element_type=F32)


def _dot_nt(a, b):
    return lax.dot_general(a, b, (((1,), (1,)), ((), ())), preferred_element_type=F32)


def _norm_mod(t, g, shift, scale):
    ms = jnp.mean(t * t, axis=-1, keepdims=True)
    y = t * lax.rsqrt(ms + EPS) * g
    return (y * (1.0 + scale) + shift).astype(BF16)


def _mod_kernel(c_ref, w_ref, b_ref, o_ref):
    a = jax.nn.silu(c_ref[...]).astype(BF16)
    o_ref[...] = _dot(a, w_ref[...].astype(BF16)) + b_ref[...]


def _modulation(cc, w_mod, b_mod):
    d, n = w_mod.shape
    tn = _tile(n, 768)
    out = pl.pallas_call(
        _mod_kernel,
        grid=(n // tn,),
        in_specs=[
            pl.BlockSpec((MOD_ROWS, d), lambda j: (0, 0)),
            pl.BlockSpec((d, tn), lambda j: (0, j)),
            pl.BlockSpec((1, tn), lambda j: (0, j)),
        ],
        out_specs=pl.BlockSpec((MOD_ROWS, tn), lambda j: (0, j)),
        out_shape=jax.ShapeDtypeStruct((MOD_ROWS, n), F32),
        compiler_params=_params("arbitrary"),
        name="modulation",
    )(cc, w_mod, b_mod.reshape(1, n))
    return out.reshape(MOD_ROWS, 1, n)


def _conv_kernel(xp_ref, x_ref, xn_ref, mod_ref, g_ref, wb_ref, wc_ref, wx_ref, wz_ref, cw_ref, cb_ref,
                 wo_ref, o_ref, h_scr, *, tm, d, seq):
    i = pl.program_id(0)
    j = pl.program_id(1)
    halo = BF16_ROWS
    mod = mod_ref[0]
    shift, scale, gate = mod[:, :d], mod[:, d:2 * d], mod[:, 2 * d:]

    @pl.when(j == 0)
    def _():
        g = g_ref[...]
        h_scr[0:halo, :] = _norm_mod(xp_ref[...], g, shift, scale)
        h_scr[halo:halo + tm, :] = _norm_mod(x_ref[...], g, shift, scale)
        h_scr[halo + tm:, :] = _norm_mod(xn_ref[...], g, shift, scale)
        o_ref[...] = jnp.zeros_like(o_ref)

    n = tm + 2 * halo
    h = h_scr[...]
    hm = h_scr[halo:halo + tm, :]
    u = _dot(h, wc_ref[...]) * _dot(h, wx_ref[...])
    pos = lax.rem(i * tm + lax.broadcasted_iota(jnp.int32, (tm, 1), 0), seq)
    u_prev = jnp.where(pos == 0, 0.0, pltpu.roll(u, 1, 0)[halo:halo + tm])
    u_next = jnp.where(pos == seq - 1, 0.0, pltpu.roll(u, n - 1, 0)[halo:halo + tm])
    cw = cw_ref[...]
    y = u_prev * cw[0:1] + u[halo:halo + tm] * cw[1:2] + u_next * cw[2:3] + cb_ref[...]
    gated = _dot(hm, wb_ref[...]) * y * jax.nn.silu(_dot(hm, wz_ref[...]))
    o_ref[...] += _dot(gated.astype(BF16), wo_ref[...])

    @pl.when(j == pl.num_programs(1) - 1)
    def _():
        o_ref[...] = x_ref[...] + gate * o_ref[...]


def _conv_layer(x2, stream, mod, norm_g, w_in, conv_w, conv_b, w_out):
    m, d = x2.shape
    di = w_out.shape[0]
    tm = min(stream.span, ROW_TILE)
    tn = _tile(di, 512)
    nj = di // tn
    hb = tm // BF16_ROWS
    last_hb = m // BF16_ROWS - 1
    w_spec = lambda c: pl.BlockSpec((d, tn), lambda i, j: (0, c * nj + j))
    return pl.pallas_call(
        functools.partial(_conv_kernel, tm=tm, d=d, seq=stream.seq),
        grid=(m // tm, nj),
        in_specs=[
            pl.BlockSpec((BF16_ROWS, d), lambda i, j: (jnp.maximum(i * hb - 1, 0), 0)),
            pl.BlockSpec((tm, d), lambda i, j: (i, 0)),
            pl.BlockSpec((BF16_ROWS, d), lambda i, j: (jnp.minimum((i + 1) * hb, last_hb), 0)),
            pl.BlockSpec((1, 1, 3 * d), lambda i, j: (stream.mod_row(i * tm), 0, 0)),
            pl.BlockSpec((1, d), lambda i, j: (0, 0)),
            w_spec(0), w_spec(1), w_spec(2), w_spec(3),
            pl.BlockSpec((3, tn), lambda i, j: (0, j)),
            pl.BlockSpec((1, tn), lambda i, j: (0, j)),
            pl.BlockSpec((tn, d), lambda i, j: (j, 0)),
        ],
        out_specs=pl.BlockSpec((tm, d), lambda i, j: (i, 0)),
        out_shape=jax.ShapeDtypeStruct((m, d), F32),
        scratch_shapes=[pltpu.VMEM((tm + 2 * BF16_ROWS, d), BF16)],
        compiler_params=_params("parallel", "arbitrary"),
        name="conv_layer",
    )(x2, x2, x2, mod, norm_g.reshape(1, d), w_in, w_in, w_in, w_in, conv_w, conv_b.reshape(1, di), w_out)


def _head_norm_rope(acc, gn, cos, sin, group, out_scale):
    outs = []
    swap = group // 4
    for c in range(acc.shape[1] // LANES):
        t = acc[:, c * LANES:(c + 1) * LANES]
        lane = lax.broadcasted_iota(jnp.int32, t.shape, 1)
        sq = t * t
        if group == LANES:
            ms = jnp.mean(sq, axis=-1, keepdims=True)
        else:
            lo = lane < group
            s_lo = jnp.sum(jnp.where(lo, sq, 0.0), axis=-1, keepdims=True)
            s_hi = jnp.sum(jnp.where(lo, 0.0, sq), axis=-1, keepdims=True)
            ms = jnp.where(lo, s_lo, s_hi) * (1.0 / group)
        y = t * lax.rsqrt(ms + EPS) * gn
        if cos is not None:
            partner = jnp.take_along_axis(y, lane ^ swap, axis=1)
            y = y * cos + partner * sin
        if out_scale != 1.0:
            y = y * out_scale
        outs.append(y)
    return outs[0] if len(outs) == 1 else jnp.concatenate(outs, axis=1)


def _proj_kernel(*refs, d, q_tiles, k_tiles, k_head_cols, group, rope, q_scale):
    if rope:
        x_ref, mod_ref, g_ref, w_ref, qn_ref, kn_ref, cos_ref, sin_ref, o_ref, h_scr = refs
    else:
        x_ref, mod_ref, g_ref, w_ref, qn_ref, kn_ref, o_ref, h_scr = refs
    j = pl.program_id(1)

    @pl.when(j == 0)
    def _():
        mod = mod_ref[0]
        h_scr[...] = _norm_mod(x_ref[...], g_ref[...], mod[:, :d], mod[:, d:2 * d])

    tm, tn = o_ref.shape
    cw = min(tn, MXU_COLS)

    def units(rh, epilogue, head_cols):
        for r in range(tm // rh):
            rows = slice(r * rh, (r + 1) * rh)
            h = h_scr[rows, :]
            for c in range(tn // cw):
                cols = slice(c * cw, (c + 1) * cw)
                acc = _dot(h, w_ref[:, cols])
                o_ref[rows, cols] = (epilogue(acc, rows) if c * cw < head_cols else acc).astype(BF16)

    def head_epilogue(gn_ref, scale):
        def epilogue(acc, rows):
            cos = cos_ref[rows, :] if rope else None
            sin = sin_ref[rows, :] if rope else None
            return _head_norm_rope(acc, gn_ref[...], cos, sin, group, scale)
        return epilogue

    @pl.when(j < q_tiles)
    def _():
        units(min(tm, EPILOGUE_ROWS), head_epilogue(qn_ref, q_scale), tn)

    @pl.when((j >= q_tiles) & (j < q_tiles + k_tiles))
    def _():
        units(min(tm, EPILOGUE_ROWS), head_epilogue(kn_ref, 1.0), k_head_cols)

    @pl.when(j >= q_tiles + k_tiles)
    def _():
        units(tm, None, 0)


def _rope_tables(seq, group):
    rows = seq // GRID_W
    row = jnp.repeat(jnp.arange(rows), GRID_W).astype(F32)
    col = jnp.tile(jnp.arange(GRID_W), rows).astype(F32)
    n_freq = group // 4
    inv_freq = ROPE_BASE ** (-(jnp.arange(n_freq, dtype=F32) / n_freq))
    ar, ac = row[:, None] * inv_freq, col[:, None] * inv_freq
    cos = jnp.concatenate([jnp.cos(ar), jnp.cos(ar), jnp.cos(ac), jnp.cos(ac)], axis=-1)
    sin = jnp.concatenate([-jnp.sin(ar), jnp.sin(ar), -jnp.sin(ac), jnp.sin(ac)], axis=-1)
    reps = LANES // group
    return jnp.tile(cos, (1, reps)), jnp.tile(sin, (1, reps))


def _proj_layer(x2, stream, mod, norm_g, w_in, q_norm, k_norm, q_cols, k_cols, rope, q_scale, first_col=0, n=None):
    m, d = x2.shape
    n = w_in.shape[1] - first_col if n is None else n
    group = q_norm.shape[0]
    tm = min(stream.span, WIDE_ROW_TILE)
    tn = math.gcd(math.gcd(q_cols, first_col), 2048)
    if k_cols % tn == 0:
        k_tiles, k_head_cols = k_cols // tn, tn
    else:
        assert k_cols < tn and k_cols % MXU_COLS == 0 and (first_col + q_cols) % tn == 0
        k_tiles, k_head_cols = 1, k_cols
    reps = LANES // group
    args = [x2, mod, norm_g.reshape(1, d), w_in,
            jnp.tile(q_norm, reps).reshape(1, LANES), jnp.tile(k_norm, reps).reshape(1, LANES)]
    in_specs = [
        pl.BlockSpec((tm, d), lambda i, j: (i, 0)),
        pl.BlockSpec((1, 1, 3 * d), lambda i, j: (stream.mod_row(i * tm), 0, 0)),
        pl.BlockSpec((1, d), lambda i, j: (0, 0)),
        pl.BlockSpec((d, tn), lambda i, j: (0, first_col // tn + j)),
        pl.BlockSpec((1, LANES), lambda i, j: (0, 0)),
        pl.BlockSpec((1, LANES), lambda i, j: (0, 0)),
    ]
    if rope:
        tiles_per_seq = stream.seq // tm
        args += list(_rope_tables(stream.seq, group))
        in_specs += [pl.BlockSpec((tm, LANES), lambda i, j: (i % tiles_per_seq, 0))] * 2
    return pl.pallas_call(
        functools.partial(_proj_kernel, d=d, q_tiles=q_cols // tn, k_tiles=k_tiles, k_head_cols=k_head_cols, group=group,
                          rope=rope, q_scale=q_scale),
        grid=(m // tm, n // tn),
        in_specs=in_specs,
        out_specs=pl.BlockSpec((tm, tn), lambda i, j: (i, j)),
        out_shape=jax.ShapeDtypeStruct((m, n), BF16),
        scratch_shapes=[pltpu.VMEM((tm, d), BF16)],
        compiler_params=_params("parallel", "arbitrary"),
        name="proj_layer",
    )(*args)


def _out_kernel(a_ref, w_ref, x_ref, gate_ref, o_ref):
    o_ref[...] = x_ref[...] + gate_ref[0] * _dot(a_ref[...], w_ref[...])


def _out_layer(a2, x2, stream, mod, w_out):
    m, d = x2.shape
    di = a2.shape[1]
    tm = min(stream.span, WIDE_ROW_TILE)
    tn = _tile(d, 1024)
    gate_block0 = 2 * d // tn
    return pl.pallas_call(
        _out_kernel,
        grid=(m // tm, d // tn),
        in_specs=[
            pl.BlockSpec((tm, di), lambda i, j: (i, 0)),
            pl.BlockSpec((di, tn), lambda i, j: (0, j)),
            pl.BlockSpec((tm, tn), lambda i, j: (i, j)),
            pl.BlockSpec((1, 1, tn), lambda i, j: (stream.mod_row(i * tm), 0, gate_block0 + j)),
        ],
        out_specs=pl.BlockSpec((tm, tn), lambda i, j: (i, j)),
        out_shape=jax.ShapeDtypeStruct((m, d), F32),
        compiler_params=_params("parallel", "arbitrary"),
        name="out_layer",
    )(a2, w_out, x2, mod)


def _lane_chunks(t):
    return [t[:, c * LANES:(c + 1) * LANES] for c in range(t.shape[1] // LANES)]


def _diff_attn_kernel(*refs, tq, tk, n_chunks, has_ctx, tiles, rounds, lam_init):
    pipelined = rounds > 1
    refs = iter(refs)
    lam_ref, q_ref = next(refs), next(refs)
    qn_ref = kn_ref = kcn_ref = kc_ref = vc_ref = vct_scr = None
    if pipelined:
        qn_ref, kn_ref = next(refs), next(refs)
        if has_ctx:
            kcn_ref = next(refs)
    k_ref, v_ref = next(refs), next(refs)
    if has_ctx:
        kc_ref, vc_ref = next(refs), next(refs)
    z_ref, sn_ref, o_ref, vt_scr = next(refs), next(refs), next(refs), next(refs)
    if has_ctx:
        vct_scr = next(refs)
    bufs = list(refs)
    per_set = 2 + has_ctx + pipelined
    sets = []
    for p in range(len(bufs) // per_set):
        group = bufs[p * per_set:(p + 1) * per_set]
        sets.append((group[0], group[1] if has_ctx else None, group[1 + has_ctx], group[-1] if pipelined else None))
    half = LANES // 2

    def transpose_values():
        for c in range(n_chunks):
            vt_scr[:, c * tk:(c + 1) * tk] = v_ref[c * tk:(c + 1) * tk, :].astype(F32).T.astype(BF16)
        if has_ctx:
            vct_scr[...] = vc_ref[...].astype(F32).T.astype(BF16)

    def sublane_groups(s):
        return s.reshape(s.shape[0] // SUBLANES, SUBLANES, s.shape[1])

    def scores(q, buf, t, k_ref=k_ref, kc_ref=kc_ref):
        s_buf, c_buf, m_buf, _ = buf
        lane = lax.broadcasted_iota(jnp.int32, q.shape, 1)
        zero = jnp.zeros_like(q)
        subs = (jnp.where(lane < half, q, zero), jnp.where(lane < half, zero, q))
        for h, qh in enumerate(subs):
            cols = slice(h * tq, (h + 1) * tq)
            mx = jnp.full((SUBLANES, tq), NEG_INF, F32)
            for c in range(n_chunks):
                s = _dot_nt(k_ref[c * tk:(c + 1) * tk, :], qh)
                s_buf[t, c, :, cols] = s
                mx = jnp.maximum(mx, jnp.max(sublane_groups(s), axis=0))
            if has_ctx:
                s = _dot_nt(kc_ref[...], qh)
                c_buf[t, :, cols] = s
                mx = jnp.maximum(mx, jnp.max(sublane_groups(s), axis=0))
            m_buf[t, :, cols] = jnp.broadcast_to(jnp.max(mx, axis=0, keepdims=True), (SUBLANES, tq))

    def weighted(buf, t):
        s_buf, c_buf, m_buf, _ = buf
        m = m_buf[t]
        lsum = jnp.zeros((SUBLANES, 2 * tq), F32)
        acc = jnp.zeros((LANES, 2 * tq), F32)
        blocks = [(s_buf[t, c], vt_scr[:, c * tk:(c + 1) * tk]) for c in range(n_chunks)]
        if has_ctx:
            blocks.append((c_buf[t], vct_scr[...]))
        for s, vt in blocks:
            p = jnp.exp2(sublane_groups(s) - m[None])
            lsum = lsum + jnp.sum(p, axis=0)
            acc = acc + _dot(vt, p.reshape(s.shape).astype(BF16))
        return acc / jnp.sum(lsum, axis=0, keepdims=True)

    lv = lam_ref[...]
    lam = (jnp.exp(jnp.sum(lv[0:1] * lv[1:2], axis=-1, keepdims=True))
           - jnp.exp(jnp.sum(lv[2:3] * lv[3:4], axis=-1, keepdims=True)) + lam_init)

    def finish(o_all, z):
        o = (o_all[:, :tq] - lam * o_all[:, tq:]).T
        ms = jnp.mean(o * o, axis=-1, keepdims=True)
        y = o * lax.rsqrt(ms + EPS) * sn_ref[...] * (1.0 - lam_init)
        return (y * jax.nn.silu(z.astype(F32))).astype(BF16)

    transpose_values()
    if not pipelined:
        scores(q_ref[...], sets[0], 0)
        o_ref[...] = finish(weighted(sets[0], 0), z_ref[...])
        return

    rows = tiles * tq

    @pl.when((pl.program_id(0) == 0) & (pl.program_id(1) == 0))
    def _():
        sets[0][3][...] = jnp.zeros_like(sets[0][3])
        for t in range(tiles):
            scores(q_ref[t * tq:(t + 1) * tq, :], sets[0], t)

    def both(q, k_ref, kc_ref, cur, nxt, t):
        s_nxt, c_nxt, m_nxt, _ = nxt
        s_cur, c_cur, m_cur, _ = cur
        lane = lax.broadcasted_iota(jnp.int32, q.shape, 1)
        zero = jnp.zeros_like(q)
        subs = (jnp.where(lane < half, q, zero), jnp.where(lane < half, zero, q))
        mxs = [jnp.full((SUBLANES, tq), NEG_INF, F32) for _ in subs]
        m = m_cur[t]
        lsum = jnp.zeros((SUBLANES, 2 * tq), F32)
        acc = jnp.zeros((LANES, 2 * tq), F32)
        tie = None
        for c in range(n_chunks + has_ctx):
            if c < n_chunks:
                k, vt, s_old = k_ref[c * tk:(c + 1) * tk, :], vt_scr[:, c * tk:(c + 1) * tk], s_cur[t, c]
            else:
                k, vt, s_old = kc_ref[...], vct_scr[...], c_cur[t]
            if tie is not None:
                k = k + jnp.tile(tie, (k.shape[0] // SUBLANES, 1)).astype(BF16)
            for h, qh in enumerate(subs):
                cols = slice(h * tq, (h + 1) * tq)
                s = _dot_nt(k, qh)
                if c < n_chunks:
                    s_nxt[t, c, :, cols] = s
                else:
                    c_nxt[t, :, cols] = s
                mxs[h] = jnp.maximum(mxs[h], jnp.max(sublane_groups(s), axis=0))
            p = jnp.exp2(sublane_groups(s_old) - m[None])
            lsum = lsum + jnp.sum(p, axis=0)
            acc = acc + _dot(vt, p.reshape(s_old.shape).astype(BF16))
            bits = pltpu.bitcast(p[-1][:, :LANES], jnp.uint32)
            tie = pltpu.bitcast((bits >> 16) >> 16, F32)
        for h, mx in enumerate(mxs):
            m_nxt[t, :, h * tq:(h + 1) * tq] = jnp.broadcast_to(jnp.max(mx, axis=0, keepdims=True), (SUBLANES, tq))
        return acc / jnp.sum(lsum, axis=0, keepdims=True)

    def round_(r, cur, nxt, next_q, k_ref, kc_ref):
        prev = jnp.maximum(r - 1, 0) * rows
        for t in range(tiles):
            out_rows = pl.ds(pl.multiple_of(prev + t * tq, tq), tq)
            o_ref[out_rows, :] = finish(cur[3][t], z_ref[out_rows, :])
            nxt[3][t] = both(next_q(t), k_ref, kc_ref, cur, nxt, t)

    @pl.loop(0, rounds - 1)
    def _(r):
        def next_q(t):
            return q_ref[pl.ds(pl.multiple_of((r + 1) * rows + t * tq, tq), tq), :]

        @pl.when((r & 1) == 0)
        def _():
            round_(r, sets[0], sets[1], next_q, k_ref, kc_ref)

        @pl.when((r & 1) == 1)
        def _():
            round_(r, sets[1], sets[0], next_q, k_ref, kc_ref)

    last = rounds - 1
    cur, nxt = sets[last % 2], sets[1 - last % 2]
    round_(last, cur, nxt, lambda t: qn_ref[t * tq:(t + 1) * tq, :], kn_ref, kcn_ref)
    for t in range(tiles):
        out_rows = slice(last * rows + t * tq, last * rows + (t + 1) * tq)
        o_ref[out_rows, :] = finish(nxt[3][t], z_ref[out_rows, :])


def _diff_attention(qkvz, seq, batch, ctx_qkvz, ctx_len, lam_vecs, sub_norm, lam_init):
    di = qkvz.shape[1] // 4
    heads = di // LANES
    tq = min(seq, 256)
    tk = min(seq, 256)
    tiles = 2 if seq >= 2 * tq else 1
    rounds = seq // (tiles * tq)
    assert rounds == 1 or rounds % 2 == 0
    has_ctx = ctx_qkvz is not None
    pipelined = rounds > 1

    def next_head(index):
        def index_map(b, h):
            wrap = h + 1 == heads
            return index(jnp.minimum(b + wrap, batch - 1), jnp.where(wrap, 0, h + 1))
        return index_map

    head_spec = lambda col0: pl.BlockSpec((seq, LANES), lambda b, h: (b, col0 + h))
    ctx_spec = lambda col0: pl.BlockSpec((ctx_len, LANES), lambda b, h: (b, col0 + h))
    args = [lam_vecs, qkvz]
    in_specs = [pl.BlockSpec(lam_vecs.shape, lambda b, h: (0, 0)), head_spec(0)]
    if pipelined:
        args += [qkvz, qkvz]
        in_specs += [pl.BlockSpec((tiles * tq, LANES), next_head(lambda b, h: (b * rounds, h))),
                     pl.BlockSpec((seq, LANES), next_head(lambda b, h: (b, heads + h)))]
        if has_ctx:
            args += [ctx_qkvz]
            in_specs += [pl.BlockSpec((ctx_len, LANES), next_head(lambda b, h: (b, heads + h)))]
    args += [qkvz, qkvz]
    in_specs += [head_spec(heads), head_spec(2 * heads)]
    if has_ctx:
        args += [ctx_qkvz, ctx_qkvz]
        in_specs += [ctx_spec(heads), ctx_spec(2 * heads)]
    args += [qkvz, sub_norm.reshape(1, LANES)]
    in_specs += [head_spec(3 * heads), pl.BlockSpec((1, LANES), lambda b, h: (0, 0))]
    scratch = [pltpu.VMEM((LANES, seq), BF16)] + ([pltpu.VMEM((LANES, ctx_len), BF16)] if has_ctx else [])
    for _ in range(2 if pipelined else 1):
        scratch += [pltpu.VMEM((tiles, seq // tk, tk, 2 * tq), F32)]
        if has_ctx:
            scratch += [pltpu.VMEM((tiles, ctx_len, 2 * tq), F32)]
        scratch += [pltpu.VMEM((tiles, SUBLANES, 2 * tq), F32)]
        if pipelined:
            scratch += [pltpu.VMEM((tiles, LANES, 2 * tq), F32)]
    return pl.pallas_call(
        functools.partial(_diff_attn_kernel, tq=tq, tk=tk, n_chunks=seq // tk, has_ctx=has_ctx, tiles=tiles,
                          rounds=rounds, lam_init=lam_init),
        grid=(batch, heads),
        in_specs=in_specs,
        out_specs=head_spec(0),
        out_shape=jax.ShapeDtypeStruct((batch * seq, di), BF16),
        scratch_shapes=scratch,
        compiler_params=_params("arbitrary", "arbitrary"),
        name="diff_attention",
    )(*args)


def _diff_self_attn_kernel(lam_ref, q_ref, k_ref, v_ref, z_ref, sn_ref, o_ref, *, heads_per_step, lam_init):
    half = LANES // 2
    lv = lam_ref[...]
    lam = (jnp.exp(jnp.sum(lv[0:1] * lv[1:2], axis=-1, keepdims=True))
           - jnp.exp(jnp.sum(lv[2:3] * lv[3:4], axis=-1, keepdims=True)) + lam_init)
    for j in range(heads_per_step):
        cs = slice(j * LANES, (j + 1) * LANES)
        q, k = q_ref[:, cs], k_ref[:, cs]
        vt = v_ref[:, cs].astype(F32).T.astype(BF16)
        lane = lax.broadcasted_iota(jnp.int32, q.shape, 1)
        zero = jnp.zeros_like(q)
        outs = []
        for qh in (jnp.where(lane < half, q, zero), jnp.where(lane < half, zero, q)):
            s = _dot_nt(k, qh)
            p = jnp.exp2(s - jnp.max(s, axis=0, keepdims=True))
            outs.append(_dot(vt, p.astype(BF16)) / jnp.sum(p, axis=0, keepdims=True))
        o = (outs[0] - lam * outs[1]).T
        ms = jnp.mean(o * o, axis=-1, keepdims=True)
        y = o * lax.rsqrt(ms + EPS) * sn_ref[...] * (1.0 - lam_init)
        o_ref[:, cs] = (y * jax.nn.silu(z_ref[:, cs].astype(F32))).astype(BF16)


def _diff_self_attention(qkvz, seq, batch, lam_vecs, sub_norm, lam_init):
    di = qkvz.shape[1] // 4
    heads = di // LANES
    hs = math.gcd(heads, 8)
    groups = heads // hs
    spec = lambda section: pl.BlockSpec((seq, hs * LANES), lambda b, g: (b, section * groups + g))
    return pl.pallas_call(
        functools.partial(_diff_self_attn_kernel, heads_per_step=hs, lam_init=lam_init),
        grid=(batch, groups),
        in_specs=[pl.BlockSpec(lam_vecs.shape, lambda b, g: (0, 0)), spec(0), spec(1), spec(2), spec(3),
                  pl.BlockSpec((1, LANES), lambda b, g: (0, 0))],
        out_specs=spec(0),
        out_shape=jax.ShapeDtypeStruct((batch * seq, di), BF16),
        compiler_params=_params("parallel", "parallel"),
        name="diff_self_attention",
    )(lam_vecs, qkvz, qkvz, qkvz, qkvz, sub_norm.reshape(1, LANES))


def _win_attn_kernel(*refs, group, kv_heads, n_z):
    q_ref, kp_ref, kc_ref, kn_ref, vp_ref, vc_ref, vn_ref, kx_ref, vx_ref, sink_ref = refs[:10]
    z_refs = refs[10:10 + n_z]
    o_ref = refs[10 + n_z]
    i = pl.program_id(1)
    blk = ATTN_BLOCK
    rows = group * blk
    zw = z_refs[0].shape[1]
    qpos = lax.broadcasted_iota(jnp.int32, (rows, blk), 0) % blk
    kpos = lax.broadcasted_iota(jnp.int32, (rows, blk), 1)
    keep_prev = (kpos >= qpos) & (i > 0)
    keep_next = (kpos <= qpos) & (i < pl.num_programs(1) - 1)
    sink2 = sink_ref[...] * math.log2(math.e)
    for n in range(kv_heads):
        ks = slice(n * LANES, (n + 1) * LANES)
        q4 = jnp.concatenate([q_ref[:, (n * group + g) * LANES:(n * group + g + 1) * LANES] for g in range(group)],
                             axis=0)
        k_all = jnp.concatenate([kp_ref[:, ks], kc_ref[:, ks], kn_ref[:, ks], kx_ref[:, ks]], axis=0)
        v_all = jnp.concatenate([vp_ref[:, ks], vc_ref[:, ks], vn_ref[:, ks], vx_ref[:, ks]], axis=0)
        s = _lane_chunks(_dot_nt(q4, k_all))
        s[0] = jnp.where(keep_prev, s[0], NEG_INF)
        s[2] = jnp.where(keep_next, s[2], NEG_INF)
        sink_b = jnp.concatenate([jnp.broadcast_to(sink2[n, g:g + 1, :], (blk, LANES)) for g in range(group)], axis=0)
        mx = sink_b
        for t in s:
            mx = jnp.maximum(mx, t)
        m = jnp.broadcast_to(jnp.max(mx, axis=-1, keepdims=True), (rows, LANES))
        ps = [jnp.exp2(t - m) for t in s]
        lsum = ps[0]
        for t in ps[1:]:
            lsum = lsum + t
        l = jnp.sum(lsum, axis=-1, keepdims=True) + jnp.exp2(sink_b - m)[:, 0:1]
        o = _dot(jnp.concatenate(ps, axis=1).astype(BF16), v_all) / l
        for g in range(group):
            col = (n * group + g) * LANES
            z = z_refs[col // zw][:, col % zw:col % zw + LANES].astype(F32)
            o_ref[:, col:col + LANES] = (o[g * blk:(g + 1) * blk] * jax.nn.silu(z)).astype(BF16)


def _win_attention(qkvz, seq, batch, ctx_kv, ctx_len, di, kv_heads, sink):
    heads = di // LANES
    group = heads // kv_heads
    kvw = kv_heads * LANES
    blk = ATTN_BLOCK
    nb = seq // blk
    z_col0 = di + 2 * kvw
    zw = math.gcd(z_col0, di)
    n_z = di // zw
    assert di % kvw == 0
    sink_rep = jnp.broadcast_to(sink.astype(F32).reshape(kv_heads, group, 1), (kv_heads, group, LANES))
    band = lambda col_block, shift: pl.BlockSpec(
        (blk, kvw), lambda b, i: (b * nb + jnp.clip(i + shift, 0, nb - 1), col_block))
    k_blk, v_blk = di // kvw, di // kvw + 1
    z_spec = lambda t: pl.BlockSpec((blk, zw), lambda b, i: (b * nb + i, z_col0 // zw + t))
    return pl.pallas_call(
        functools.partial(_win_attn_kernel, group=group, kv_heads=kv_heads, n_z=n_z),
        grid=(batch, nb),
        in_specs=[
            pl.BlockSpec((blk, di), lambda b, i: (b * nb + i, 0)),
            band(k_blk, -1), band(k_blk, 0), band(k_blk, 1),
            band(v_blk, -1), band(v_blk, 0), band(v_blk, 1),
            pl.BlockSpec((ctx_len, kvw), lambda b, i: (b, 0)),
            pl.BlockSpec((ctx_len, kvw), lambda b, i: (b, 1)),
            pl.BlockSpec((kv_heads, group, LANES), lambda b, i: (0, 0, 0)),
        ] + [z_spec(t) for t in range(n_z)],
        out_specs=pl.BlockSpec((blk, di), lambda b, i: (b * nb + i, 0)),
        out_shape=jax.ShapeDtypeStruct((batch * seq, di), BF16),
        compiler_params=_params("parallel", "arbitrary"),
        name="win_attention",
    )(*([qkvz] * 7 + [ctx_kv, ctx_kv, sink_rep] + [qkvz] * n_z))


class _Stream(NamedTuple):
    seq: int
    span: int
    mod_row: Callable


def kernel(x, c, ctx, c_ctx, l0_norm, l0_w_mod, l0_b_mod, l0_w_in, l0_conv_w, l0_conv_b, l0_w_out, l1_norm, l1_w_mod, l1_b_mod, l1_w_in, l1_q_norm, l1_k_norm, l1_lam_q1, l1_lam_k1, l1_lam_q2, l1_lam_k2, l1_sub_norm, l1_w_out, l2_norm, l2_w_mod, l2_b_mod, l2_w_in, l2_q_norm, l2_k_norm, l2_sink, l2_w_out, l3_norm, l3_w_mod, l3_b_mod, l3_w_in, l3_conv_w, l3_conv_b, l3_w_out):
    batch, seq, d = x.shape
    ctx_len = ctx.shape[1]
    di = l0_w_out.shape[0]
    assert batch < MOD_ROWS
    xs = _Stream(seq, seq, lambda row: row // seq)
    cs = _Stream(ctx_len, batch * ctx_len, lambda row: batch)

    x2 = x.reshape(batch * seq, d)
    c2 = ctx.reshape(batch * ctx_len, d)
    cc = jnp.zeros((MOD_ROWS, d), F32).at[:batch].set(c).at[batch].set(c_ctx)
    mod0, mod1, mod2, mod3 = (_modulation(cc, w, b) for w, b in
                              ((l0_w_mod, l0_b_mod), (l1_w_mod, l1_b_mod), (l2_w_mod, l2_b_mod), (l3_w_mod, l3_b_mod)))
    bf = lambda w: w.astype(BF16)

    w_in, w_out = bf(l0_w_in), bf(l0_w_out)
    x2n = _conv_layer(x2, xs, mod0, l0_norm, w_in, l0_conv_w, l0_conv_b, w_out)
    c2 = _conv_layer(c2, cs, mod0, l0_norm, w_in, l0_conv_w, l0_conv_b, w_out)
    x2 = x2n

    w_in, w_out = bf(l1_w_in), bf(l1_w_out)
    lam_init = 0.8 - 0.6 * math.exp(-0.3 * 1)
    q_scale = l1_q_norm.shape[0] ** -0.5 * math.log2(math.e)
    proj = functools.partial(_proj_layer, norm_g=l1_norm, w_in=w_in, q_norm=l1_q_norm, k_norm=l1_k_norm,
                             q_cols=di, k_cols=di, q_scale=q_scale)
    qx = proj(x2, xs, mod1, rope=True)
    qc = proj(c2, cs, mod1, rope=False)
    lam_vecs = jnp.stack([l1_lam_q1, l1_lam_k1, l1_lam_q2, l1_lam_k2]).astype(F32)
    ax = _diff_attention(qx, seq, batch, qc, ctx_len, lam_vecs, l1_sub_norm, lam_init)
    ac = _diff_self_attention(qc, ctx_len, batch, lam_vecs, l1_sub_norm, lam_init)
    x2 = _out_layer(ax, x2, xs, mod1, w_out)
    c2 = _out_layer(ac, c2, cs, mod1, w_out)

    w_in, w_out = bf(l2_w_in), bf(l2_w_out)
    kv_cols = (l2_w_in.shape[1] - 2 * di) // 2
    proj = functools.partial(_proj_layer, norm_g=l2_norm, w_in=w_in, q_norm=l2_q_norm, k_norm=l2_k_norm,
                             q_cols=di, k_cols=kv_cols, q_scale=l2_q_norm.shape[0] ** -0.5 * math.log2(math.e))
    qx = proj(x2, xs, mod2, rope=True)
    kvc = proj(c2, cs, mod2, rope=False, q_cols=0, first_col=di, n=2 * kv_cols)
    ax = _win_attention(qx, seq, batch, kvc, ctx_len, di, kv_cols // LANES, l2_sink)
    x2 = _out_layer(ax, x2, xs, mod2, w_out)

    x2 = _conv_layer(x2, xs, mod3, l3_norm, bf(l3_w_in), l3_conv_w, l3_conv_b, bf(l3_w_out))
    return x2.reshape(batch, seq, d)
```
